```python
import functools
import jax, jax.numpy as jnp
from jax import lax
import numpy as np

D_MODEL = 1024
BATCH = 4
SEQ = 4096
DEPTH = 2
DEC_BATCH = 32
DEC_SEQ = 1
PAST_LEN = 8192
PAGE_SIZE = 128

HEAD_DIM = 128
N_ATT_GROUPS = 3
HEADS_PER_GROUP = 4
WINDOWS = (128, 512, 2048)
DILATIONS = (1, 4, 16)
N_STEPS = 128
N_ATT_HEADS = N_ATT_GROUPS * HEADS_PER_GROUP
ATT_QKV = N_ATT_HEADS * HEAD_DIM
ATT_OUT = HEADS_PER_GROUP * HEAD_DIM
CHUNK = 128
SGU_WIDTH = 512
SGU_GROUPS = 8
SGU_GROUP_DIM = SGU_WIDTH // SGU_GROUPS
D_FF = -(-8 * D_MODEL // (3 * 256)) * 256
SPLITS = (ATT_QKV, 2 * ATT_QKV, 3 * ATT_QKV, 3 * ATT_QKV + SGU_WIDTH,
          3 * ATT_QKV + 2 * SGU_WIDTH, 3 * ATT_QKV + 2 * SGU_WIDTH + D_MODEL)
D_IN = 3 * ATT_QKV + 2 * SGU_WIDTH + 2 * D_MODEL
N_ADA = 6
EPS = 1e-6
NEG_INF = -1e30

kernel_name = "dilated_attn_sgu_gated_hybrid_step"


def rms_norm(x, w):
    xf = x.astype(jnp.float32)
    y = xf * lax.rsqrt(jnp.mean(xf * xf, axis=-1, keepdims=True) + EPS)
    return (y * w.astype(jnp.float32)).astype(x.dtype)


def layer_norm(x, w, b):
    xf = x.astype(jnp.float32)
    mu = jnp.mean(xf, axis=-1, keepdims=True)
    var = jnp.mean(jnp.square(xf - mu), axis=-1, keepdims=True)
    y = (xf - mu) * lax.rsqrt(var + EPS)
    return (y * w.astype(jnp.float32) + b.astype(jnp.float32)).astype(x.dtype)


def alibi_slopes():
    h = np.arange(1, N_ATT_HEADS + 1, dtype=np.float32)
    s = np.power(np.float32(2.0), -8.0 * h / N_ATT_HEADS).astype(np.float32)
    return jnp.asarray(s.reshape(N_ATT_GROUPS, HEADS_PER_GROUP))


def dilated_band_attention(q, k, v, slopes, dil):
    B, S, H, Dh = q.shape
    L = S // dil
    N = B * dil
    nb = -(-L // N_STEPS)
    Lp = nb * N_STEPS

    def to_blocks(t):
        t = t.reshape(B, L, dil, H, Dh).transpose(0, 2, 1, 3, 4).reshape(N, L, H, Dh)
        t = jnp.pad(t, ((0, 0), (0, Lp - L), (0, 0), (0, 0)))
        return t.reshape(N, nb, N_STEPS, H, Dh)

    def with_prev(t):
        prev = jnp.pad(t, ((0, 0), (1, 0), (0, 0), (0, 0), (0, 0)))[:, :-1]
        return jnp.concatenate([prev, t], axis=2)

    qb = to_blocks(q)
    kk = with_prev(to_blocks(k))
    vv = with_prev(to_blocks(v))
    s = jnp.einsum('nbqhd,nbkhd->nbhqk', qb, kk).astype(jnp.float32) * (HEAD_DIM ** -0.5)
    qi = jnp.arange(N_STEPS)[:, None]
    ki = jnp.arange(2 * N_STEPS)[None, :]
    dist = N_STEPS + qi - ki
    key_step = (jnp.arange(nb)[:, None, None] - 1) * N_STEPS + ki[None]
    valid = (dist >= 0) & (dist <= N_STEPS) & (key_step >= 0)
    bias = -slopes[:, None, None] * (dist * dil).astype(jnp.float32)
    s = jnp.where(valid[None, :, None], s + bias[None, None], NEG_INF)
    lse = jax.nn.logsumexp(s, axis=-1)
    p = jnp.exp(s - lse[..., None])
    o = jnp.einsum('nbhqk,nbkhd->nbqhd', p.astype(v.dtype), vv)

    def from_blocks(t):
        t = t.reshape(N, Lp, *t.shape[3:])[:, :L]
        t = t.reshape(B, dil, L, *t.shape[2:])
        t = jnp.swapaxes(t, 1, 2)
        return t.reshape(B, S, *t.shape[3:])

    return from_blocks(o), from_blocks(jnp.swapaxes(lse, 2, 3))


def dilated_cached_attention(q, k_new, v_new, kv_cache, slopes, dil):
    Lc = kv_cache.shape[1]
    T = q.shape[1]
    k_all = jnp.concatenate([kv_cache[:, :, 0].astype(k_new.dtype), k_new], axis=1)
    v_all = jnp.concatenate([kv_cache[:, :, 1].astype(v_new.dtype), v_new], axis=1)
    steps = jnp.arange(N_STEPS + 1)
    idx = Lc + jnp.arange(T)[:, None] - dil * steps[None, :]
    valid = idx >= 0
    idx = jnp.maximum(idx, 0)
    kg = k_all[:, idx]
    vg = v_all[:, idx]
    s = jnp.einsum('bthd,btjhd->bthj', q, kg).astype(jnp.float32) * (HEAD_DIM ** -0.5)
    s = s - slopes[:, None] * (dil * steps).astype(jnp.float32)
    s = jnp.where(valid[:, None, :], s, NEG_INF)
    lse = jax.nn.logsumexp(s, axis=-1)
    p = jnp.exp(s - lse[..., None])
    o = jnp.einsum('bthj,btjhd->bthd', p.astype(vg.dtype), vg)
    return o, lse


def merge_by_denominator(outs, lses):
    alpha = jax.nn.softmax(jnp.stack(lses, axis=0), axis=0)
    return jnp.einsum('gnth,gnthd->nthd', alpha.astype(outs[0].dtype), jnp.stack(outs, axis=0))


def attend_prompt(q, k, v):
    slopes = alibi_slopes()
    S = q.shape[1]
    outs, lses, states = [], [], []
    for g in range(N_ATT_GROUPS):
        o, lse = dilated_band_attention(q[:, :, g], k[:, :, g], v[:, :, g], slopes[g], DILATIONS[g])
        outs.append(o)
        lses.append(lse)
        lw = min(WINDOWS[g], S)
        states.append(jnp.stack([k[:, S - lw:, g], v[:, S - lw:, g]], axis=2))
    return merge_by_denominator(outs, lses), states


def attend_sample(q, k, v, cache1, cache2, cache3):
    slopes = alibi_slopes()
    caches = (cache1, cache2, cache3)
    outs, lses, states = [], [], []
    for g in range(N_ATT_GROUPS):
        o, lse = dilated_cached_attention(q[:, :, g], k[:, :, g], v[:, :, g], caches[g], slopes[g], DILATIONS[g])
        outs.append(o)
        lses.append(lse)
        states.append(jnp.stack([k[:, :, g], v[:, :, g]], axis=2))
    return merge_by_denominator(outs, lses), states


def sgu_prompt(vs, w_s, b_s):
    B, S, _ = vs.shape
    vr = vs.reshape(B, S // CHUNK, CHUNK, SGU_GROUPS, SGU_GROUP_DIM)
    wm = w_s * jnp.tril(jnp.ones((CHUNK, CHUNK), w_s.dtype))
    mixed = jnp.einsum('gts,bnsgc->bntgc', wm, vr) + b_s.T[:, :, None]
    return mixed.reshape(B, S, SGU_WIDTH)


def sgu_sample(vs, w_s, b_s):
    Bd, T, _ = vs.shape
    vr = vs.reshape(Bd, T, SGU_GROUPS, SGU_GROUP_DIM)
    wm = (w_s * jnp.tril(jnp.ones((CHUNK, CHUNK), w_s.dtype)))[:, :T, :T]
    mixed = jnp.einsum('gts,bsgc->btgc', wm, vr) + b_s[:, :T].T[:, :, None]
    return mixed.reshape(Bd, T, SGU_WIDTH)


def trunk_layer(x, c, attend, sgu, w_ada, b_ada, norm1_w, w_in, q_norm_w, k_norm_w,
                sgu_ln_w, sgu_ln_b, w_proj_att, w_proj_sgu, w_out, norm2_w, w_ffn_in, w_ffn_out):
    n, t = x.shape[0], x.shape[1]
    mod = (jax.nn.silu(c) @ w_ada + b_ada)[:, None, :]
    sh1, sc1, g1, sh2, sc2, g2 = jnp.split(mod, N_ADA, axis=-1)
    h = rms_norm(x, norm1_w) * (1 + sc1) + sh1
    z = h @ w_in
    q, k, v, u, vs, ga, gb = jnp.split(z, SPLITS, axis=-1)
    hs = (n, t, N_ATT_GROUPS, HEADS_PER_GROUP, HEAD_DIM)
    q = rms_norm(q.reshape(hs), q_norm_w[:, None, :])
    k = rms_norm(k.reshape(hs), k_norm_w[:, None, :])
    v = v.reshape(hs)
    y_att, att_state = attend(q, k, v)
    u = jax.nn.gelu(u, approximate=False)
    vs = layer_norm(jax.nn.gelu(vs, approximate=False), sgu_ln_w, sgu_ln_b)
    y_sgu = u * sgu(vs)
    merged = (jax.nn.sigmoid(ga) * (y_att.reshape(n, t, ATT_OUT) @ w_proj_att)
              + jax.nn.sigmoid(gb) * (y_sgu @ w_proj_sgu))
    x = x + g1 * (merged @ w_out)
    h2 = rms_norm(x, norm2_w) * (1 + sc2) + sh2
    a, b = jnp.split(h2 @ w_ffn_in, 2, axis=-1)
    x = x + g2 * ((jax.nn.silu(a) * b) @ w_ffn_out)
    return x, att_state, vs


def setup_inputs(seed: int = 0) -> dict:
    key = jax.random.key(seed)
    ks = jax.random.split(key, 24)
    f32 = jnp.float32

    def nrm(k, shape, scale):
        return jax.random.normal(k, shape, f32) * scale

    def cache(k, w):
        return nrm(k, (DEPTH, DEC_BATCH, min(w, PAST_LEN), 2, HEADS_PER_GROUP, HEAD_DIM), 1.0)

    return {
        "x_prompt": nrm(ks[0], (BATCH, SEQ, D_MODEL), 1.0),
        "x_sample": nrm(ks[1], (DEC_BATCH, DEC_SEQ, D_MODEL), 1.0),
        "cache_kv_w128": cache(ks[2], WINDOWS[0]),
        "cache_kv_w512": cache(ks[3], WINDOWS[1]),
        "cache_kv_w2048": cache(ks[4], WINDOWS[2]),
        "c_prompt": nrm(ks[5], (BATCH, D_MODEL), 1.0),
        "c_sample": nrm(ks[6], (DEC_BATCH, D_MODEL), 1.0),
        "w_ada": nrm(ks[7], (DEPTH, D_MODEL, N_ADA * D_MODEL), 0.5 * D_MODEL ** -0.5),
        "b_ada": nrm(ks[8], (DEPTH, N_ADA * D_MODEL), 0.01),
        "norm1_w": 1.0 + nrm(ks[9], (DEPTH, D_MODEL), 0.02),
        "w_in": nrm(ks[10], (DEPTH, D_MODEL, D_IN), D_MODEL ** -0.5),
        "q_norm_w": 1.0 + nrm(ks[11], (DEPTH, N_ATT_GROUPS, HEAD_DIM), 0.02),
        "k_norm_w": 1.0 + nrm(ks[12], (DEPTH, N_ATT_GROUPS, HEAD_DIM), 0.02),
        "sgu_ln_w": 1.0 + nrm(ks[13], (DEPTH, SGU_WIDTH), 0.02),
        "sgu_ln_b": nrm(ks[14], (DEPTH, SGU_WIDTH), 0.01),
        "w_spatial": nrm(ks[15], (DEPTH, SGU_GROUPS, CHUNK, CHUNK), CHUNK ** -0.5),
        "b_spatial": 1.0 + nrm(ks[16], (DEPTH, SGU_GROUPS, CHUNK), 0.1),
        "w_proj_att": nrm(ks[17], (DEPTH, ATT_OUT, D_MODEL), ATT_OUT ** -0.5),
        "w_proj_sgu": nrm(ks[18], (DEPTH, SGU_WIDTH, D_MODEL), SGU_WIDTH ** -0.5),
        "w_out": nrm(ks[19], (DEPTH, D_MODEL, D_MODEL), D_MODEL ** -0.5),
        "norm2_w": 1.0 + nrm(ks[20], (DEPTH, D_MODEL), 0.02),
        "w_ffn_in": nrm(ks[21], (DEPTH, D_MODEL, 2 * D_FF), D_MODEL ** -0.5),
        "w_ffn_out": nrm(ks[22], (DEPTH, D_FF, D_MODEL), D_FF ** -0.5),
    }


def reference(x_prompt, x_sample, cache_kv_w128, cache_kv_w512, cache_kv_w2048, c_prompt, c_sample,
              w_ada, b_ada, norm1_w, w_in, q_norm_w, k_norm_w, sgu_ln_w, sgu_ln_b, w_spatial, b_spatial,
              w_proj_att, w_proj_sgu, w_out, norm2_w, w_ffn_in, w_ffn_out):
    def layer_weights(l):
        return (w_ada[l], b_ada[l], norm1_w[l], w_in[l], q_norm_w[l], k_norm_w[l], sgu_ln_w[l], sgu_ln_b[l],
                w_proj_att[l], w_proj_sgu[l], w_out[l], norm2_w[l], w_ffn_in[l], w_ffn_out[l])

    xp = x_prompt
    p_kv1, p_kv2, p_kv3 = [], [], []
    for l in range(DEPTH):
        sgu = functools.partial(sgu_prompt, w_s=w_spatial[l], b_s=b_spatial[l])
        xp, st, _ = trunk_layer(xp, c_prompt, attend_prompt, sgu, *layer_weights(l))
        p_kv1.append(st[0])
        p_kv2.append(st[1])
        p_kv3.append(st[2])

    xs = x_sample
    s_kv1, s_kv2, s_kv3, s_v = [], [], [], []
    for l in range(DEPTH):
        attend = functools.partial(attend_sample, cache1=cache_kv_w128[l], cache2=cache_kv_w512[l],
                                   cache3=cache_kv_w2048[l])
        sgu = functools.partial(sgu_sample, w_s=w_spatial[l], b_s=b_spatial[l])
        xs, st, vrows = trunk_layer(xs, c_sample, attend, sgu, *layer_weights(l))
        s_kv1.append(st[0])
        s_kv2.append(st[1])
        s_kv3.append(st[2])
        s_v.append(vrows)

    return (xp, xs,
            jnp.stack(p_kv1), jnp.stack(p_kv2), jnp.stack(p_kv3),
            jnp.stack(s_kv1), jnp.stack(s_kv2), jnp.stack(s_kv3),
            jnp.stack(s_v))
```

```python
import functools
import math

import jax
import jax.numpy as jnp
import numpy as np
from jax.experimental import pallas as pl
from jax.experimental.pallas import tpu as pltpu

HEAD_DIM = 128
N_GROUPS = 3
HEADS = 4
GROUP_COLS = HEADS * HEAD_DIM
QKV_COLS = N_GROUPS * GROUP_COLS
WINDOWS = (128, 512, 2048)
DILATIONS = (1, 4, 16)
N_STEPS = 128
CHUNK = 128
SGU_WIDTH = 512
SGU_GROUPS = 8
SGU_GROUP_DIM = SGU_WIDTH // SGU_GROUPS
N_ADA = 6
EPS = 1e-6
NEG_INF = -1e30
QK_SCALE = HEAD_DIM ** -0.5

LANES = 128
VMEM_LIMIT_BYTES = 56 * 1024 * 1024

BF16 = jnp.bfloat16
F32 = jnp.float32


def _params(n_axes):
    return pltpu.CompilerParams(dimension_semantics=("arbitrary",) * n_axes,
                                vmem_limit_bytes=VMEM_LIMIT_BYTES)


def _resident(shape):
    nd = len(shape)
    return pl.BlockSpec(shape, lambda *_: (0,) * nd, pipeline_mode=pl.Buffered(1))


def _dot(a, b):
    return jnp.dot(a, b, preferred_element_type=F32)


def _dot_nt(a, b):
    return jax.lax.dot_general(a, b, (((1,), (1,)), ((), ())), preferred_element_type=F32)


def _rms(x):
    return x * jax.lax.rsqrt(jnp.mean(x * x, axis=-1, keepdims=True) + EPS)


def _gelu(x):
    return 0.5 * x * (1.0 + jax.lax.erf(x * (2.0 ** -0.5)))


def _sigmoid(x):
    return 1.0 / (1.0 + jnp.exp(-x))


def _ada_kernel(c_ref, w_ref, b_ref, o_ref):
    c = c_ref[...]
    a = c * _sigmoid(c)
    o_ref[...] = jnp.dot(a, w_ref[...], preferred_element_type=F32,
                         precision=jax.lax.Precision.HIGHEST) + b_ref[...]


def _ada_call(c_all, w_ada, b_ada):
    depth, d, n = w_ada.shape
    rows = c_all.shape[0]
    tn = 1024
    return pl.pallas_call(
        _ada_kernel,
        grid=(depth, n // tn),
        in_specs=[pl.BlockSpec((rows, d), lambda l, j: (0, 0)),
                  pl.BlockSpec((None, d, tn), lambda l, j: (l, 0, j)),
                  pl.BlockSpec((None, 1, tn), lambda l, j: (l, 0, j))],
        out_specs=pl.BlockSpec((None, rows, tn), lambda l, j: (l, 0, j)),
        out_shape=jax.ShapeDtypeStruct((depth, rows, n), F32),
        compiler_params=_params(2),
        name="adaln_mod",
    )(c_all, w_ada, b_ada.reshape(depth, 1, n))


def _inproj_kernel(x_ref, mod_ref, n1w_ref, w_ref, qw_ref, kw_ref, lnw_ref, lnb_ref,
                   *refs, tm, d_model, dils, state_rows, tiles_per_seq, deinterleave):
    (q0, k0, v0, q1, k1, v1, q2, k2, v2, u_ref, vs_ref, ga_ref, gb_ref,
     st0, st1, st2, h_scr, slab) = refs
    q_refs, k_refs, v_refs, st_refs = (q0, q1, q2), (k0, k1, k2), (v0, v1, v2), (st0, st1, st2)
    ti = pl.program_id(0) % tiles_per_seq

    mod = mod_ref[0]
    sh1 = mod[:, 0:d_model]
    sc1 = mod[:, d_model:2 * d_model]
    h = _rms(x_ref[...]) * n1w_ref[...] * (1.0 + sc1) + sh1
    h_scr[...] = h.astype(BF16)

    def proj(c0, width):
        return _dot(h_scr[...], w_ref[:, c0:c0 + width])

    def put(ref, g, val):
        d = dils[g]
        if d == 1 or not deinterleave:
            ref[...] = val.astype(ref.dtype)
            return
        for hh in range(HEADS):
            slab[hh] = val[:, hh * HEAD_DIM:(hh + 1) * HEAD_DIM]
        for r in range(d):
            for hh in range(HEADS):
                ref[0, r, :, hh * HEAD_DIM:(hh + 1) * HEAD_DIM] = (
                    slab[hh, pl.ds(r, tm // d, stride=d), :].astype(ref.dtype))

    def head_rms(z, w):
        parts = [_rms(z[:, hh * HEAD_DIM:(hh + 1) * HEAD_DIM]) for hh in range(HEADS)]
        return jnp.concatenate(parts, axis=-1) * w

    def put_state(g, kn, v):
        st, rows = st_refs[g], state_rows[g]
        if rows is None:
            st[:, 0:GROUP_COLS] = kn
            st[:, GROUP_COLS:2 * GROUP_COLS] = v
        elif rows < tm:
            @pl.when(ti == tiles_per_seq - 1)
            def _():
                st[:, 0:GROUP_COLS] = kn[tm - rows:, :]
                st[:, GROUP_COLS:2 * GROUP_COLS] = v[tm - rows:, :]
        else:
            @pl.when(ti >= tiles_per_seq - rows // tm)
            def _():
                st[:, 0:GROUP_COLS] = kn
                st[:, GROUP_COLS:2 * GROUP_COLS] = v

    for g in range(N_GROUPS):
        c0 = g * GROUP_COLS
        q = head_rms(proj(c0, GROUP_COLS), qw_ref[:, c0:c0 + GROUP_COLS] * QK_SCALE)
        put(q_refs[g], g, q)
        kn = head_rms(proj(QKV_COLS + c0, GROUP_COLS), kw_ref[:, c0:c0 + GROUP_COLS])
        put(k_refs[g], g, kn)
        v = proj(2 * QKV_COLS + c0, GROUP_COLS)
        put(v_refs[g], g, v)
        put_state(g, kn, v)

    c0 = 3 * QKV_COLS
    u_ref[...] = _gelu(proj(c0, SGU_WIDTH)).astype(u_ref.dtype)
    vs = _gelu(proj(c0 + SGU_WIDTH, SGU_WIDTH))
    mu = jnp.mean(vs, axis=-1, keepdims=True)
    vc = vs - mu
    var = jnp.mean(vc * vc, axis=-1, keepdims=True)
    vs_ref[...] = (vc * jax.lax.rsqrt(var + EPS) * lnw_ref[...] + lnb_ref[...]).astype(vs_ref.dtype)
    c0 += 2 * SGU_WIDTH
    ga_ref[...] = _sigmoid(proj(c0, d_model)).astype(ga_ref.dtype)
    gb_ref[...] = _sigmoid(proj(c0 + d_model, d_model)).astype(gb_ref.dtype)


def _inproj_call(x, mod, n1w, w_in, qw, kw, lnw, lnb, *, seq, tm, deinterleave, act_dtype):
    t, d_model = x.shape
    batch = t // seq
    tiles_per_seq = seq // tm
    d_in = w_in.shape[1]

    def tok(cols, dtype):
        return (jax.ShapeDtypeStruct((t, cols), dtype), pl.BlockSpec((tm, cols), lambda i: (i, 0)))

    def grp(g, dtype):
        d = DILATIONS[g]
        if d == 1 or not deinterleave:
            return tok(GROUP_COLS, dtype)
        assert tm % d == 0 and (tm // d) % 16 == 0
        return (jax.ShapeDtypeStruct((batch, d, seq // d, GROUP_COLS), dtype),
                pl.BlockSpec((1, d, tm // d, GROUP_COLS),
                             lambda i: (i // tiles_per_seq, 0, i % tiles_per_seq, 0)))

    outs = []
    for g in range(N_GROUPS):
        outs += [grp(g, act_dtype)] * 3
    outs += [tok(SGU_WIDTH, act_dtype), tok(SGU_WIDTH, act_dtype if deinterleave else F32),
             tok(d_model, act_dtype), tok(d_model, act_dtype)]

    state_rows = []
    for g in range(N_GROUPS):
        if not deinterleave:
            state_rows.append(None)
            outs.append(tok(2 * GROUP_COLS, F32))
            continue
        rows = min(WINDOWS[g], seq)
        state_rows.append(rows)
        shape = jax.ShapeDtypeStruct((batch * rows, 2 * GROUP_COLS), F32)
        if rows < tm:
            spec = pl.BlockSpec((rows, 2 * GROUP_COLS), lambda i: (i // tiles_per_seq, 0))
        else:
            assert rows % tm == 0
            nblk = rows // tm
            spec = pl.BlockSpec(
                (tm, 2 * GROUP_COLS),
                lambda i, nblk=nblk: ((i // tiles_per_seq) * nblk
                                      + jnp.maximum(i % tiles_per_seq - (tiles_per_seq - nblk), 0), 0))
        outs.append((shape, spec))

    kern = functools.partial(_inproj_kernel, tm=tm, d_model=d_model, dils=DILATIONS,
                             state_rows=tuple(state_rows), tiles_per_seq=tiles_per_seq,
                             deinterleave=deinterleave)
    return pl.pallas_call(
        kern,
        grid=(t // tm,),
        in_specs=[pl.BlockSpec((tm, d_model), lambda i: (i, 0)),
                  pl.BlockSpec((1,) + mod.shape[1:], lambda i: (i // tiles_per_seq, 0, 0)),
                  _resident((1, d_model)),
                  _resident((d_model, d_in)),
                  _resident((1, QKV_COLS)), _resident((1, QKV_COLS)),
                  _resident((1, SGU_WIDTH)), _resident((1, SGU_WIDTH))],
        out_specs=[o[1] for o in outs],
        out_shape=[o[0] for o in outs],
        scratch_shapes=[pltpu.VMEM((tm, d_model), BF16),
                        pltpu.VMEM((HEADS, tm, LANES), F32)],
        compiler_params=_params(1),
        name="inproj_deint" if deinterleave else "inproj_tok",
    )(x, mod, n1w, w_in, qw, kw, lnw, lnb)


def _alibi_slopes():
    h = np.arange(1, N_GROUPS * HEADS + 1, dtype=np.float32)
    s = np.power(np.float32(2.0), -8.0 * h / (N_GROUPS * HEADS)).astype(np.float32)
    return s.reshape(N_GROUPS, HEADS)


def _band_bias():
    qi = np.arange(N_STEPS)[:, None]
    ki = np.arange(2 * N_STEPS)[None, :]
    dist = N_STEPS + qi - ki
    valid = (dist >= 0) & (dist <= N_STEPS)
    slopes = _alibi_slopes()
    out = np.empty((N_GROUPS, HEADS, 2, N_STEPS, 2 * N_STEPS), np.float32)
    for g in range(N_GROUPS):
        for hh in range(HEADS):
            b = np.where(valid, -slopes[g, hh] * (dist * DILATIONS[g]).astype(np.float32),
                         np.float32(NEG_INF)).astype(np.float32)
            out[g, hh, 1] = b
            first = b.copy()
            first[:, :N_STEPS] = NEG_INF
            out[g, hh, 0] = first
    return out


def _attn_block(q, kp, ko, vp, vo, bias_p, bias_o):
    sp = _dot_nt(q, kp) + bias_p
    so = _dot_nt(q, ko) + bias_o
    m = jnp.maximum(jnp.max(sp, axis=-1, keepdims=True), jnp.max(so, axis=-1, keepdims=True))
    pp = jnp.exp(sp - m)
    po = jnp.exp(so - m)
    l = jnp.sum(pp, axis=-1, keepdims=True) + jnp.sum(po, axis=-1, keepdims=True)
    acc = _dot(pp.astype(vp.dtype), vp) + _dot(po.astype(vo.dtype), vo)
    return acc / l, m + jnp.log(l)


def _attn_kernel(q0, k0, v0, q1, k1, v1, q2, k2, v2, bias_ref, o_ref, o_scr, lse_scr, *, seq):
    blk = N_STEPS

    def run_group(g, qr, kr, vr):
        d = DILATIONS[g]
        nblk = seq // d // blk

        def body(it, carry):
            r = it // nblk
            i = it % nblk
            own = pl.multiple_of(i * blk, blk)
            prev = pl.multiple_of(jnp.maximum(i - 1, 0) * blk, blk)
            if d == 1:
                ld = lambda ref, start: ref[pl.ds(start, blk), :]
            else:
                ld = lambda ref, start: ref[0, r, pl.ds(start, blk), :]
            sel = jnp.minimum(i, 1)
            o, lse = _attn_block(ld(qr, own), ld(kr, prev), ld(kr, own), ld(vr, prev), ld(vr, own),
                                 bias_ref[g, 0, sel, :, 0:blk], bias_ref[g, 0, sel, :, blk:2 * blk])
            lse_b = jnp.broadcast_to(lse, (blk, LANES))
            if d == 1:
                o_scr[g, pl.ds(own, blk), :] = o
                lse_scr[g, pl.ds(own, blk), :] = lse_b
            else:
                start = i * (blk * d) + r
                o_scr[g, pl.ds(start, blk, stride=d), :] = o
                lse_scr[g, pl.ds(start, blk, stride=d), :] = lse_b
            return carry

        jax.lax.fori_loop(0, d * nblk, body, 0)

    run_group(0, q0, k0, v0)
    run_group(1, q1, k1, v1)
    run_group(2, q2, k2, v2)

    def merge(i, carry):
        rows = pl.ds(pl.multiple_of(i * blk, blk), blk)
        lses = [lse_scr[g, rows, :] for g in range(N_GROUPS)]
        mx = jnp.maximum(jnp.maximum(lses[0], lses[1]), lses[2])
        ws = [jnp.exp(x - mx) for x in lses]
        num = ws[0] * o_scr[0, rows, :] + ws[1] * o_scr[1, rows, :] + ws[2] * o_scr[2, rows, :]
        o_ref[rows, :] = (num / (ws[0] + ws[1] + ws[2])).astype(o_ref.dtype)
        return carry

    jax.lax.fori_loop(0, seq // blk, merge, 0)


def _attn_call(qkv, bias, *, batch, seq):
    in_specs = []
    for g in range(N_GROUPS):
        d = DILATIONS[g]
        if d == 1:
            spec = pl.BlockSpec((seq, HEAD_DIM), lambda b, h: (b, h))
        else:
            spec = pl.BlockSpec((1, d, seq // d, HEAD_DIM), lambda b, h: (b, 0, 0, h))
        in_specs += [spec] * 3
    in_specs.append(pl.BlockSpec((N_GROUPS, 1, 2, N_STEPS, 2 * N_STEPS), lambda b, h: (0, h, 0, 0, 0)))
    return pl.pallas_call(
        functools.partial(_attn_kernel, seq=seq),
        grid=(batch, HEADS),
        in_specs=in_specs,
        out_specs=pl.BlockSpec((seq, HEAD_DIM), lambda b, h: (b, h)),
        out_shape=jax.ShapeDtypeStruct((batch * seq, GROUP_COLS), BF16),
        scratch_shapes=[pltpu.VMEM((N_GROUPS, seq, LANES), F32),
                        pltpu.VMEM((N_GROUPS, seq, LANES), F32)],
        compiler_params=_params(2),
        name="attn_prompt",
    )(*qkv, bias)


def _attn_sample_kernel(q_ref, st0, st1, st2, c0, c1, c2, bias_ref, o_ref, *, bb):
    st_refs, c_refs = (st0, st1, st2), (c0, c1, c2)
    for b in range(bb):
        outs, lses = [], []
        for g in range(N_GROUPS):
            qrow = q_ref[b:b + 1, g * GROUP_COLS:(g + 1) * GROUP_COLS]
            kn = st_refs[g][b:b + 1, 0:GROUP_COLS]
            vn = st_refs[g][b:b + 1, GROUP_COLS:2 * GROUP_COLS]
            prod = c_refs[g][b, :, 0:GROUP_COLS] * qrow
            self_prod = kn * qrow
            o_h, lse_h = [], []
            for hh in range(HEADS):
                cols = slice(hh * HEAD_DIM, (hh + 1) * HEAD_DIM)
                s = jnp.sum(prod[:, cols], axis=-1, keepdims=True) + bias_ref[g, hh]
                s_self = jnp.sum(self_prod[:, cols], axis=-1, keepdims=True)
                m = jnp.maximum(jnp.max(s, axis=0, keepdims=True), s_self)
                p = jnp.exp(s - m)
                p_self = jnp.exp(s_self - m)
                l = jnp.sum(p, axis=0, keepdims=True) + p_self
                vh = c_refs[g][b, :, GROUP_COLS + hh * HEAD_DIM:GROUP_COLS + (hh + 1) * HEAD_DIM]
                acc = jnp.sum(p * vh, axis=0, keepdims=True) + p_self * vn[:, cols]
                o_h.append(acc / l)
                lse_h.append(jnp.broadcast_to(m + jnp.log(l), (1, HEAD_DIM)))
            outs.append(jnp.concatenate(o_h, axis=-1))
            lses.append(jnp.concatenate(lse_h, axis=-1))
        mx = jnp.maximum(jnp.maximum(lses[0], lses[1]), lses[2])
        ws = [jnp.exp(x - mx) for x in lses]
        num = ws[0] * outs[0] + ws[1] * outs[1] + ws[2] * outs[2]
        o_ref[b:b + 1, :] = (num / (ws[0] + ws[1] + ws[2])).astype(o_ref.dtype)


def _sample_bias():
    slopes = _alibi_slopes()
    steps = (N_STEPS - np.arange(N_STEPS)).astype(np.float32)
    out = np.empty((N_GROUPS, HEADS, N_STEPS, 1), np.float32)
    for g in range(N_GROUPS):
        for hh in range(HEADS):
            out[g, hh, :, 0] = -(steps * np.float32(DILATIONS[g])) * slopes[g, hh]
    return out


def _attn_sample_call(q, states, caches, layer, bias):
    bd = q.shape[0]
    bb = 8
    row = 2 * GROUP_COLS
    in_specs = [pl.BlockSpec((bb, QKV_COLS), lambda i: (i, 0))]
    in_specs += [pl.BlockSpec((bb, row), lambda i: (i, 0))] * 3
    in_specs += [pl.BlockSpec((None, bb, N_STEPS, row), lambda i: (layer, i, 0, 0))] * 3
    in_specs.append(pl.BlockSpec((N_GROUPS, HEADS, N_STEPS, 1), lambda i: (0, 0, 0, 0)))
    return pl.pallas_call(
        functools.partial(_attn_sample_kernel, bb=bb),
        grid=(bd // bb,),
        in_specs=in_specs,
        out_specs=pl.BlockSpec((bb, GROUP_COLS), lambda i: (i, 0)),
        out_shape=jax.ShapeDtypeStruct((bd, GROUP_COLS), BF16),
        compiler_params=_params(1),
        name="attn_sample",
    )(q, *states, *caches, bias)


def _post_kernel(x_ref, mod_ref, ya_ref, u_ref, vs_ref, ga_ref, gb_ref, ws_ref, bs_ref,
                 wpa_ref, wps_ref, wo_ref, n2w_ref, wfi_ref, wfo_ref, o_ref, mix_scr,
                 *, tm, d_model, d_ff, chunked):
    mod = mod_ref[0]
    g1 = mod[:, 2 * d_model:3 * d_model]
    sh2 = mod[:, 3 * d_model:4 * d_model]
    sc2 = mod[:, 4 * d_model:5 * d_model]
    g2 = mod[:, 5 * d_model:6 * d_model]

    if chunked:
        row = jax.lax.broadcasted_iota(jnp.int32, (CHUNK, CHUNK), 0)
        col = jax.lax.broadcasted_iota(jnp.int32, (CHUNK, CHUNK), 1)
        tril = col <= row
        lane = jax.lax.broadcasted_iota(jnp.int32, (CHUNK, LANES), 1)
        low = lane < SGU_GROUP_DIM
        wms = [jnp.where(tril, ws_ref[gi], 0.0).astype(BF16) for gi in range(SGU_GROUPS)]
        for c in range(tm // CHUNK):
            rows = slice(c * CHUNK, (c + 1) * CHUNK)
            for j in range(SGU_WIDTH // LANES):
                cols = slice(j * LANES, (j + 1) * LANES)
                v = vs_ref[rows, cols]
                mix_scr[rows, cols] = jnp.where(low, _dot(wms[2 * j], v), _dot(wms[2 * j + 1], v))
        mixed = mix_scr[...] + jnp.concatenate([bs_ref[...]] * (tm // CHUNK), axis=0)
    else:
        mixed = vs_ref[...] * ws_ref[...] + bs_ref[...]

    ys = (u_ref[...].astype(F32) * mixed).astype(BF16)
    xa = _dot(ya_ref[...], wpa_ref[...])
    xs = _dot(ys, wps_ref[...])
    merged = (ga_ref[...].astype(F32) * xa + gb_ref[...].astype(F32) * xs).astype(BF16)
    x1 = x_ref[...] + g1 * _dot(merged, wo_ref[...])
    h2 = (_rms(x1) * n2w_ref[...] * (1.0 + sc2) + sh2).astype(BF16)
    a = _dot(h2, wfi_ref[:, 0:d_ff])
    b = _dot(h2, wfi_ref[:, d_ff:2 * d_ff])
    act = (a * _sigmoid(a) * b).astype(BF16)
    o_ref[...] = x1 + g2 * _dot(act, wfo_ref[...])


def _post_call(x, mod, ya, u, vs, ga, gb, ws, bs, wpa, wps, wo, n2w, wfi, wfo, *, seq, tm, chunked):
    t, d_model = x.shape
    d_ff = wfo.shape[0]
    tiles_per_seq = seq // tm

    def tok(cols):
        return pl.BlockSpec((tm, cols), lambda i: (i, 0))

    kern = functools.partial(_post_kernel, tm=tm, d_model=d_model, d_ff=d_ff, chunked=chunked)
    return pl.pallas_call(
        kern,
        grid=(t // tm,),
        in_specs=[tok(d_model),
                  pl.BlockSpec((1,) + mod.shape[1:], lambda i: (i // tiles_per_seq, 0, 0)),
                  tok(GROUP_COLS), tok(SGU_WIDTH), tok(SGU_WIDTH), tok(d_model), tok(d_model),
                  _resident(ws.shape), _resident(bs.shape),
                  _resident(wpa.shape), _resident(wps.shape), _resident(wo.shape),
                  _resident(n2w.shape), _resident(wfi.shape), _resident(wfo.shape)],
        out_specs=tok(d_model),
        out_shape=jax.ShapeDtypeStruct((t, d_model), F32),
        scratch_shapes=[pltpu.VMEM((tm, SGU_WIDTH), F32)],
        compiler_params=_params(1),
        name="post_chunked" if chunked else "post_tok",
    )(x, mod, ya, u, vs, ga, gb, ws, bs, wpa, wps, wo, n2w, wfi, wfo)


def kernel(x_prompt, x_sample, cache_kv_w128, cache_kv_w512, cache_kv_w2048, c_prompt, c_sample,
           w_ada, b_ada, norm1_w, w_in, q_norm_w, k_norm_w, sgu_ln_w, sgu_ln_b, w_spatial, b_spatial,
           w_proj_att, w_proj_sgu, w_out, norm2_w, w_ffn_in, w_ffn_out):
    batch, seq, d_model = x_prompt.shape
    bd, dec_seq, _ = x_sample.shape
    depth = w_in.shape[0]
    assert dec_seq == 1 and seq % (N_STEPS * DILATIONS[-1]) == 0
    caches_in = (cache_kv_w128, cache_kv_w512, cache_kv_w2048)
    for g in range(N_GROUPS):
        assert caches_in[g].shape[2] == WINDOWS[g]
    tm = 256

    n_c = batch + bd
    c_all = jnp.concatenate([c_prompt, c_sample], axis=0)
    c_all = jnp.pad(c_all, ((0, -n_c % 8), (0, 0)))
    mod = _ada_call(c_all, w_ada, b_ada)

    band_bias = jnp.asarray(_band_bias())
    sample_bias = jnp.asarray(_sample_bias())
    row = 2 * GROUP_COLS
    caches = [caches_in[g].reshape(depth, bd, N_STEPS, DILATIONS[g] * row) for g in range(N_GROUPS)]

    xp = x_prompt.reshape(batch * seq, d_model)
    xs = x_sample.reshape(bd, d_model)
    p_states = [[] for _ in range(N_GROUPS)]
    s_states = [[] for _ in range(N_GROUPS)]
    s_v = []
    for l in range(depth):
        w_in_l = w_in[l].astype(BF16)
        qw = jnp.tile(q_norm_w[l][:, None, :], (1, HEADS, 1)).reshape(1, QKV_COLS)
        kw = jnp.tile(k_norm_w[l][:, None, :], (1, HEADS, 1)).reshape(1, QKV_COLS)
        lnw, lnb = sgu_ln_w[l].reshape(1, -1), sgu_ln_b[l].reshape(1, -1)
        n1w, n2w = norm1_w[l].reshape(1, -1), norm2_w[l].reshape(1, -1)
        wpa, wps, wo = w_proj_att[l].astype(BF16), w_proj_sgu[l].astype(BF16), w_out[l].astype(BF16)
        wfi, wfo = w_ffn_in[l].astype(BF16), w_ffn_out[l].astype(BF16)
        mod_p = mod[l, :batch].reshape(batch, 1, -1)
        mod_s = mod[l, batch:n_c].reshape(1, bd, -1)
        bs_p = jnp.repeat(b_spatial[l].T, SGU_GROUP_DIM, axis=1)
        ws_s = jnp.repeat(w_spatial[l][:, 0, 0], SGU_GROUP_DIM).reshape(1, SGU_WIDTH)
        bs_s = jnp.repeat(b_spatial[l][:, 0], SGU_GROUP_DIM).reshape(1, SGU_WIDTH)

        outs = _inproj_call(xp, mod_p, n1w, w_in_l, qw, kw, lnw, lnb,
                            seq=seq, tm=tm, deinterleave=True, act_dtype=BF16)
        qkv, (u, vs, ga, gb), sts = outs[:9], outs[9:13], outs[13:]
        ya = _attn_call(qkv, band_bias, batch=batch, seq=seq)
        xp = _post_call(xp, mod_p, ya, u, vs, ga, gb, w_spatial[l], bs_p, wpa, wps, wo, n2w, wfi, wfo,
                        seq=seq, tm=tm, chunked=True)
        for g in range(N_GROUPS):
            p_states[g].append(sts[g].reshape(batch, -1, 2, HEADS, HEAD_DIM))

        outs = _inproj_call(xs, mod_s, n1w, w_in_l, qw, kw, lnw, lnb,
                            seq=bd, tm=bd, deinterleave=False, act_dtype=F32)
        qkv, (u, vs, ga, gb), sts = outs[:9], outs[9:13], outs[13:]
        q_all = jnp.concatenate([qkv[0], qkv[3], qkv[6]], axis=-1)
        ya = _attn_sample_call(q_all, sts, caches, l, sample_bias)
        xs = _post_call(xs, mod_s, ya, u, vs, ga, gb, ws_s, bs_s, wpa, wps, wo, n2w, wfi, wfo,
                        seq=bd, tm=bd, chunked=False)
        for g in range(N_GROUPS):
            s_states[g].append(sts[g].reshape(bd, 1, 2, HEADS, HEAD_DIM))
        s_v.append(vs.reshape(bd, 1, SGU_WIDTH))

    return (xp.reshape(batch, seq, d_model), xs.reshape(bd, 1, d_model),
            jnp.stack(p_states[0]), jnp.stack(p_states[1]), jnp.stack(p_states[2]),
            jnp.stack(s_states[0]), jnp.stack(s_states[1]), jnp.stack(s_states[2]),
            jnp.stack(s_v))
```

```python
import functools
import math

import jax
import jax.numpy as jnp
import numpy as np
from jax.experimental import pallas as pl
from jax.experimental.pallas import tpu as pltpu

HEAD_DIM = 128
N_GROUPS = 3
HEADS = 4
GROUP_COLS = HEADS * HEAD_DIM
QKV_COLS = N_GROUPS * GROUP_COLS
WINDOWS = (128, 512, 2048)
DILATIONS = (1, 4, 16)
N_STEPS = 128
CHUNK = 128
SGU_WIDTH = 512
SGU_GROUPS = 8
SGU_GROUP_DIM = SGU_WIDTH // SGU_GROUPS
N_ADA = 6
EPS = 1e-6
NEG_INF = -1e30
QK_SCALE = HEAD_DIM ** -0.5

LANES = 128
VMEM_LIMIT_BYTES = 56 * 1024 * 1024

BF16 = jnp.bfloat16
F32 = jnp.float32


def _params(n_axes):
    return pltpu.CompilerParams(dimension_semantics=("arbitrary",) * n_axes,
                                vmem_limit_bytes=VMEM_LIMIT_BYTES)


def _resident(shape):
    nd = len(shape)
    return pl.BlockSpec(shape, lambda *_: (0,) * nd, pipeline_mode=pl.Buffered(1))


def _dot(a, b):
    return jnp.dot(a, b, preferred_element_type=F32)


def _dot_nt(a, b):
    return jax.lax.dot_general(a, b, (((1,), (1,)), ((), ())), preferred_element_type=F32)


def _rms(x):
    return x * jax.lax.rsqrt(jnp.mean(x * x, axis=-1, keepdims=True) + EPS)


def _gelu(x):
    return 0.5 * x * (1.0 + jax.lax.erf(x * (2.0 ** -0.5)))


def _sigmoid(x):
    return 1.0 / (1.0 + jnp.exp(-x))


def _ada_kernel(c_ref, w_ref, b_ref, o_ref):
    c = c_ref[...]
    a = c * _sigmoid(c)
    o_ref[...] = jnp.dot(a, w_ref[...], preferred_element_type=F32,
                         precision=jax.lax.Precision.HIGHEST) + b_ref[...]


def _ada_call(c_all, w_ada, b_ada):
    depth, d, n = w_ada.shape
    rows = c_all.shape[0]
    tn = 1024
    return pl.pallas_call(
        _ada_kernel,
        grid=(depth, n // tn),
        in_specs=[pl.BlockSpec((rows, d), lambda l, j: (0, 0)),
                  pl.BlockSpec((None, d, tn), lambda l, j: (l, 0, j)),
                  pl.BlockSpec((None, 1, tn), lambda l, j: (l, 0, j))],
        out_specs=pl.BlockSpec((None, rows, tn), lambda l, j: (l, 0, j)),
        out_shape=jax.ShapeDtypeStruct((depth, rows, n), F32),
        compiler_params=_params(2),
        name="adaln_mod",
    )(c_all, w_ada, b_ada.reshape(depth, 1, n))


def _inproj_kernel(x_ref, mod_ref, n1w_ref, w_ref, qw_ref, kw_ref, lnw_ref, lnb_ref,
                   *refs, tm, d_model, dils, state_rows, tiles_per_seq, deinterleave, n_aliased):
    (q0, k0, v0, q1, k1, v1, q2, k2, v2, u_ref, vs_ref, ga_ref, gb_ref,
     st0, st1, st2, h_scr, slab, kv_stash) = refs[n_aliased:]
    q_refs, k_refs, v_refs, st_refs = (q0, q1, q2), (k0, k1, k2), (v0, v1, v2), (st0, st1, st2)
    ti = pl.program_id(0) % tiles_per_seq

    mod = mod_ref[0]
    sh1 = mod[:, 0:d_model]
    sc1 = mod[:, d_model:2 * d_model]
    h = _rms(x_ref[...]) * n1w_ref[...] * (1.0 + sc1) + sh1
    h_scr[...] = h.astype(BF16)

    def proj(c0, width):
        return _dot(h_scr[...], w_ref[:, c0:c0 + width])

    def put(ref, g, val):
        d = dils[g]
        if d == 1 or not deinterleave:
            ref[...] = val.astype(ref.dtype)
            return
        for hh in range(HEADS):
            slab[hh] = val[:, hh * HEAD_DIM:(hh + 1) * HEAD_DIM]
        for r in range(d):
            for hh in range(HEADS):
                ref[0, r, :, hh * HEAD_DIM:(hh + 1) * HEAD_DIM] = (
                    slab[hh, pl.ds(r, tm // d, stride=d), :].astype(ref.dtype))

    def head_rms(z, w):
        parts = [_rms(z[:, hh * HEAD_DIM:(hh + 1) * HEAD_DIM]) for hh in range(HEADS)]
        return jnp.concatenate(parts, axis=-1) * w

    def put_state(g, kn, v):
        if state_rows[g] is None:
            st_refs[g][:, 0:GROUP_COLS] = kn
            st_refs[g][:, GROUP_COLS:2 * GROUP_COLS] = v
        else:
            for hh in range(HEADS):
                cols = slice(hh * HEAD_DIM, (hh + 1) * HEAD_DIM)
                kv_stash[g, hh] = kn[:, cols]
                kv_stash[g, HEADS + hh] = v[:, cols]

    def flush_state(g):
        st, rows = st_refs[g], min(state_rows[g], tm)
        first = tiles_per_seq - max(state_rows[g] // tm, 1)

        @pl.when(ti >= first)
        def _():
            for j in range(2 * HEADS):
                st[pl.ds(j, rows, stride=2 * HEADS), :] = kv_stash[g, j, tm - rows:, :]

    for g in range(N_GROUPS):
        c0 = g * GROUP_COLS
        q = head_rms(proj(c0, GROUP_COLS), qw_ref[:, c0:c0 + GROUP_COLS] * QK_SCALE)
        put(q_refs[g], g, q)
        kn = head_rms(proj(QKV_COLS + c0, GROUP_COLS), kw_ref[:, c0:c0 + GROUP_COLS])
        put(k_refs[g], g, kn)
        v = proj(2 * QKV_COLS + c0, GROUP_COLS)
        put(v_refs[g], g, v)
        put_state(g, kn, v)

    c0 = 3 * QKV_COLS
    u_ref[...] = _gelu(proj(c0, SGU_WIDTH)).astype(u_ref.dtype)
    vs = _gelu(proj(c0 + SGU_WIDTH, SGU_WIDTH))
    mu = jnp.mean(vs, axis=-1, keepdims=True)
    vc = vs - mu
    var = jnp.mean(vc * vc, axis=-1, keepdims=True)
    vs_ref[...] = (vc * jax.lax.rsqrt(var + EPS) * lnw_ref[...] + lnb_ref[...]).astype(vs_ref.dtype)
    c0 += 2 * SGU_WIDTH
    ga_ref[...] = _sigmoid(proj(c0, d_model)).astype(ga_ref.dtype)
    gb_ref[...] = _sigmoid(proj(c0 + d_model, d_model)).astype(gb_ref.dtype)

    for g in range(N_GROUPS):
        if state_rows[g] is not None:
            flush_state(g)


def _inproj_call(x, mod, n1w, w_in, qw, kw, lnw, lnb, *, seq, tm, deinterleave, act_dtype,
                 layer=0, depth=1, prev_states=None):
    t, d_model = x.shape
    batch = t // seq
    tiles_per_seq = seq // tm
    d_in = w_in.shape[1]

    def tok(cols, dtype):
        return (jax.ShapeDtypeStruct((t, cols), dtype), pl.BlockSpec((tm, cols), lambda i: (i, 0)))

    def grp(g, dtype):
        d = DILATIONS[g]
        if d == 1 or not deinterleave:
            return tok(GROUP_COLS, dtype)
        assert tm % d == 0 and (tm // d) % 16 == 0
        return (jax.ShapeDtypeStruct((batch, d, seq // d, GROUP_COLS), dtype),
                pl.BlockSpec((1, d, tm // d, GROUP_COLS),
                             lambda i: (i // tiles_per_seq, 0, i % tiles_per_seq, 0)))

    outs = []
    for g in range(N_GROUPS):
        outs += [grp(g, act_dtype)] * 3
    outs += [tok(SGU_WIDTH, act_dtype), tok(SGU_WIDTH, act_dtype if deinterleave else F32),
             tok(d_model, act_dtype), tok(d_model, act_dtype)]

    state_rows = []
    for g in range(N_GROUPS):
        if not deinterleave:
            state_rows.append(None)
            outs.append(tok(2 * GROUP_COLS, F32))
            continue
        rows = min(WINDOWS[g], seq)
        state_rows.append(rows)
        blk_rows = min(rows, tm)
        assert rows % blk_rows == 0
        nblk = rows // blk_rows
        shape = jax.ShapeDtypeStruct((depth * batch * rows * 2 * HEADS, LANES), F32)
        spec = pl.BlockSpec(
            (blk_rows * 2 * HEADS, LANES),
            lambda i, nblk=nblk: ((layer * batch + i // tiles_per_seq) * nblk
                                  + jnp.maximum(i % tiles_per_seq - (tiles_per_seq - nblk), 0), 0))
        outs.append((shape, spec))

    aliased = list(prev_states) if prev_states is not None else []
    n_in = 8
    kern = functools.partial(_inproj_kernel, tm=tm, d_model=d_model, dils=DILATIONS,
                             state_rows=tuple(state_rows), tiles_per_seq=tiles_per_seq,
                             deinterleave=deinterleave, n_aliased=len(aliased))
    return pl.pallas_call(
        kern,
        grid=(t // tm,),
        in_specs=[pl.BlockSpec((tm, d_model), lambda i: (i, 0)),
                  pl.BlockSpec((1,) + mod.shape[1:], lambda i: (i // tiles_per_seq, 0, 0)),
                  _resident((1, d_model)),
                  _resident((d_model, d_in)),
                  _resident((1, QKV_COLS)), _resident((1, QKV_COLS)),
                  _resident((1, SGU_WIDTH)), _resident((1, SGU_WIDTH))]
                 + [pl.BlockSpec(memory_space=pl.ANY)] * len(aliased),
        out_specs=[o[1] for o in outs],
        out_shape=[o[0] for o in outs],
        input_output_aliases={n_in + g: 13 + g for g in range(len(aliased))},
        scratch_shapes=[pltpu.VMEM((tm, d_model), BF16),
                        pltpu.VMEM((HEADS, tm, LANES), F32),
                        pltpu.VMEM((N_GROUPS, 2 * HEADS, tm, LANES), F32)],
        compiler_params=_params(1),
        name="inproj_deint" if deinterleave else "inproj_tok",
    )(x, mod, n1w, w_in, qw, kw, lnw, lnb, *aliased)


def _alibi_slopes():
    h = np.arange(1, N_GROUPS * HEADS + 1, dtype=np.float32)
    s = np.power(np.float32(2.0), -8.0 * h / (N_GROUPS * HEADS)).astype(np.float32)
    return s.reshape(N_GROUPS, HEADS)


def _band_bias():
    qi = np.arange(N_STEPS)[:, None]
    ki = np.arange(2 * N_STEPS)[None, :]
    dist = N_STEPS + qi - ki
    valid = (dist >= 0) & (dist <= N_STEPS)
    slopes = _alibi_slopes()
    out = np.empty((N_GROUPS, HEADS, 2, N_STEPS, 2 * N_STEPS), np.float32)
    for g in range(N_GROUPS):
        for hh in range(HEADS):
            b = np.where(valid, -slopes[g, hh] * (dist * DILATIONS[g]).astype(np.float32),
                         np.float32(NEG_INF)).astype(np.float32)
            out[g, hh, 1] = b
            first = b.copy()
            first[:, :N_STEPS] = NEG_INF
            out[g, hh, 0] = first
    return out


def _attn_block(q, k2, v2, bias):
    s = _dot_nt(q, k2) + bias
    m = jnp.max(s, axis=-1, keepdims=True)
    p = jnp.exp(s - m)
    l = jnp.sum(p, axis=-1, keepdims=True)
    acc = _dot(p.astype(v2.dtype), v2)
    return acc * (1.0 / l), m + jnp.log(l)


ATTN_UNROLL = 8


def _attn_kernel(q0, k0, v0, q1, k1, v1, q2, k2, v2, bias_ref, o_ref, o_scr, lse_scr, *, seq):
    blk = N_STEPS

    def run_group(g, qr, kr, vr):
        d = DILATIONS[g]
        nblk = seq // d // blk

        def one_block(r, i):
            own = pl.multiple_of(i * blk, blk)
            prev = pl.multiple_of(jnp.maximum(i - 1, 0) * blk, blk)
            if d == 1:
                ld = lambda ref, start: ref[pl.ds(start, blk), :]
            else:
                ld = lambda ref, start: ref[0, r, pl.ds(start, blk), :]
            k2_ = jnp.concatenate([ld(kr, prev), ld(kr, own)], axis=0)
            v2_ = jnp.concatenate([ld(vr, prev), ld(vr, own)], axis=0)
            o, lse = _attn_block(ld(qr, own), k2_, v2_, bias_ref[g, 0, jnp.minimum(i, 1)])
            lse_b = jnp.broadcast_to(lse, (blk, LANES))
            if d == 1:
                o_scr[g, pl.ds(own, blk), :] = o
                lse_scr[g, pl.ds(own, blk), :] = lse_b
            else:
                start = i * (blk * d) + r
                o_scr[g, pl.ds(start, blk, stride=d), :] = o
                lse_scr[g, pl.ds(start, blk, stride=d), :] = lse_b

        ur = min(ATTN_UNROLL, d)
        ui = ATTN_UNROLL // ur
        assert d % ur == 0 and nblk % ui == 0

        def body(it, carry):
            i0 = (it // (d // ur)) * ui
            r0 = (it % (d // ur)) * ur
            for a in range(ui):
                for b in range(ur):
                    one_block(r0 + b, i0 + a)
            return carry

        jax.lax.fori_loop(0, (d // ur) * (nblk // ui), body, 0)

    run_group(0, q0, k0, v0)
    run_group(1, q1, k1, v1)
    run_group(2, q2, k2, v2)

    def merge(i, carry):
        rows = pl.ds(pl.multiple_of(i * blk, blk), blk)
        lses = [lse_scr[g, rows, :] for g in range(N_GROUPS)]
        mx = jnp.maximum(jnp.maximum(lses[0], lses[1]), lses[2])
        ws = [jnp.exp(x - mx) for x in lses]
        num = ws[0] * o_scr[0, rows, :] + ws[1] * o_scr[1, rows, :] + ws[2] * o_scr[2, rows, :]
        o_ref[rows, :] = (num / (ws[0] + ws[1] + ws[2])).astype(o_ref.dtype)
        return carry

    jax.lax.fori_loop(0, seq // blk, merge, 0)


def _attn_call(qkv, bias, *, batch, seq):
    in_specs = []
    for g in range(N_GROUPS):
        d = DILATIONS[g]
        if d == 1:
            spec = pl.BlockSpec((seq, HEAD_DIM), lambda b, h: (b, h))
        else:
            spec = pl.BlockSpec((1, d, seq // d, HEAD_DIM), lambda b, h: (b, 0, 0, h))
        in_specs += [spec] * 3
    in_specs.append(pl.BlockSpec((N_GROUPS, 1, 2, N_STEPS, 2 * N_STEPS), lambda b, h: (0, h, 0, 0, 0)))
    return pl.pallas_call(
        functools.partial(_attn_kernel, seq=seq),
        grid=(batch, HEADS),
        in_specs=in_specs,
        out_specs=pl.BlockSpec((seq, HEAD_DIM), lambda b, h: (b, h)),
        out_shape=jax.ShapeDtypeStruct((batch * seq, GROUP_COLS), BF16),
        scratch_shapes=[pltpu.VMEM((N_GROUPS, seq, LANES), F32),
                        pltpu.VMEM((N_GROUPS, seq, LANES), F32)],
        compiler_params=_params(2),
        name="attn_prompt",
    )(*qkv, bias)


def _attn_sample_kernel(q_ref, kv_ref, c0, c1, c2, slope_ref, o_ref, *, bb):
    c_refs = (c0, c1, c2)
    back = (N_STEPS - jax.lax.broadcasted_iota(jnp.int32, (N_STEPS, 1, 1), 0)).astype(F32)

    def to_v_rows(stat):
        return pltpu.roll(jnp.broadcast_to(stat, (2 * HEADS, HEAD_DIM)), HEADS, axis=0)

    for b in range(bb):
        outs, lses = [], []
        for g in range(N_GROUPS):
            q8, kv8, x = q_ref[b, g], kv_ref[b, g], c_refs[g][b]
            bias = (back * (-float(DILATIONS[g]))) * slope_ref[g][:, 0:1]
            s = jnp.sum(x * q8, axis=-1, keepdims=True) + bias
            s_self = jnp.sum(kv8 * q8, axis=-1, keepdims=True)
            m = jnp.maximum(jnp.max(s, axis=0), s_self)
            p = jnp.exp(s - m)
            p_self = jnp.exp(s_self - m)
            l = jnp.sum(p, axis=0) + p_self
            p_v = pltpu.roll(jnp.broadcast_to(p, x.shape), HEADS, axis=1)
            acc = jnp.sum(p_v * x, axis=0) + to_v_rows(p_self) * kv8
            outs.append(acc / to_v_rows(l))
            lses.append(to_v_rows(m + jnp.log(l)))
        mx = jnp.maximum(jnp.maximum(lses[0], lses[1]), lses[2])
        ws = [jnp.exp(v - mx) for v in lses]
        num = ws[0] * outs[0] + ws[1] * outs[1] + ws[2] * outs[2]
        o_ref[b] = num / (ws[0] + ws[1] + ws[2])


def _attn_sample_call(q8, kv8, caches, layer, slopes8):
    bd = q8.shape[0]
    bb = 8
    tile = (2 * HEADS, HEAD_DIM)
    in_specs = [pl.BlockSpec((bb, N_GROUPS) + tile, lambda i: (i, 0, 0, 0))] * 2
    in_specs += [pl.BlockSpec((None, bb, N_STEPS) + tile, lambda i: (layer, i, 0, 0, 0))] * 3
    in_specs.append(pl.BlockSpec((N_GROUPS,) + tile, lambda i: (0, 0, 0)))
    return pl.pallas_call(
        functools.partial(_attn_sample_kernel, bb=bb),
        grid=(bd // bb,),
        in_specs=in_specs,
        out_specs=pl.BlockSpec((bb,) + tile, lambda i: (i, 0, 0)),
        out_shape=jax.ShapeDtypeStruct((bd,) + tile, F32),
        compiler_params=_params(1),
        name="attn_sample",
    )(q8, kv8, *caches, slopes8)


def _post_kernel(x_ref, mod_ref, ya_ref, u_ref, vs_ref, ga_ref, gb_ref, ws_ref, bs_ref,
                 wpa_ref, wps_ref, wo_ref, n2w_ref, wfi_ref, wfo_ref, o_ref, mix_scr,
                 *, tm, d_model, d_ff, chunked):
    mod = mod_ref[0]
    g1 = mod[:, 2 * d_model:3 * d_model]
    sh2 = mod[:, 3 * d_model:4 * d_model]
    sc2 = mod[:, 4 * d_model:5 * d_model]
    g2 = mod[:, 5 * d_model:6 * d_model]

    if chunked:
        row = jax.lax.broadcasted_iota(jnp.int32, (CHUNK, CHUNK), 0)
        col = jax.lax.broadcasted_iota(jnp.int32, (CHUNK, CHUNK), 1)
        tril = col <= row
        lane = jax.lax.broadcasted_iota(jnp.int32, (CHUNK, LANES), 1)
        low = lane < SGU_GROUP_DIM
        wms = [jnp.where(tril, ws_ref[gi], 0.0).astype(BF16) for gi in range(SGU_GROUPS)]
        for c in range(tm // CHUNK):
            rows = slice(c * CHUNK, (c + 1) * CHUNK)
            for j in range(SGU_WIDTH // LANES):
                cols = slice(j * LANES, (j + 1) * LANES)
                v = vs_ref[rows, cols]
                mix_scr[rows, cols] = jnp.where(low, _dot(wms[2 * j], v), _dot(wms[2 * j + 1], v))
        mixed = mix_scr[...] + jnp.concatenate([bs_ref[...]] * (tm // CHUNK), axis=0)
    else:
        mixed = vs_ref[...] * ws_ref[...] + bs_ref[...]

    ys = (u_ref[...].astype(F32) * mixed).astype(BF16)
    xa = _dot(ya_ref[...], wpa_ref[...])
    xs = _dot(ys, wps_ref[...])
    merged = (ga_ref[...].astype(F32) * xa + gb_ref[...].astype(F32) * xs).astype(BF16)
    x1 = x_ref[...] + g1 * _dot(merged, wo_ref[...])
    h2 = (_rms(x1) * n2w_ref[...] * (1.0 + sc2) + sh2).astype(BF16)
    a = _dot(h2, wfi_ref[:, 0:d_ff])
    b = _dot(h2, wfi_ref[:, d_ff:2 * d_ff])
    act = (a * _sigmoid(a) * b).astype(BF16)
    o_ref[...] = x1 + g2 * _dot(act, wfo_ref[...])


def _post_call(x, mod, ya, u, vs, ga, gb, ws, bs, wpa, wps, wo, n2w, wfi, wfo, *, seq, tm, chunked):
    t, d_model = x.shape
    d_ff = wfo.shape[0]
    tiles_per_seq = seq // tm

    def tok(cols):
        return pl.BlockSpec((tm, cols), lambda i: (i, 0))

    kern = functools.partial(_post_kernel, tm=tm, d_model=d_model, d_ff=d_ff, chunked=chunked)
    return pl.pallas_call(
        kern,
        grid=(t // tm,),
        in_specs=[tok(d_model),
                  pl.BlockSpec((1,) + mod.shape[1:], lambda i: (i // tiles_per_seq, 0, 0)),
                  tok(GROUP_COLS), tok(SGU_WIDTH), tok(SGU_WIDTH), tok(d_model), tok(d_model),
                  _resident(ws.shape), _resident(bs.shape),
                  _resident(wpa.shape), _resident(wps.shape), _resident(wo.shape),
                  _resident(n2w.shape), _resident(wfi.shape), _resident(wfo.shape)],
        out_specs=tok(d_model),
        out_shape=jax.ShapeDtypeStruct((t, d_model), F32),
        scratch_shapes=[pltpu.VMEM((tm, SGU_WIDTH), F32)],
        compiler_params=_params(1),
        name="post_chunked" if chunked else "post_tok",
    )(x, mod, ya, u, vs, ga, gb, ws, bs, wpa, wps, wo, n2w, wfi, wfo)


def kernel(x_prompt, x_sample, cache_kv_w128, cache_kv_w512, cache_kv_w2048, c_prompt, c_sample,
           w_ada, b_ada, norm1_w, w_in, q_norm_w, k_norm_w, sgu_ln_w, sgu_ln_b, w_spatial, b_spatial,
           w_proj_att, w_proj_sgu, w_out, norm2_w, w_ffn_in, w_ffn_out):
    batch, seq, d_model = x_prompt.shape
    bd, dec_seq, _ = x_sample.shape
    depth = w_in.shape[0]
    assert dec_seq == 1 and seq % (N_STEPS * DILATIONS[-1]) == 0
    caches_in = (cache_kv_w128, cache_kv_w512, cache_kv_w2048)
    for g in range(N_GROUPS):
        assert caches_in[g].shape[2] == WINDOWS[g]
    tm = 256

    n_c = batch + bd
    c_all = jnp.concatenate([c_prompt, c_sample], axis=0)
    c_all = jnp.pad(c_all, ((0, -n_c % 8), (0, 0)))
    mod = _ada_call(c_all, w_ada, b_ada)

    band_bias = jnp.asarray(_band_bias())
    tile = (2 * HEADS, HEAD_DIM)
    slopes8 = np.zeros((N_GROUPS,) + tile, np.float32)
    slopes8[:, :HEADS, :] = _alibi_slopes()[:, :, None]
    slopes8 = jnp.asarray(slopes8)
    caches = [caches_in[g].reshape((depth, bd, N_STEPS, DILATIONS[g] * tile[0], tile[1]))
              for g in range(N_GROUPS)]

    xp = x_prompt.reshape(batch * seq, d_model)
    xs = x_sample.reshape(bd, d_model)
    p_states = None
    s_states = [[] for _ in range(N_GROUPS)]
    s_v = []
    for l in range(depth):
        w_in_l = w_in[l].astype(BF16)
        qw = jnp.tile(q_norm_w[l][:, None, :], (1, HEADS, 1)).reshape(1, QKV_COLS)
        kw = jnp.tile(k_norm_w[l][:, None, :], (1, HEADS, 1)).reshape(1, QKV_COLS)
        lnw, lnb = sgu_ln_w[l].reshape(1, -1), sgu_ln_b[l].reshape(1, -1)
        n1w, n2w = norm1_w[l].reshape(1, -1), norm2_w[l].reshape(1, -1)
        wpa, wps, wo = w_proj_att[l].astype(BF16), w_proj_sgu[l].astype(BF16), w_out[l].astype(BF16)
        wfi, wfo = w_ffn_in[l].astype(BF16), w_ffn_out[l].astype(BF16)
        mod_p = mod[l, :batch].reshape(batch, 1, -1)
        mod_s = mod[l, batch:n_c].reshape(1, bd, -1)
        bs_p = jnp.repeat(b_spatial[l].T, SGU_GROUP_DIM, axis=1)
        ws_s = jnp.repeat(w_spatial[l][:, 0, 0], SGU_GROUP_DIM).reshape(1, SGU_WIDTH)
        bs_s = jnp.repeat(b_spatial[l][:, 0], SGU_GROUP_DIM).reshape(1, SGU_WIDTH)

        outs = _inproj_call(xp, mod_p, n1w, w_in_l, qw, kw, lnw, lnb,
                            seq=seq, tm=tm, deinterleave=True, act_dtype=BF16,
                            layer=l, depth=depth, prev_states=p_states)
        qkv, (u, vs, ga, gb), p_states = outs[:9], outs[9:13], outs[13:]
        ya = _attn_call(qkv, band_bias, batch=batch, seq=seq)
        xp = _post_call(xp, mod_p, ya, u, vs, ga, gb, w_spatial[l], bs_p, wpa, wps, wo, n2w, wfi, wfo,
                        seq=seq, tm=tm, chunked=True)

        outs = _inproj_call(xs, mod_s, n1w, w_in_l, qw, kw, lnw, lnb,
                            seq=bd, tm=bd, deinterleave=False, act_dtype=F32)
        qkv, (u, vs, ga, gb), sts = outs[:9], outs[9:13], outs[13:]
        q8 = jnp.stack([qkv[0], qkv[3], qkv[6]], axis=1).reshape(bd, N_GROUPS, HEADS, HEAD_DIM)
        q8 = jnp.pad(q8, ((0, 0), (0, 0), (0, HEADS), (0, 0)))
        kv8 = jnp.stack(sts, axis=1).reshape((bd, N_GROUPS) + tile)
        ya = _attn_sample_call(q8, kv8, caches, l, slopes8)[:, HEADS:, :].reshape(bd, GROUP_COLS)
        xs = _post_call(xs, mod_s, ya.astype(BF16), u, vs, ga, gb, ws_s, bs_s, wpa, wps, wo, n2w, wfi, wfo,
                        seq=bd, tm=bd, chunked=False)
        for g in range(N_GROUPS):
            s_states[g].append(sts[g].reshape(bd, 1, 2, HEADS, HEAD_DIM))
        s_v.append(vs.reshape(bd, 1, SGU_WIDTH))

    p_states = [p_states[g].reshape(depth, batch, -1, 2, HEADS, HEAD_DIM) for g in range(N_GROUPS)]
    return (xp.reshape(batch, seq, d_model), xs.reshape(bd, 1, d_model),
            p_states[0], p_states[1], p_states[2],
            jnp.stack(s_states[0]), jnp.stack(s_states[1]), jnp.stack(s_states[2]),
            jnp.stack(s_v))
```

```python
import functools
import math

import jax
import jax.numpy as jnp
import numpy as np
from jax.experimental import pallas as pl
from jax.experimental.pallas import tpu as pltpu

HEAD_DIM = 128
N_GROUPS = 3
HEADS = 4
GROUP_COLS = HEADS * HEAD_DIM
QKV_COLS = N_GROUPS * GROUP_COLS
WINDOWS = (128, 512, 2048)
DILATIONS = (1, 4, 16)
N_STEPS = 128
CHUNK = 128
SGU_WIDTH = 512
SGU_GROUPS = 8
SGU_GROUP_DIM = SGU_WIDTH // SGU_GROUPS
N_ADA = 6
EPS = 1e-6
NEG_INF = -1e30
QK_SCALE = HEAD_DIM ** -0.5

LANES = 128
VMEM_LIMIT_BYTES = 56 * 1024 * 1024

BF16 = jnp.bfloat16
F32 = jnp.float32


def _params(n_axes):
    return pltpu.CompilerParams(dimension_semantics=("arbitrary",) * n_axes,
                                vmem_limit_bytes=VMEM_LIMIT_BYTES)


def _resident(shape):
    nd = len(shape)
    return pl.BlockSpec(shape, lambda *_: (0,) * nd, pipeline_mode=pl.Buffered(1))


def _resident_layer(shape, layer):
    nd = len(shape)
    return pl.BlockSpec((None,) + tuple(shape[1:]), lambda *_: (layer,) + (0,) * (nd - 1),
                        pipeline_mode=pl.Buffered(1))


def _dot(a, b):
    return jnp.dot(a, b, preferred_element_type=F32)


def _dot_nt(a, b):
    return jax.lax.dot_general(a, b, (((1,), (1,)), ((), ())), preferred_element_type=F32)


def _rms(x):
    return x * jax.lax.rsqrt(jnp.mean(x * x, axis=-1, keepdims=True) + EPS)


def _gelu(x):
    return 0.5 * x * (1.0 + jax.lax.erf(x * (2.0 ** -0.5)))


def _sigmoid(x):
    return 1.0 / (1.0 + jnp.exp(-x))


def _ada_kernel(c_ref, w_ref, b_ref, o_ref):
    c = c_ref[...]
    a = c * _sigmoid(c)
    o_ref[...] = jnp.dot(a, w_ref[...], preferred_element_type=F32,
                         precision=jax.lax.Precision.HIGHEST) + b_ref[...]


def _ada_call(c_all, w_ada, b_ada):
    depth, d, n = w_ada.shape
    rows = c_all.shape[0]
    tn = 1024
    return pl.pallas_call(
        _ada_kernel,
        grid=(depth, n // tn),
        in_specs=[pl.BlockSpec((rows, d), lambda l, j: (0, 0)),
                  pl.BlockSpec((None, d, tn), lambda l, j: (l, 0, j)),
                  pl.BlockSpec((None, 1, tn), lambda l, j: (l, 0, j))],
        out_specs=pl.BlockSpec((None, rows, tn), lambda l, j: (l, 0, j)),
        out_shape=jax.ShapeDtypeStruct((depth, rows, n), F32),
        compiler_params=_params(2),
        name="adaln_mod",
    )(c_all, w_ada, b_ada.reshape(depth, 1, n))


def _inproj_kernel(x_ref, mod_ref, n1w_ref, w_ref, qw_ref, kw_ref, lnw_ref, lnb_ref,
                   *refs, tm, d_model, dils, state_rows, tiles_per_seq, deinterleave, n_aliased):
    (q0, k0, v0, q1, k1, v1, q2, k2, v2, u_ref, vs_ref, ga_ref, gb_ref,
     st0, st1, st2, h_scr, slab, kv_stash) = refs[n_aliased:]
    q_refs, k_refs, v_refs, st_refs = (q0, q1, q2), (k0, k1, k2), (v0, v1, v2), (st0, st1, st2)
    ti = pl.program_id(0) % tiles_per_seq

    mod = mod_ref[0]
    sh1 = mod[:, 0:d_model]
    sc1 = mod[:, d_model:2 * d_model]
    h = _rms(x_ref[...]) * n1w_ref[...] * (1.0 + sc1) + sh1
    h_scr[...] = h.astype(BF16)

    def proj(c0, width):
        return _dot(h_scr[...], w_ref[:, c0:c0 + width])

    def put(ref, g, val):
        d = dils[g]
        if d == 1 or not deinterleave:
            ref[...] = val.astype(ref.dtype)
            return
        for hh in range(HEADS):
            slab[hh] = val[:, hh * HEAD_DIM:(hh + 1) * HEAD_DIM]
        for r in range(d):
            for hh in range(HEADS):
                ref[0, r, :, hh * HEAD_DIM:(hh + 1) * HEAD_DIM] = (
                    slab[hh, pl.ds(r, tm // d, stride=d), :].astype(ref.dtype))

    def head_rms(z, w):
        parts = [_rms(z[:, hh * HEAD_DIM:(hh + 1) * HEAD_DIM]) for hh in range(HEADS)]
        return jnp.concatenate(parts, axis=-1) * w

    def put_state(g, kn, v):
        if state_rows[g] is None:
            st_refs[g][:, 0:GROUP_COLS] = kn
            st_refs[g][:, GROUP_COLS:2 * GROUP_COLS] = v
        else:
            for hh in range(HEADS):
                cols = slice(hh * HEAD_DIM, (hh + 1) * HEAD_DIM)
                kv_stash[g, hh] = kn[:, cols]
                kv_stash[g, HEADS + hh] = v[:, cols]

    def flush_state(g):
        st, rows = st_refs[g], min(state_rows[g], tm)
        first = tiles_per_seq - max(state_rows[g] // tm, 1)

        @pl.when(ti >= first)
        def _():
            for j in range(2 * HEADS):
                st[pl.ds(j, rows, stride=2 * HEADS), :] = kv_stash[g, j, tm - rows:, :]

    for g in range(N_GROUPS):
        c0 = g * GROUP_COLS
        q = head_rms(proj(c0, GROUP_COLS), qw_ref[:, c0:c0 + GROUP_COLS] * QK_SCALE)
        put(q_refs[g], g, q)
        kn = head_rms(proj(QKV_COLS + c0, GROUP_COLS), kw_ref[:, c0:c0 + GROUP_COLS])
        put(k_refs[g], g, kn)
        v = proj(2 * QKV_COLS + c0, GROUP_COLS)
        put(v_refs[g], g, v)
        put_state(g, kn, v)

    c0 = 3 * QKV_COLS
    u_ref[...] = _gelu(proj(c0, SGU_WIDTH)).astype(u_ref.dtype)
    vs = _gelu(proj(c0 + SGU_WIDTH, SGU_WIDTH))
    mu = jnp.mean(vs, axis=-1, keepdims=True)
    vc = vs - mu
    var = jnp.mean(vc * vc, axis=-1, keepdims=True)
    vs_ref[...] = (vc * jax.lax.rsqrt(var + EPS) * lnw_ref[...] + lnb_ref[...]).astype(vs_ref.dtype)
    c0 += 2 * SGU_WIDTH
    ga_ref[...] = _sigmoid(proj(c0, d_model)).astype(ga_ref.dtype)
    gb_ref[...] = _sigmoid(proj(c0 + d_model, d_model)).astype(gb_ref.dtype)

    for g in range(N_GROUPS):
        if state_rows[g] is not None:
            flush_state(g)


def _inproj_call(x, mod, n1w, w_in, qw, kw, lnw, lnb, *, seq, tm, deinterleave, act_dtype,
                 layer=0, depth=1, prev_states=None):
    t, d_model = x.shape
    batch = t // seq
    tiles_per_seq = seq // tm

    def tok(cols, dtype):
        return (jax.ShapeDtypeStruct((t, cols), dtype), pl.BlockSpec((tm, cols), lambda i: (i, 0)))

    def grp(g, dtype):
        d = DILATIONS[g]
        if d == 1 or not deinterleave:
            return tok(GROUP_COLS, dtype)
        assert tm % d == 0 and (tm // d) % 16 == 0
        return (jax.ShapeDtypeStruct((batch, d, seq // d, GROUP_COLS), dtype),
                pl.BlockSpec((1, d, tm // d, GROUP_COLS),
                             lambda i: (i // tiles_per_seq, 0, i % tiles_per_seq, 0)))

    outs = []
    for g in range(N_GROUPS):
        outs += [grp(g, act_dtype)] * 3
    outs += [tok(SGU_WIDTH, act_dtype), tok(SGU_WIDTH, act_dtype if deinterleave else F32),
             tok(d_model, act_dtype), tok(d_model, act_dtype)]

    state_rows = []
    for g in range(N_GROUPS):
        if not deinterleave:
            state_rows.append(None)
            outs.append(tok(2 * GROUP_COLS, F32))
            continue
        rows = min(WINDOWS[g], seq)
        state_rows.append(rows)
        blk_rows = min(rows, tm)
        assert rows % blk_rows == 0
        nblk = rows // blk_rows
        shape = jax.ShapeDtypeStruct((depth * batch * rows * 2 * HEADS, LANES), F32)
        spec = pl.BlockSpec(
            (blk_rows * 2 * HEADS, LANES),
            lambda i, nblk=nblk: ((layer * batch + i // tiles_per_seq) * nblk
                                  + jnp.maximum(i % tiles_per_seq - (tiles_per_seq - nblk), 0), 0))
        outs.append((shape, spec))

    aliased = list(prev_states) if prev_states is not None else []
    n_in = 8
    kern = functools.partial(_inproj_kernel, tm=tm, d_model=d_model, dils=DILATIONS,
                             state_rows=tuple(state_rows), tiles_per_seq=tiles_per_seq,
                             deinterleave=deinterleave, n_aliased=len(aliased))
    return pl.pallas_call(
        kern,
        grid=(t // tm,),
        in_specs=[pl.BlockSpec((tm, d_model), lambda i: (i, 0)),
                  pl.BlockSpec((1,) + mod.shape[1:], lambda i: (i // tiles_per_seq, 0, 0)),
                  _resident((1, d_model)),
                  _resident_layer(w_in.shape, layer),
                  _resident((1, QKV_COLS)), _resident((1, QKV_COLS)),
                  _resident((1, SGU_WIDTH)), _resident((1, SGU_WIDTH))]
                 + [pl.BlockSpec(memory_space=pl.ANY)] * len(aliased),
        out_specs=[o[1] for o in outs],
        out_shape=[o[0] for o in outs],
        input_output_aliases={n_in + g: 13 + g for g in range(len(aliased))},
        scratch_shapes=[pltpu.VMEM((tm, d_model), BF16),
                        pltpu.VMEM((HEADS, tm, LANES), F32),
                        pltpu.VMEM((N_GROUPS, 2 * HEADS, tm, LANES), F32)],
        compiler_params=_params(1),
        name="inproj_deint" if deinterleave else "inproj_tok",
    )(x, mod, n1w, w_in, qw, kw, lnw, lnb, *aliased)


def _alibi_slopes():
    h = np.arange(1, N_GROUPS * HEADS + 1, dtype=np.float32)
    s = np.power(np.float32(2.0), -8.0 * h / (N_GROUPS * HEADS)).astype(np.float32)
    return s.reshape(N_GROUPS, HEADS)


def _band_bias():
    qi = np.arange(N_STEPS)[:, None]
    ki = np.arange(2 * N_STEPS)[None, :]
    dist = N_STEPS + qi - ki
    valid = (dist >= 0) & (dist <= N_STEPS)
    slopes = _alibi_slopes()
    out = np.empty((N_GROUPS, HEADS, 2, N_STEPS, 2 * N_STEPS), np.float32)
    for g in range(N_GROUPS):
        for hh in range(HEADS):
            b = np.where(valid, -slopes[g, hh] * (dist * DILATIONS[g]).astype(np.float32),
                         np.float32(NEG_INF)).astype(np.float32)
            out[g, hh, 1] = b
            first = b.copy()
            first[:, :N_STEPS] = NEG_INF
            out[g, hh, 0] = first
    return out


def _attn_block(q, k2, v2, bias):
    s = _dot_nt(q, k2) + bias
    m = jnp.max(s, axis=-1, keepdims=True)
    p = jnp.exp(s - m)
    l = jnp.sum(p, axis=-1, keepdims=True)
    return _dot(p.astype(v2.dtype), v2), m, l


ATTN_UNROLL = 32


def _attn_kernel(q0, k0, v0, q1, k1, v1, q2, k2, v2, bias_ref, o_ref, o_scr, m_scr, l_scr, *, seq):
    blk = N_STEPS

    def run_group(g, qr, kr, vr):
        d = DILATIONS[g]
        nblk = seq // d // blk

        def one_block(r, i):
            own = pl.multiple_of(i * blk, blk)
            prev = pl.multiple_of(jnp.maximum(i - 1, 0) * blk, blk)
            if d == 1:
                ld = lambda ref, start: ref[pl.ds(start, blk), :]
            else:
                ld = lambda ref, start: ref[0, r, pl.ds(start, blk), :]
            k2_ = jnp.concatenate([ld(kr, prev), ld(kr, own)], axis=0)
            v2_ = jnp.concatenate([ld(vr, prev), ld(vr, own)], axis=0)
            o, m, l = _attn_block(ld(qr, own), k2_, v2_, bias_ref[g, 0, jnp.minimum(i, 1)])
            m = jnp.broadcast_to(m, (blk, LANES))
            l = jnp.broadcast_to(l, (blk, LANES))
            if d > 1:
                rows = pl.ds(i * (blk * d) + r, blk, stride=d)
                o_scr[g - 1, rows, :] = o
                m_scr[g - 1, rows, :] = m
                l_scr[g - 1, rows, :] = l
                return
            rows = pl.ds(own, blk)
            ms = [m, m_scr[0, rows, :], m_scr[1, rows, :]]
            mx = jnp.maximum(jnp.maximum(ms[0], ms[1]), ms[2])
            ws = [jnp.exp(x - mx) for x in ms]
            num = ws[0] * o + ws[1] * o_scr[0, rows, :] + ws[2] * o_scr[1, rows, :]
            den = ws[0] * l + ws[1] * l_scr[0, rows, :] + ws[2] * l_scr[1, rows, :]
            o_ref[rows, :] = (num / den).astype(o_ref.dtype)

        ur = min(ATTN_UNROLL, d)
        ui = ATTN_UNROLL // ur
        assert d % ur == 0 and nblk % ui == 0

        def body(it, carry):
            i0 = (it // (d // ur)) * ui
            r0 = (it % (d // ur)) * ur
            for a in range(ui):
                for b in range(ur):
                    one_block(r0 + b, i0 + a)
            return carry

        jax.lax.fori_loop(0, (d // ur) * (nblk // ui), body, 0)

    run_group(1, q1, k1, v1)
    run_group(2, q2, k2, v2)
    run_group(0, q0, k0, v0)


def _attn_call(qkv, bias, *, batch, seq):
    in_specs = []
    for g in range(N_GROUPS):
        d = DILATIONS[g]
        if d == 1:
            spec = pl.BlockSpec((seq, HEAD_DIM), lambda b, h: (b, h))
        else:
            spec = pl.BlockSpec((1, d, seq // d, HEAD_DIM), lambda b, h: (b, 0, 0, h))
        in_specs += [spec] * 3
    in_specs.append(pl.BlockSpec((N_GROUPS, 1, 2, N_STEPS, 2 * N_STEPS), lambda b, h: (0, h, 0, 0, 0)))
    return pl.pallas_call(
        functools.partial(_attn_kernel, seq=seq),
        grid=(batch, HEADS),
        in_specs=in_specs,
        out_specs=pl.BlockSpec((seq, HEAD_DIM), lambda b, h: (b, h)),
        out_shape=jax.ShapeDtypeStruct((batch * seq, GROUP_COLS), BF16),
        scratch_shapes=[pltpu.VMEM((N_GROUPS - 1, seq, LANES), F32)] * 3,
        compiler_params=_params(2),
        name="attn_prompt",
    )(*qkv, bias)


def _attn_sample_kernel(q_ref, kv_ref, c0, c1, c2, slope_ref, o_ref, *, bb):
    c_refs = (c0, c1, c2)
    back = (N_STEPS - jax.lax.broadcasted_iota(jnp.int32, (N_STEPS, 1, 1), 0)).astype(F32)

    def to_v_rows(stat):
        return pltpu.roll(jnp.broadcast_to(stat, (2 * HEADS, HEAD_DIM)), HEADS, axis=0)

    for b in range(bb):
        outs, lses = [], []
        for g in range(N_GROUPS):
            q8, kv8, x = q_ref[b, g], kv_ref[b, g], c_refs[g][b]
            bias = (back * (-float(DILATIONS[g]))) * slope_ref[g][:, 0:1]
            s = jnp.sum(x * q8, axis=-1, keepdims=True) + bias
            s_self = jnp.sum(kv8 * q8, axis=-1, keepdims=True)
            m = jnp.maximum(jnp.max(s, axis=0), s_self)
            p = jnp.exp(s - m)
            p_self = jnp.exp(s_self - m)
            l = jnp.sum(p, axis=0) + p_self
            p_v = pltpu.roll(jnp.broadcast_to(p, x.shape), HEADS, axis=1)
            acc = jnp.sum(p_v * x, axis=0) + to_v_rows(p_self) * kv8
            outs.append(acc / to_v_rows(l))
            lses.append(to_v_rows(m + jnp.log(l)))
        mx = jnp.maximum(jnp.maximum(lses[0], lses[1]), lses[2])
        ws = [jnp.exp(v - mx) for v in lses]
        num = ws[0] * outs[0] + ws[1] * outs[1] + ws[2] * outs[2]
        o_ref[b] = num / (ws[0] + ws[1] + ws[2])


def _attn_sample_call(q8, kv8, caches, layer, slopes8):
    bd = q8.shape[0]
    bb = 8
    tile = (2 * HEADS, HEAD_DIM)
    in_specs = [pl.BlockSpec((bb, N_GROUPS) + tile, lambda i: (i, 0, 0, 0))] * 2
    in_specs += [pl.BlockSpec((None, bb, N_STEPS) + tile, lambda i: (layer, i, 0, 0, 0))] * 3
    in_specs.append(pl.BlockSpec((N_GROUPS,) + tile, lambda i: (0, 0, 0)))
    return pl.pallas_call(
        functools.partial(_attn_sample_kernel, bb=bb),
        grid=(bd // bb,),
        in_specs=in_specs,
        out_specs=pl.BlockSpec((bb,) + tile, lambda i: (i, 0, 0)),
        out_shape=jax.ShapeDtypeStruct((bd,) + tile, F32),
        compiler_params=_params(1),
        name="attn_sample",
    )(q8, kv8, *caches, slopes8)


def _post_kernel(x_ref, mod_ref, ya_ref, u_ref, vs_ref, ga_ref, gb_ref, ws_ref, bs_ref,
                 wpa_ref, wps_ref, wo_ref, n2w_ref, wfi_ref, wfo_ref, o_ref, mix_scr,
                 *, tm, d_model, d_ff, chunked):
    mod = mod_ref[0]
    g1 = mod[:, 2 * d_model:3 * d_model]
    sh2 = mod[:, 3 * d_model:4 * d_model]
    sc2 = mod[:, 4 * d_model:5 * d_model]
    g2 = mod[:, 5 * d_model:6 * d_model]

    if chunked:
        row = jax.lax.broadcasted_iota(jnp.int32, (CHUNK, CHUNK), 0)
        col = jax.lax.broadcasted_iota(jnp.int32, (CHUNK, CHUNK), 1)
        tril = col <= row
        lane = jax.lax.broadcasted_iota(jnp.int32, (CHUNK, LANES), 1)
        low = lane < SGU_GROUP_DIM
        wms = [jnp.where(tril, ws_ref[gi], 0.0).astype(BF16) for gi in range(SGU_GROUPS)]
        for c in range(tm // CHUNK):
            rows = slice(c * CHUNK, (c + 1) * CHUNK)
            for j in range(SGU_WIDTH // LANES):
                cols = slice(j * LANES, (j + 1) * LANES)
                v = vs_ref[rows, cols]
                mix_scr[rows, cols] = jnp.where(low, _dot(wms[2 * j], v), _dot(wms[2 * j + 1], v))
        mixed = mix_scr[...] + jnp.concatenate([bs_ref[...]] * (tm // CHUNK), axis=0)
    else:
        mixed = vs_ref[...] * ws_ref[...] + bs_ref[...]

    ys = (u_ref[...].astype(F32) * mixed).astype(BF16)
    xa = _dot(ya_ref[...], wpa_ref[...])
    xs = _dot(ys, wps_ref[...])
    merged = (ga_ref[...].astype(F32) * xa + gb_ref[...].astype(F32) * xs).astype(BF16)
    x1 = x_ref[...] + g1 * _dot(merged, wo_ref[...])
    h2 = (_rms(x1) * n2w_ref[...] * (1.0 + sc2) + sh2).astype(BF16)
    a = _dot(h2, wfi_ref[:, 0:d_ff])
    b = _dot(h2, wfi_ref[:, d_ff:2 * d_ff])
    act = (a * _sigmoid(a) * b).astype(BF16)
    o_ref[...] = x1 + g2 * _dot(act, wfo_ref[...])


def _post_call(x, mod, ya, u, vs, ga, gb, ws, bs, wpa, wps, wo, n2w, wfi, wfo, *,
               seq, tm, chunked, layer):
    t, d_model = x.shape
    d_ff = wfo.shape[1]
    tiles_per_seq = seq // tm

    def tok(cols):
        return pl.BlockSpec((tm, cols), lambda i: (i, 0))

    kern = functools.partial(_post_kernel, tm=tm, d_model=d_model, d_ff=d_ff, chunked=chunked)
    return pl.pallas_call(
        kern,
        grid=(t // tm,),
        in_specs=[tok(d_model),
                  pl.BlockSpec((1,) + mod.shape[1:], lambda i: (i // tiles_per_seq, 0, 0)),
                  tok(GROUP_COLS), tok(SGU_WIDTH), tok(SGU_WIDTH), tok(d_model), tok(d_model),
                  _resident(ws.shape), _resident(bs.shape),
                  _resident_layer(wpa.shape, layer), _resident_layer(wps.shape, layer),
                  _resident_layer(wo.shape, layer), _resident(n2w.shape),
                  _resident_layer(wfi.shape, layer), _resident_layer(wfo.shape, layer)],
        out_specs=tok(d_model),
        out_shape=jax.ShapeDtypeStruct((t, d_model), F32),
        scratch_shapes=[pltpu.VMEM((tm, SGU_WIDTH), F32)],
        compiler_params=_params(1),
        name="post_chunked" if chunked else "post_tok",
    )(x, mod, ya, u, vs, ga, gb, ws, bs, wpa, wps, wo, n2w, wfi, wfo)


def kernel(x_prompt, x_sample, cache_kv_w128, cache_kv_w512, cache_kv_w2048, c_prompt, c_sample,
           w_ada, b_ada, norm1_w, w_in, q_norm_w, k_norm_w, sgu_ln_w, sgu_ln_b, w_spatial, b_spatial,
           w_proj_att, w_proj_sgu, w_out, norm2_w, w_ffn_in, w_ffn_out):
    batch, seq, d_model = x_prompt.shape
    bd, dec_seq, _ = x_sample.shape
    depth = w_in.shape[0]
    assert dec_seq == 1 and seq % (N_STEPS * DILATIONS[-1]) == 0
    caches_in = (cache_kv_w128, cache_kv_w512, cache_kv_w2048)
    for g in range(N_GROUPS):
        assert caches_in[g].shape[2] == WINDOWS[g]
    tm = 256

    n_c = batch + bd
    c_all = jnp.concatenate([c_prompt, c_sample], axis=0)
    c_all = jnp.pad(c_all, ((0, -n_c % 8), (0, 0)))
    mod = _ada_call(c_all, w_ada, b_ada)

    band_bias = jnp.asarray(_band_bias())
    tile = (2 * HEADS, HEAD_DIM)
    slopes8 = np.zeros((N_GROUPS,) + tile, np.float32)
    slopes8[:, :HEADS, :] = _alibi_slopes()[:, :, None]
    slopes8 = jnp.asarray(slopes8)
    caches = [caches_in[g].reshape((depth, bd, N_STEPS, DILATIONS[g] * tile[0], tile[1]))
              for g in range(N_GROUPS)]

    xp = x_prompt.reshape(batch * seq, d_model)
    xs = x_sample.reshape(bd, d_model)
    p_states = None
    s_states = [[] for _ in range(N_GROUPS)]
    s_v = []
    w_in_l = w_in.astype(BF16)
    wpa, wps, wo = w_proj_att.astype(BF16), w_proj_sgu.astype(BF16), w_out.astype(BF16)
    wfi, wfo = w_ffn_in.astype(BF16), w_ffn_out.astype(BF16)
    for l in range(depth):
        qw = jnp.tile(q_norm_w[l][:, None, :], (1, HEADS, 1)).reshape(1, QKV_COLS)
        kw = jnp.tile(k_norm_w[l][:, None, :], (1, HEADS, 1)).reshape(1, QKV_COLS)
        lnw, lnb = sgu_ln_w[l].reshape(1, -1), sgu_ln_b[l].reshape(1, -1)
        n1w, n2w = norm1_w[l].reshape(1, -1), norm2_w[l].reshape(1, -1)
        mod_p = mod[l, :batch].reshape(batch, 1, -1)
        mod_s = mod[l, batch:n_c].reshape(1, bd, -1)
        bs_p = jnp.repeat(b_spatial[l].T, SGU_GROUP_DIM, axis=1)
        ws_s = jnp.repeat(w_spatial[l][:, 0, 0], SGU_GROUP_DIM).reshape(1, SGU_WIDTH)
        bs_s = jnp.repeat(b_spatial[l][:, 0], SGU_GROUP_DIM).reshape(1, SGU_WIDTH)

        outs = _inproj_call(xp, mod_p, n1w, w_in_l, qw, kw, lnw, lnb,
                            seq=seq, tm=tm, deinterleave=True, act_dtype=BF16,
                            layer=l, depth=depth, prev_states=p_states)
        qkv, (u, vs, ga, gb), p_states = outs[:9], outs[9:13], outs[13:]
        ya = _attn_call(qkv, band_bias, batch=batch, seq=seq)
        xp = _post_call(xp, mod_p, ya, u, vs, ga, gb, w_spatial[l], bs_p, wpa, wps, wo, n2w, wfi, wfo,
                        seq=seq, tm=tm, chunked=True, layer=l)

        outs = _inproj_call(xs, mod_s, n1w, w_in_l, qw, kw, lnw, lnb,
                            seq=bd, tm=bd, deinterleave=False, act_dtype=F32, layer=l)
        qkv, (u, vs, ga, gb), sts = outs[:9], outs[9:13], outs[13:]
        q8 = jnp.stack([qkv[0], qkv[3], qkv[6]], axis=1).reshape(bd, N_GROUPS, HEADS, HEAD_DIM)
        q8 = jnp.pad(q8, ((0, 0), (0, 0), (0, HEADS), (0, 0)))
        kv8 = jnp.stack(sts, axis=1).reshape((bd, N_GROUPS) + tile)
        ya = _attn_sample_call(q8, kv8, caches, l, slopes8)[:, HEADS:, :].reshape(bd, GROUP_COLS)
        xs = _post_call(xs, mod_s, ya.astype(BF16), u, vs, ga, gb, ws_s, bs_s, wpa, wps, wo, n2w, wfi, wfo,
                        seq=bd, tm=bd, chunked=False, layer=l)
        for g in range(N_GROUPS):
            s_states[g].append(sts[g].reshape(bd, 1, 2, HEADS, HEAD_DIM))
        s_v.append(vs.reshape(bd, 1, SGU_WIDTH))

    p_states = [p_states[g].reshape(depth, batch, -1, 2, HEADS, HEAD_DIM) for g in range(N_GROUPS)]
    return (xp.reshape(batch, seq, d_model), xs.reshape(bd, 1, d_model),
            p_states[0], p_states[1], p_states[2],
            jnp.stack(s_states[0]), jnp.stack(s_states[1]), jnp.stack(s_states[2]),
            jnp.stack(s_v))
```

```python
import functools
import math

import jax
import jax.numpy as jnp
import numpy as np
from jax.experimental import pallas as pl
from jax.experimental.pallas import tpu as pltpu

HEAD_DIM = 128
N_GROUPS = 3
HEADS = 4
GROUP_COLS = HEADS * HEAD_DIM
QKV_COLS = N_GROUPS * GROUP_COLS
WINDOWS = (128, 512, 2048)
DILATIONS = (1, 4, 16)
N_STEPS = 128
CHUNK = 128
SGU_WIDTH = 512
SGU_GROUPS = 8
SGU_GROUP_DIM = SGU_WIDTH // SGU_GROUPS
N_ADA = 6
EPS = 1e-6
NEG_INF = -1e30
QK_SCALE = HEAD_DIM ** -0.5

LANES = 128
VMEM_LIMIT_BYTES = 56 * 1024 * 1024

BF16 = jnp.bfloat16
F32 = jnp.float32


def _params(n_axes, flags=None):
    return pltpu.CompilerParams(dimension_semantics=("arbitrary",) * n_axes,
                                vmem_limit_bytes=VMEM_LIMIT_BYTES, flags=flags)


def _resident(shape):
    nd = len(shape)
    return pl.BlockSpec(shape, lambda *_: (0,) * nd, pipeline_mode=pl.Buffered(1))


def _resident_layer(shape, layer):
    nd = len(shape)
    return pl.BlockSpec((None,) + tuple(shape[1:]), lambda *_: (layer,) + (0,) * (nd - 1),
                        pipeline_mode=pl.Buffered(1))


def _dot(a, b):
    return jnp.dot(a, b, preferred_element_type=F32)


def _dot_nt(a, b):
    return jax.lax.dot_general(a, b, (((1,), (1,)), ((), ())), preferred_element_type=F32)


def _rms(x):
    return x * jax.lax.rsqrt(jnp.mean(x * x, axis=-1, keepdims=True) + EPS)


def _gelu(x):
    return 0.5 * x * (1.0 + jax.lax.erf(x * (2.0 ** -0.5)))


def _sigmoid(x):
    return 1.0 / (1.0 + jnp.exp(-x))


def _ada_kernel(c_ref, w_ref, b_ref, o_ref):
    c = c_ref[...]
    a = c * _sigmoid(c)
    o_ref[...] = jnp.dot(a, w_ref[...], preferred_element_type=F32,
                         precision=jax.lax.Precision.HIGHEST) + b_ref[...]


def _ada_call(c_all, w_ada, b_ada):
    depth, d, n = w_ada.shape
    rows = c_all.shape[0]
    tn = 1024
    return pl.pallas_call(
        _ada_kernel,
        grid=(depth, n // tn),
        in_specs=[pl.BlockSpec((rows, d), lambda l, j: (0, 0)),
                  pl.BlockSpec((None, d, tn), lambda l, j: (l, 0, j)),
                  pl.BlockSpec((None, 1, tn), lambda l, j: (l, 0, j))],
        out_specs=pl.BlockSpec((None, rows, tn), lambda l, j: (l, 0, j)),
        out_shape=jax.ShapeDtypeStruct((depth, rows, n), F32),
        compiler_params=_params(2),
        name="adaln_mod",
    )(c_all, w_ada, b_ada.reshape(depth, 1, n))


def _modulated_norm(x, w, mod, which, d_model):
    shift = mod[:, 3 * which * d_model:(3 * which + 1) * d_model]
    scale = mod[:, (3 * which + 1) * d_model:(3 * which + 2) * d_model]
    return _rms(x) * w * (1.0 + scale) + shift


def _inproj_kernel(x_ref, mod_ref, n1w_ref, w_ref, qw_ref, kw_ref,
                   *refs, tm, d_model, dils, state_rows, tiles_per_seq, deinterleave, n_aliased):
    (q0, k0, v0, q1, k1, v1, q2, k2, v2, st0, st1, st2, h_scr, slab, kv_stash) = refs[n_aliased:]
    q_refs, k_refs, v_refs, st_refs = (q0, q1, q2), (k0, k1, k2), (v0, v1, v2), (st0, st1, st2)
    ti = pl.program_id(0) % tiles_per_seq

    h = _modulated_norm(x_ref[...], n1w_ref[...], mod_ref[0], 0, d_model)
    h_scr[...] = h.astype(BF16)

    def proj(c0, width):
        return _dot(h_scr[...], w_ref[:, c0:c0 + width])

    def put(ref, g, val):
        d = dils[g]
        if d == 1 or not deinterleave:
            ref[...] = val.astype(ref.dtype)
            return
        for hh in range(HEADS):
            slab[hh] = val[:, hh * HEAD_DIM:(hh + 1) * HEAD_DIM]
        for r in range(d):
            for hh in range(HEADS):
                ref[0, r, :, hh * HEAD_DIM:(hh + 1) * HEAD_DIM] = (
                    slab[hh, pl.ds(r, tm // d, stride=d), :].astype(ref.dtype))

    def head_rms(z, w):
        parts = [_rms(z[:, hh * HEAD_DIM:(hh + 1) * HEAD_DIM]) for hh in range(HEADS)]
        return jnp.concatenate(parts, axis=-1) * w

    def put_state(g, kn, v):
        if state_rows[g] is None:
            st_refs[g][:, 0:GROUP_COLS] = kn
            st_refs[g][:, GROUP_COLS:2 * GROUP_COLS] = v
        else:
            for hh in range(HEADS):
                cols = slice(hh * HEAD_DIM, (hh + 1) * HEAD_DIM)
                kv_stash[g, hh] = kn[:, cols]
                kv_stash[g, HEADS + hh] = v[:, cols]

    def flush_state(g):
        st, rows = st_refs[g], min(state_rows[g], tm)
        first = tiles_per_seq - max(state_rows[g] // tm, 1)

        @pl.when(ti >= first)
        def _():
            for j in range(2 * HEADS):
                st[pl.ds(j, rows, stride=2 * HEADS), :] = kv_stash[g, j, tm - rows:, :]

    for g in reversed(range(N_GROUPS)):
        c0 = g * GROUP_COLS
        q = head_rms(proj(c0, GROUP_COLS), qw_ref[:, c0:c0 + GROUP_COLS] * QK_SCALE)
        put(q_refs[g], g, q)
        kn = head_rms(proj(QKV_COLS + c0, GROUP_COLS), kw_ref[:, c0:c0 + GROUP_COLS])
        put(k_refs[g], g, kn)
        v = proj(2 * QKV_COLS + c0, GROUP_COLS)
        put(v_refs[g], g, v)
        put_state(g, kn, v)

    for g in range(N_GROUPS):
        if state_rows[g] is not None:
            flush_state(g)


def _inproj_call(x, mod, n1w, w_in, qw, kw, *, seq, tm, deinterleave, act_dtype,
                 layer=0, depth=1, prev_states=None):
    t, d_model = x.shape
    batch = t // seq
    tiles_per_seq = seq // tm

    def tok(cols, dtype):
        return (jax.ShapeDtypeStruct((t, cols), dtype), pl.BlockSpec((tm, cols), lambda i: (i, 0)))

    def grp(g, dtype):
        d = DILATIONS[g]
        if d == 1 or not deinterleave:
            return tok(GROUP_COLS, dtype)
        assert tm % d == 0 and (tm // d) % 16 == 0
        return (jax.ShapeDtypeStruct((batch, d, seq // d, GROUP_COLS), dtype),
                pl.BlockSpec((1, d, tm // d, GROUP_COLS),
                             lambda i: (i // tiles_per_seq, 0, i % tiles_per_seq, 0)))

    outs = []
    for g in range(N_GROUPS):
        outs += [grp(g, act_dtype)] * 3

    state_rows = []
    for g in range(N_GROUPS):
        if not deinterleave:
            state_rows.append(None)
            outs.append(tok(2 * GROUP_COLS, F32))
            continue
        rows = min(WINDOWS[g], seq)
        state_rows.append(rows)
        blk_rows = min(rows, tm)
        assert rows % blk_rows == 0
        nblk = rows // blk_rows
        shape = jax.ShapeDtypeStruct((depth * batch * rows * 2 * HEADS, LANES), F32)
        spec = pl.BlockSpec(
            (blk_rows * 2 * HEADS, LANES),
            lambda i, nblk=nblk: ((layer * batch + i // tiles_per_seq) * nblk
                                  + jnp.maximum(i % tiles_per_seq - (tiles_per_seq - nblk), 0), 0))
        outs.append((shape, spec))

    aliased = list(prev_states) if prev_states is not None else []
    n_in = 6
    w_qkv = pl.BlockSpec((None, d_model, 3 * QKV_COLS), lambda i: (layer, 0, 0),
                         pipeline_mode=pl.Buffered(1))
    kern = functools.partial(_inproj_kernel, tm=tm, d_model=d_model, dils=DILATIONS,
                             state_rows=tuple(state_rows), tiles_per_seq=tiles_per_seq,
                             deinterleave=deinterleave, n_aliased=len(aliased))
    return pl.pallas_call(
        kern,
        grid=(t // tm,),
        in_specs=[pl.BlockSpec((tm, d_model), lambda i: (i, 0)),
                  pl.BlockSpec((1,) + mod.shape[1:], lambda i: (i // tiles_per_seq, 0, 0)),
                  _resident((1, d_model)),
                  w_qkv,
                  _resident((1, QKV_COLS)), _resident((1, QKV_COLS))]
                 + [pl.BlockSpec(memory_space=pl.ANY)] * len(aliased),
        out_specs=[o[1] for o in outs],
        out_shape=[o[0] for o in outs],
        input_output_aliases={n_in + g: 3 * N_GROUPS + g for g in range(len(aliased))},
        scratch_shapes=[pltpu.VMEM((tm, d_model), BF16),
                        pltpu.VMEM((HEADS, tm, LANES), F32),
                        pltpu.VMEM((N_GROUPS, 2 * HEADS, tm, LANES), F32)],
        compiler_params=_params(1),
        name="inproj_deint" if deinterleave else "inproj_tok",
    )(x, mod, n1w, w_in, qw, kw, *aliased)


def _alibi_slopes():
    h = np.arange(1, N_GROUPS * HEADS + 1, dtype=np.float32)
    s = np.power(np.float32(2.0), -8.0 * h / (N_GROUPS * HEADS)).astype(np.float32)
    return s.reshape(N_GROUPS, HEADS)


def _band_bias():
    qi = np.arange(N_STEPS)[:, None]
    ki = np.arange(2 * N_STEPS)[None, :]
    dist = N_STEPS + qi - ki
    valid = (dist >= 0) & (dist <= N_STEPS)
    slopes = _alibi_slopes()
    out = np.empty((N_GROUPS, HEADS, 2, N_STEPS, 2 * N_STEPS), np.float32)
    for g in range(N_GROUPS):
        for hh in range(HEADS):
            b = np.where(valid, -slopes[g, hh] * (dist * DILATIONS[g]).astype(np.float32),
                         np.float32(NEG_INF)).astype(np.float32)
            out[g, hh, 1] = b
            first = b.copy()
            first[:, :N_STEPS] = NEG_INF
            out[g, hh, 0] = first
    return out


def _attn_block(q, k2, v2, bias):
    s = _dot_nt(q, k2) + bias
    m = jnp.max(s, axis=-1, keepdims=True)
    p = jnp.exp(s - m)
    l = jnp.sum(p, axis=-1, keepdims=True)
    return _dot(p.astype(v2.dtype), v2), m, l


ATTN_UNROLL = 32


def _attn_kernel(q0, k0, v0, q1, k1, v1, q2, k2, v2, bias_ref, o_ref, o_scr, m_scr, l_scr, *, seq):
    blk = N_STEPS

    def run_group(g, qr, kr, vr):
        d = DILATIONS[g]
        nblk = seq // d // blk

        def one_block(r, i):
            own = pl.multiple_of(i * blk, blk)
            prev = pl.multiple_of(jnp.maximum(i - 1, 0) * blk, blk)
            if d == 1:
                ld = lambda ref, start: ref[pl.ds(start, blk), :]
            else:
                ld = lambda ref, start: ref[0, r, pl.ds(start, blk), :]
            k2_ = jnp.concatenate([ld(kr, prev), ld(kr, own)], axis=0)
            v2_ = jnp.concatenate([ld(vr, prev), ld(vr, own)], axis=0)
            o, m, l = _attn_block(ld(qr, own), k2_, v2_, bias_ref[g, 0, jnp.minimum(i, 1)])
            m = jnp.broadcast_to(m, (blk, LANES))
            l = jnp.broadcast_to(l, (blk, LANES))
            if d > 1:
                rows = pl.ds(i * (blk * d) + r, blk, stride=d)
                o_scr[g - 1, rows, :] = o
                m_scr[g - 1, rows, :] = m
                l_scr[g - 1, rows, :] = l
                return
            rows = pl.ds(own, blk)
            ms = [m, m_scr[0, rows, :], m_scr[1, rows, :]]
            mx = jnp.maximum(jnp.maximum(ms[0], ms[1]), ms[2])
            ws = [jnp.exp(x - mx) for x in ms]
            num = ws[0] * o + ws[1] * o_scr[0, rows, :] + ws[2] * o_scr[1, rows, :]
            den = ws[0] * l + ws[1] * l_scr[0, rows, :] + ws[2] * l_scr[1, rows, :]
            o_ref[rows, :] = (num / den).astype(o_ref.dtype)

        ur = min(ATTN_UNROLL, d)
        ui = ATTN_UNROLL // ur
        assert d % ur == 0 and nblk % ui == 0

        def body(it, carry):
            i0 = (it // (d // ur)) * ui
            r0 = (it % (d // ur)) * ur
            for a in range(ui):
                for b in range(ur):
                    one_block(r0 + b, i0 + a)
            return carry

        jax.lax.fori_loop(0, (d // ur) * (nblk // ui), body, 0)

    run_group(1, q1, k1, v1)
    run_group(2, q2, k2, v2)
    run_group(0, q0, k0, v0)


def _attn_call(qkv, bias, *, batch, seq):
    in_specs = []
    for g in range(N_GROUPS):
        d = DILATIONS[g]
        if d == 1:
            spec = pl.BlockSpec((seq, HEAD_DIM), lambda b, h: (b, h))
        else:
            spec = pl.BlockSpec((1, d, seq // d, HEAD_DIM), lambda b, h: (b, 0, 0, h))
        in_specs += [spec] * 3
    in_specs.append(pl.BlockSpec((N_GROUPS, 1, 2, N_STEPS, 2 * N_STEPS), lambda b, h: (0, h, 0, 0, 0)))
    return pl.pallas_call(
        functools.partial(_attn_kernel, seq=seq),
        grid=(batch, HEADS),
        in_specs=in_specs,
        out_specs=pl.BlockSpec((seq, HEAD_DIM), lambda b, h: (b, h)),
        out_shape=jax.ShapeDtypeStruct((batch * seq, GROUP_COLS), BF16),
        scratch_shapes=[pltpu.VMEM((N_GROUPS - 1, seq, LANES), F32)] * 3,
        compiler_params=_params(2),
        name="attn_prompt",
    )(*qkv, bias)


def _attn_sample_kernel(q_ref, kv_ref, c0, c1, c2, slope_ref, o_ref, *, bb):
    c_refs = (c0, c1, c2)
    back = (N_STEPS - jax.lax.broadcasted_iota(jnp.int32, (N_STEPS, 1, 1), 0)).astype(F32)

    def to_v_rows(stat):
        return pltpu.roll(jnp.broadcast_to(stat, (2 * HEADS, HEAD_DIM)), HEADS, axis=0)

    for b in range(bb):
        outs, lses = [], []
        for g in range(N_GROUPS):
            q8, kv8, x = q_ref[b, g], kv_ref[b, g], c_refs[g][b]
            bias = (back * (-float(DILATIONS[g]))) * slope_ref[g][:, 0:1]
            s = jnp.sum(x * q8, axis=-1, keepdims=True) + bias
            s_self = jnp.sum(kv8 * q8, axis=-1, keepdims=True)
            m = jnp.maximum(jnp.max(s, axis=0), s_self)
            p = jnp.exp(s - m)
            p_self = jnp.exp(s_self - m)
            l = jnp.sum(p, axis=0) + p_self
            p_v = pltpu.roll(jnp.broadcast_to(p, x.shape), HEADS, axis=1)
            acc = jnp.sum(p_v * x, axis=0) + to_v_rows(p_self) * kv8
            outs.append(acc / to_v_rows(l))
            lses.append(to_v_rows(m + jnp.log(l)))
        mx = jnp.maximum(jnp.maximum(lses[0], lses[1]), lses[2])
        ws = [jnp.exp(v - mx) for v in lses]
        num = ws[0] * outs[0] + ws[1] * outs[1] + ws[2] * outs[2]
        o_ref[b] = num / (ws[0] + ws[1] + ws[2])


def _attn_sample_call(q8, kv8, caches, layer, slopes8):
    bd = q8.shape[0]
    bb = 8
    tile = (2 * HEADS, HEAD_DIM)
    in_specs = [pl.BlockSpec((bb, N_GROUPS) + tile, lambda i: (i, 0, 0, 0))] * 2
    in_specs += [pl.BlockSpec((None, bb, N_STEPS) + tile, lambda i: (layer, i, 0, 0, 0))] * 3
    in_specs.append(pl.BlockSpec((N_GROUPS,) + tile, lambda i: (0, 0, 0)))
    return pl.pallas_call(
        functools.partial(_attn_sample_kernel, bb=bb),
        grid=(bd // bb,),
        in_specs=in_specs,
        out_specs=pl.BlockSpec((bb,) + tile, lambda i: (i, 0, 0)),
        out_shape=jax.ShapeDtypeStruct((bd,) + tile, F32),
        compiler_params=_params(1),
        name="attn_sample",
    )(q8, kv8, *caches, slopes8)


def _post_kernel(x_ref, mod_ref, ya_ref, n1w_ref, wg1_ref, wg2_ref, lnw_ref, lnb_ref, ws_ref, bs_ref,
                 wpa_ref, wps_ref, wo_ref, n2w_ref, wfi_ref, wfo_ref, *refs,
                 tm, d_model, d_ff, chunked):
    if chunked:
        o_ref, mix_scr = refs
    else:
        o_ref, vs_out_ref, mix_scr = refs
    mod = mod_ref[0]
    g1 = mod[:, 2 * d_model:3 * d_model]
    g2 = mod[:, 5 * d_model:6 * d_model]
    x = x_ref[...]

    h = _modulated_norm(x, n1w_ref[...], mod, 0, d_model).astype(BF16)
    w = SGU_WIDTH
    u = _gelu(_dot(h, wg1_ref[:, 0:w]))
    vs = _gelu(_dot(h, wg1_ref[:, w:2 * w]))
    mu = jnp.mean(vs, axis=-1, keepdims=True)
    vc = vs - mu
    var = jnp.mean(vc * vc, axis=-1, keepdims=True)
    vs = vc * jax.lax.rsqrt(var + EPS) * lnw_ref[...] + lnb_ref[...]
    half = d_model // 2
    ga = jnp.concatenate([_sigmoid(_dot(h, wg1_ref[:, 2 * w:2 * w + half])),
                          _sigmoid(_dot(h, wg2_ref[:, 0:half]))], axis=-1)
    gb = _sigmoid(_dot(h, wg2_ref[:, half:half + d_model]))

    if chunked:
        row = jax.lax.broadcasted_iota(jnp.int32, (CHUNK, CHUNK), 0)
        col = jax.lax.broadcasted_iota(jnp.int32, (CHUNK, CHUNK), 1)
        tril = col <= row
        lane = jax.lax.broadcasted_iota(jnp.int32, (CHUNK, LANES), 1)
        low = lane < SGU_GROUP_DIM
        wms = [jnp.where(tril, ws_ref[gi], 0.0).astype(BF16) for gi in range(SGU_GROUPS)]
        vs_b = vs.astype(BF16)
        for c in range(tm // CHUNK):
            rows = slice(c * CHUNK, (c + 1) * CHUNK)
            for j in range(SGU_WIDTH // LANES):
                cols = slice(j * LANES, (j + 1) * LANES)
                v = vs_b[rows, cols]
                mix_scr[rows, cols] = jnp.where(low, _dot(wms[2 * j], v), _dot(wms[2 * j + 1], v))
        mixed = mix_scr[...] + jnp.concatenate([bs_ref[...]] * (tm // CHUNK), axis=0)
    else:
        vs_out_ref[...] = vs
        mixed = vs * ws_ref[...] + bs_ref[...]

    ys = (u * mixed).astype(BF16)
    xa = _dot(ya_ref[...], wpa_ref[...])
    xs = _dot(ys, wps_ref[...])
    merged = (ga * xa + gb * xs).astype(BF16)
    x1 = x + g1 * _dot(merged, wo_ref[...])
    h2 = _modulated_norm(x1, n2w_ref[...], mod, 1, d_model).astype(BF16)
    a = _dot(h2, wfi_ref[:, 0:d_ff])
    b = _dot(h2, wfi_ref[:, d_ff:2 * d_ff])
    act = (a * _sigmoid(a) * b).astype(BF16)
    o_ref[...] = x1 + g2 * _dot(act, wfo_ref[...])


def _post_call(x, mod, ya, n1w, w_in, lnw, lnb, ws, bs, wpa, wps, wo, n2w, wfi, wfo, *,
               seq, tm, chunked, layer):
    t, d_model = x.shape
    d_ff = wfo.shape[1]
    tiles_per_seq = seq // tm
    gblk = QKV_COLS
    assert 2 * SGU_WIDTH + d_model // 2 == gblk and w_in.shape[2] == 5 * gblk

    def w_gate(j):
        return pl.BlockSpec((None, d_model, gblk), lambda i: (layer, 0, 3 + j),
                            pipeline_mode=pl.Buffered(1))

    def tok(cols):
        return pl.BlockSpec((tm, cols), lambda i: (i, 0))

    out_specs, out_shape = [tok(d_model)], [jax.ShapeDtypeStruct((t, d_model), F32)]
    if not chunked:
        out_specs.append(tok(SGU_WIDTH))
        out_shape.append(jax.ShapeDtypeStruct((t, SGU_WIDTH), F32))
    kern = functools.partial(_post_kernel, tm=tm, d_model=d_model, d_ff=d_ff, chunked=chunked)
    return pl.pallas_call(
        kern,
        grid=(t // tm,),
        in_specs=[tok(d_model),
                  pl.BlockSpec((1,) + mod.shape[1:], lambda i: (i // tiles_per_seq, 0, 0)),
                  tok(GROUP_COLS), _resident(n1w.shape), w_gate(0), w_gate(1),
                  _resident(lnw.shape), _resident(lnb.shape),
                  _resident(ws.shape), _resident(bs.shape),
                  _resident_layer(wpa.shape, layer), _resident_layer(wps.shape, layer),
                  _resident_layer(wo.shape, layer), _resident(n2w.shape),
                  _resident_layer(wfi.shape, layer), _resident_layer(wfo.shape, layer)],
        out_specs=out_specs,
        out_shape=out_shape,
        scratch_shapes=[pltpu.VMEM((tm, SGU_WIDTH), F32)],
        compiler_params=_params(1),
        name="post_chunked" if chunked else "post_tok",
    )(x, mod, ya, n1w, w_in, w_in, lnw, lnb, ws, bs, wpa, wps, wo, n2w, wfi, wfo)


def kernel(x_prompt, x_sample, cache_kv_w128, cache_kv_w512, cache_kv_w2048, c_prompt, c_sample,
           w_ada, b_ada, norm1_w, w_in, q_norm_w, k_norm_w, sgu_ln_w, sgu_ln_b, w_spatial, b_spatial,
           w_proj_att, w_proj_sgu, w_out, norm2_w, w_ffn_in, w_ffn_out):
    batch, seq, d_model = x_prompt.shape
    bd, dec_seq, _ = x_sample.shape
    depth = w_in.shape[0]
    assert dec_seq == 1 and seq % (N_STEPS * DILATIONS[-1]) == 0
    caches_in = (cache_kv_w128, cache_kv_w512, cache_kv_w2048)
    for g in range(N_GROUPS):
        assert caches_in[g].shape[2] == WINDOWS[g]
    tm = 512
    tm_in = 512

    n_c = batch + bd
    c_all = jnp.concatenate([c_prompt, c_sample], axis=0)
    c_all = jnp.pad(c_all, ((0, -n_c % 8), (0, 0)))
    mod = _ada_call(c_all, w_ada, b_ada)

    band_bias = jnp.asarray(_band_bias())
    tile = (2 * HEADS, HEAD_DIM)
    slopes8 = np.zeros((N_GROUPS,) + tile, np.float32)
    slopes8[:, :HEADS, :] = _alibi_slopes()[:, :, None]
    slopes8 = jnp.asarray(slopes8)
    caches = [caches_in[g].reshape((depth, bd, N_STEPS, DILATIONS[g] * tile[0], tile[1]))
              for g in range(N_GROUPS)]

    xp = x_prompt.reshape(batch * seq, d_model)
    xs = x_sample.reshape(bd, d_model)
    p_states = None
    s_states = [[] for _ in range(N_GROUPS)]
    s_v = []
    w_in_l = w_in.astype(BF16)
    wpa, wps, wo = w_proj_att.astype(BF16), w_proj_sgu.astype(BF16), w_out.astype(BF16)
    wfi, wfo = w_ffn_in.astype(BF16), w_ffn_out.astype(BF16)
    for l in range(depth):
        qw = jnp.tile(q_norm_w[l][:, None, :], (1, HEADS, 1)).reshape(1, QKV_COLS)
        kw = jnp.tile(k_norm_w[l][:, None, :], (1, HEADS, 1)).reshape(1, QKV_COLS)
        lnw, lnb = sgu_ln_w[l].reshape(1, -1), sgu_ln_b[l].reshape(1, -1)
        n1w, n2w = norm1_w[l].reshape(1, -1), norm2_w[l].reshape(1, -1)
        mod_p = mod[l, :batch].reshape(batch, 1, -1)
        mod_s = mod[l, batch:n_c].reshape(1, bd, -1)
        bs_p = jnp.repeat(b_spatial[l].T, SGU_GROUP_DIM, axis=1)
        ws_s = jnp.repeat(w_spatial[l][:, 0, 0], SGU_GROUP_DIM).reshape(1, SGU_WIDTH)
        bs_s = jnp.repeat(b_spatial[l][:, 0], SGU_GROUP_DIM).reshape(1, SGU_WIDTH)

        outs = _inproj_call(xp, mod_p, n1w, w_in_l, qw, kw,
                            seq=seq, tm=tm_in, deinterleave=True, act_dtype=BF16,
                            layer=l, depth=depth, prev_states=p_states)
        qkv, p_states = outs[:9], outs[9:]
        ya = _attn_call(qkv, band_bias, batch=batch, seq=seq)
        (xp,) = _post_call(xp, mod_p, ya, n1w, w_in_l, lnw, lnb, w_spatial[l], bs_p,
                           wpa, wps, wo, n2w, wfi, wfo, seq=seq, tm=tm, chunked=True, layer=l)

        outs = _inproj_call(xs, mod_s, n1w, w_in_l, qw, kw,
                            seq=bd, tm=bd, deinterleave=False, act_dtype=F32, layer=l)
        qkv, sts = outs[:9], outs[9:]
        q8 = jnp.stack([qkv[0], qkv[3], qkv[6]], axis=1).reshape(bd, N_GROUPS, HEADS, HEAD_DIM)
        q8 = jnp.pad(q8, ((0, 0), (0, 0), (0, HEADS), (0, 0)))
        kv8 = jnp.stack(sts, axis=1).reshape((bd, N_GROUPS) + tile)
        ya = _attn_sample_call(q8, kv8, caches, l, slopes8)[:, HEADS:, :].reshape(bd, GROUP_COLS)
        xs, vs = _post_call(xs, mod_s, ya.astype(BF16), n1w, w_in_l, lnw, lnb, ws_s, bs_s,
                            wpa, wps, wo, n2w, wfi, wfo, seq=bd, tm=bd, chunked=False, layer=l)
        for g in range(N_GROUPS):
            s_states[g].append(sts[g].reshape(bd, 1, 2, HEADS, HEAD_DIM))
        s_v.append(vs.reshape(bd, 1, SGU_WIDTH))

    p_states = [p_states[g].reshape(depth, batch, -1, 2, HEADS, HEAD_DIM) for g in range(N_GROUPS)]
    return (xp.reshape(batch, seq, d_model), xs.reshape(bd, 1, d_model),
            p_states[0], p_states[1], p_states[2],
            jnp.stack(s_states[0]), jnp.stack(s_states[1]), jnp.stack(s_states[2]),
            jnp.stack(s_v))
```

```python
import functools
import math

import jax
import jax.numpy as jnp
import numpy as np
from jax.experimental import pallas as pl
from jax.experimental.pallas import tpu as pltpu

HEAD_DIM = 128
N_GROUPS = 3
HEADS = 4
GROUP_COLS = HEADS * HEAD_DIM
QKV_COLS = N_GROUPS * GROUP_COLS
WINDOWS = (128, 512, 2048)
DILATIONS = (1, 4, 16)
N_STEPS = 128
CHUNK = 128
SGU_WIDTH = 512
SGU_GROUPS = 8
SGU_GROUP_DIM = SGU_WIDTH // SGU_GROUPS
N_ADA = 6
EPS = 1e-6
NEG_INF = -1e30
QK_SCALE = HEAD_DIM ** -0.5

LANES = 128
VMEM_LIMIT_BYTES = 56 * 1024 * 1024

BF16 = jnp.bfloat16
F32 = jnp.float32


def _params(n_axes, flags=None):
    return pltpu.CompilerParams(dimension_semantics=("arbitrary",) * n_axes,
                                vmem_limit_bytes=VMEM_LIMIT_BYTES, flags=flags)


def _resident(shape):
    nd = len(shape)
    return pl.BlockSpec(shape, lambda *_: (0,) * nd, pipeline_mode=pl.Buffered(1))


def _resident_layer(shape, layer):
    nd = len(shape)
    return pl.BlockSpec((None,) + tuple(shape[1:]), lambda *_: (layer,) + (0,) * (nd - 1),
                        pipeline_mode=pl.Buffered(1))


def _dot(a, b):
    return jnp.dot(a, b, preferred_element_type=F32)


def _dot_nt(a, b):
    return jax.lax.dot_general(a, b, (((1,), (1,)), ((), ())), preferred_element_type=F32)


def _rms(x):
    return x * jax.lax.rsqrt(jnp.mean(x * x, axis=-1, keepdims=True) + EPS)


def _gelu(x):
    return 0.5 * x * (1.0 + jax.lax.erf(x * (2.0 ** -0.5)))


def _sigmoid(x):
    return 1.0 / (1.0 + jnp.exp(-x))


def _ada_kernel(c_ref, w_ref, b_ref, o_ref):
    c = c_ref[...]
    a = c * _sigmoid(c)
    w = w_ref[...]
    a_hi = a.astype(BF16)
    a_lo = (a - a_hi.astype(F32)).astype(BF16)
    w_hi = w.astype(BF16)
    w_lo = (w - w_hi.astype(F32)).astype(BF16)
    rows = a.shape[0]
    both = _dot(jnp.concatenate([a_hi, a_lo], axis=0), w_hi)
    o_ref[...] = both[:rows] + both[rows:] + _dot(a_hi, w_lo) + b_ref[...]


def _ada_call(c_all, w_ada, b_ada):
    depth, d, n = w_ada.shape
    rows = c_all.shape[0]
    tn = 1024
    return pl.pallas_call(
        _ada_kernel,
        grid=(depth, n // tn),
        in_specs=[pl.BlockSpec((rows, d), lambda l, j: (0, 0)),
                  pl.BlockSpec((None, d, tn), lambda l, j: (l, 0, j)),
                  pl.BlockSpec((None, 1, tn), lambda l, j: (l, 0, j))],
        out_specs=pl.BlockSpec((None, rows, tn), lambda l, j: (l, 0, j)),
        out_shape=jax.ShapeDtypeStruct((depth, rows, n), F32),
        compiler_params=_params(2),
        name="adaln_mod",
    )(c_all, w_ada, b_ada.reshape(depth, 1, n))


def _modulated_norm(x, w, mod, which, d_model):
    shift = mod[:, 3 * which * d_model:(3 * which + 1) * d_model]
    scale = mod[:, (3 * which + 1) * d_model:(3 * which + 2) * d_model]
    return _rms(x) * w * (1.0 + scale) + shift


def _inproj_kernel(x_ref, mod_ref, n1w_ref, w_ref, qw_ref, kw_ref,
                   *refs, tm, d_model, dils, state_rows, tiles_per_seq, deinterleave, n_aliased):
    refs = refs[n_aliased:]
    if deinterleave:
        q_refs, k_refs, v_refs, st_refs = refs[0:9:3], refs[1:9:3], refs[2:9:3], refs[9:12]
        h_scr, slab, kv_stash = refs[12:]
    else:
        q_refs, st_refs = refs[0:3], refs[3:6]
        h_scr, slab, kv_stash = refs[6:]
    ti = pl.program_id(0) % tiles_per_seq
    tile_rows = 2 * HEADS

    h = _modulated_norm(x_ref[...], n1w_ref[...], mod_ref[0], 0, d_model)
    h_scr[...] = h.astype(BF16)

    def proj(c0, width):
        return _dot(h_scr[...], w_ref[:, c0:c0 + width])

    def put_tiles(ref, first_row, val):
        for hh in range(HEADS):
            ref[pl.ds(first_row + hh, tm, stride=tile_rows), :] = val[:, hh * HEAD_DIM:(hh + 1) * HEAD_DIM]

    def put(ref, g, val, slabs=None):
        d = dils[g]
        if d == 1:
            ref[...] = val.astype(ref.dtype)
            return
        if slabs is None:
            slabs = [slab.at[g - 1, hh] for hh in range(HEADS)]
            for hh in range(HEADS):
                slabs[hh][...] = val[:, hh * HEAD_DIM:(hh + 1) * HEAD_DIM]
        for r in range(d):
            for hh in range(HEADS):
                ref[0, r, :, hh * HEAD_DIM:(hh + 1) * HEAD_DIM] = (
                    slabs[hh][pl.ds(r, tm // d, stride=d), :].astype(ref.dtype))

    def head_rms(z, w):
        parts = [_rms(z[:, hh * HEAD_DIM:(hh + 1) * HEAD_DIM]) for hh in range(HEADS)]
        return jnp.concatenate(parts, axis=-1) * w

    def put_state(g, kn, v):
        if state_rows[g] is None:
            put_tiles(st_refs[g], 0, kn)
            put_tiles(st_refs[g], HEADS, v)
        else:
            for hh in range(HEADS):
                cols = slice(hh * HEAD_DIM, (hh + 1) * HEAD_DIM)
                kv_stash[g, hh] = kn[:, cols]
                kv_stash[g, HEADS + hh] = v[:, cols]

    def flush_state(g):
        st, rows = st_refs[g], min(state_rows[g], tm)
        first = tiles_per_seq - max(state_rows[g] // tm, 1)

        @pl.when(ti >= first)
        def _():
            for j in range(2 * HEADS):
                st[pl.ds(j, rows, stride=2 * HEADS), :] = kv_stash[g, j, tm - rows:, :]

    for g in reversed(range(N_GROUPS)):
        c0 = g * GROUP_COLS
        q = head_rms(proj(c0, GROUP_COLS), qw_ref[:, c0:c0 + GROUP_COLS] * QK_SCALE)
        kn = head_rms(proj(QKV_COLS + c0, GROUP_COLS), kw_ref[:, c0:c0 + GROUP_COLS])
        v = proj(2 * QKV_COLS + c0, GROUP_COLS)
        put_state(g, kn, v)
        if not deinterleave:
            put_tiles(q_refs[g], 0, q)
            put_tiles(q_refs[g], HEADS, jnp.zeros_like(q))
            continue
        put(q_refs[g], g, q)
        put(k_refs[g], g, kn, [kv_stash.at[g, hh] for hh in range(HEADS)])
        put(v_refs[g], g, v, [kv_stash.at[g, HEADS + hh] for hh in range(HEADS)])

    for g in range(N_GROUPS):
        if state_rows[g] is not None:
            flush_state(g)


def _inproj_call(x, mod, n1w, w_in, qw, kw, *, seq, tm, deinterleave, act_dtype,
                 layer=0, depth=1, prev_states=None):
    t, d_model = x.shape
    batch = t // seq
    tiles_per_seq = seq // tm

    def tok(cols, dtype):
        return (jax.ShapeDtypeStruct((t, cols), dtype), pl.BlockSpec((tm, cols), lambda i: (i, 0)))

    def grp(g, dtype):
        d = DILATIONS[g]
        if d == 1:
            return tok(GROUP_COLS, dtype)
        assert tm % d == 0 and (tm // d) % 16 == 0
        return (jax.ShapeDtypeStruct((batch, d, seq // d, GROUP_COLS), dtype),
                pl.BlockSpec((1, d, tm // d, GROUP_COLS),
                             lambda i: (i // tiles_per_seq, 0, i % tiles_per_seq, 0)))

    tile_rows = 2 * HEADS
    outs = []
    for g in range(N_GROUPS):
        if deinterleave:
            outs += [grp(g, act_dtype)] * 3
        else:
            assert t == tm
            outs.append((jax.ShapeDtypeStruct((t * tile_rows, LANES), F32),
                         pl.BlockSpec((t * tile_rows, LANES), lambda i: (0, 0))))

    state_rows = []
    for g in range(N_GROUPS):
        if not deinterleave:
            state_rows.append(None)
            outs.append((jax.ShapeDtypeStruct((depth * t * tile_rows, LANES), F32),
                         pl.BlockSpec((t * tile_rows, LANES), lambda i: (layer, 0))))
            continue
        rows = min(WINDOWS[g], seq)
        state_rows.append(rows)
        blk_rows = min(rows, tm)
        assert rows % blk_rows == 0
        nblk = rows // blk_rows
        shape = jax.ShapeDtypeStruct((depth * batch * rows * 2 * HEADS, LANES), F32)
        spec = pl.BlockSpec(
            (blk_rows * 2 * HEADS, LANES),
            lambda i, nblk=nblk: ((layer * batch + i // tiles_per_seq) * nblk
                                  + jnp.maximum(i % tiles_per_seq - (tiles_per_seq - nblk), 0), 0))
        outs.append((shape, spec))

    aliased = list(prev_states) if prev_states is not None else []
    n_in = 6
    w_qkv = pl.BlockSpec((None, d_model, 3 * QKV_COLS), lambda i: (layer, 0, 0),
                         pipeline_mode=pl.Buffered(1))
    kern = functools.partial(_inproj_kernel, tm=tm, d_model=d_model, dils=DILATIONS,
                             state_rows=tuple(state_rows), tiles_per_seq=tiles_per_seq,
                             deinterleave=deinterleave, n_aliased=len(aliased))
    return pl.pallas_call(
        kern,
        grid=(t // tm,),
        in_specs=[pl.BlockSpec((tm, d_model), lambda i: (i, 0)),
                  pl.BlockSpec((1,) + mod.shape[1:], lambda i: (i // tiles_per_seq, 0, 0)),
                  _resident((1, d_model)),
                  w_qkv,
                  _resident((1, QKV_COLS)), _resident((1, QKV_COLS))]
                 + [pl.BlockSpec(memory_space=pl.ANY)] * len(aliased),
        out_specs=[o[1] for o in outs],
        out_shape=[o[0] for o in outs],
        input_output_aliases={n_in + g: len(outs) - N_GROUPS + g for g in range(len(aliased))},
        scratch_shapes=[pltpu.VMEM((tm, d_model), BF16),
                        pltpu.VMEM((N_GROUPS - 1, HEADS, tm, LANES), F32),
                        pltpu.VMEM((N_GROUPS, 2 * HEADS, tm, LANES), F32)],
        compiler_params=_params(1),
        name="inproj_deint" if deinterleave else "inproj_tok",
    )(x, mod, n1w, w_in, qw, kw, *aliased)


def _alibi_slopes():
    h = np.arange(1, N_GROUPS * HEADS + 1, dtype=np.float32)
    s = np.power(np.float32(2.0), -8.0 * h / (N_GROUPS * HEADS)).astype(np.float32)
    return s.reshape(N_GROUPS, HEADS)


def _band_bias():
    qi = np.arange(N_STEPS)[:, None]
    ki = np.arange(2 * N_STEPS)[None, :]
    dist = N_STEPS + qi - ki
    valid = (dist >= 0) & (dist <= N_STEPS)
    slopes = _alibi_slopes()
    out = np.empty((N_GROUPS, HEADS, 2, N_STEPS, 2 * N_STEPS), np.float32)
    for g in range(N_GROUPS):
        for hh in range(HEADS):
            b = np.where(valid, -slopes[g, hh] * (dist * DILATIONS[g]).astype(np.float32),
                         np.float32(NEG_INF)).astype(np.float32)
            out[g, hh, 1] = b
            first = b.copy()
            first[:, :N_STEPS] = NEG_INF
            out[g, hh, 0] = first
    return out


def _attn_block(q, k2, v2, bias):
    s = _dot_nt(q, k2) + bias
    m = jnp.max(s, axis=-1, keepdims=True)
    p = jnp.exp(s - m)
    l = jnp.sum(p, axis=-1, keepdims=True)
    return _dot(p.astype(v2.dtype), v2), m, l


ATTN_UNROLL = 32


def _attn_kernel(q0, k0, v0, q1, k1, v1, q2, k2, v2, bias_ref, o_ref, o_scr, m_scr, l_scr, *, seq):
    blk = N_STEPS

    def run_group(g, qr, kr, vr):
        d = DILATIONS[g]
        nblk = seq // d // blk

        def one_block(r, i):
            own = pl.multiple_of(i * blk, blk)
            prev = pl.multiple_of(jnp.maximum(i - 1, 0) * blk, blk)
            if d == 1:
                ld = lambda ref, start: ref[pl.ds(start, blk), :]
            else:
                ld = lambda ref, start: ref[0, r, pl.ds(start, blk), :]
            k2_ = jnp.concatenate([ld(kr, prev), ld(kr, own)], axis=0)
            v2_ = jnp.concatenate([ld(vr, prev), ld(vr, own)], axis=0)
            o, m, l = _attn_block(ld(qr, own), k2_, v2_, bias_ref[g, 0, jnp.minimum(i, 1)])
            m = jnp.broadcast_to(m, (blk, LANES))
            l = jnp.broadcast_to(l, (blk, LANES))
            if d > 1:
                rows = pl.ds(i * (blk * d) + r, blk, stride=d)
                o_scr[g - 1, rows, :] = o
                m_scr[g - 1, rows, :] = m
                l_scr[g - 1, rows, :] = l
                return
            rows = pl.ds(own, blk)
            ms = [m, m_scr[0, rows, :], m_scr[1, rows, :]]
            mx = jnp.maximum(jnp.maximum(ms[0], ms[1]), ms[2])
            ws = [jnp.exp(x - mx) for x in ms]
            num = ws[0] * o + ws[1] * o_scr[0, rows, :] + ws[2] * o_scr[1, rows, :]
            den = ws[0] * l + ws[1] * l_scr[0, rows, :] + ws[2] * l_scr[1, rows, :]
            o_ref[rows, :] = (num / den).astype(o_ref.dtype)

        ur = min(ATTN_UNROLL, d)
        ui = ATTN_UNROLL // ur
        assert d % ur == 0 and nblk % ui == 0

        def body(it, carry):
            i0 = (it // (d // ur)) * ui
            r0 = (it % (d // ur)) * ur
            for a in range(ui):
                for b in range(ur):
                    one_block(r0 + b, i0 + a)
            return carry

        jax.lax.fori_loop(0, (d // ur) * (nblk // ui), body, 0)

    run_group(1, q1, k1, v1)
    run_group(2, q2, k2, v2)
    run_group(0, q0, k0, v0)


def _attn_call(qkv, bias, *, batch, seq):
    in_specs = []
    for g in range(N_GROUPS):
        d = DILATIONS[g]
        if d == 1:
            spec = pl.BlockSpec((seq, HEAD_DIM), lambda b, h: (b, h))
        else:
            spec = pl.BlockSpec((1, d, seq // d, HEAD_DIM), lambda b, h: (b, 0, 0, h))
        in_specs += [spec] * 3
    in_specs.append(pl.BlockSpec((N_GROUPS, 1, 2, N_STEPS, 2 * N_STEPS), lambda b, h: (0, h, 0, 0, 0)))
    return pl.pallas_call(
        functools.partial(_attn_kernel, seq=seq),
        grid=(batch, HEADS),
        in_specs=in_specs,
        out_specs=pl.BlockSpec((seq, HEAD_DIM), lambda b, h: (b, h)),
        out_shape=jax.ShapeDtypeStruct((batch * seq, GROUP_COLS), BF16),
        scratch_shapes=[pltpu.VMEM((N_GROUPS - 1, seq, LANES), F32)] * 3,
        compiler_params=_params(2),
        name="attn_prompt",
    )(*qkv, bias)


def _attn_sample_kernel(q0, q1, q2, n0, n1, n2, c0, c1, c2, slope_ref, o_ref, *, bb):
    q_refs, n_refs, c_refs = (q0, q1, q2), (n0, n1, n2), (c0, c1, c2)
    back = (N_STEPS - jax.lax.broadcasted_iota(jnp.int32, (N_STEPS, 1, 1), 0)).astype(F32)

    def to_v_rows(stat):
        return pltpu.roll(jnp.broadcast_to(stat, (2 * HEADS, HEAD_DIM)), HEADS, axis=0)

    for b in range(bb):
        outs, lses = [], []
        for g in range(N_GROUPS):
            q8, kv8, x = q_refs[g][b], n_refs[g][b], c_refs[g][b]
            bias = (back * (-float(DILATIONS[g]))) * slope_ref[g][:, 0:1]
            s = jnp.sum(x * q8, axis=-1, keepdims=True) + bias
            s_self = jnp.sum(kv8 * q8, axis=-1, keepdims=True)
            m = jnp.maximum(jnp.max(s, axis=0), s_self)
            p = jnp.exp(s - m)
            p_self = jnp.exp(s_self - m)
            l = jnp.sum(p, axis=0) + p_self
            p_v = pltpu.roll(jnp.broadcast_to(p, x.shape), HEADS, axis=1)
            acc = jnp.sum(p_v * x, axis=0) + to_v_rows(p_self) * kv8
            outs.append(acc / to_v_rows(l))
            lses.append(to_v_rows(m + jnp.log(l)))
        mx = jnp.maximum(jnp.maximum(lses[0], lses[1]), lses[2])
        ws = [jnp.exp(v - mx) for v in lses]
        num = ws[0] * outs[0] + ws[1] * outs[1] + ws[2] * outs[2]
        o8 = num / (ws[0] + ws[1] + ws[2])
        for hh in range(HEADS):
            o_ref[b:b + 1, hh * HEAD_DIM:(hh + 1) * HEAD_DIM] = o8[HEADS + hh:HEADS + hh + 1, :]


def _attn_sample_call(q_tiles, new_tiles, caches, layer, slopes8):
    bd = q_tiles[0].shape[0]
    bb = 8
    nb = bd // bb
    tile = (2 * HEADS, HEAD_DIM)
    in_specs = [pl.BlockSpec((bb,) + tile, lambda i: (i, 0, 0))] * 3
    in_specs += [pl.BlockSpec((bb,) + tile, lambda i: (layer * nb + i, 0, 0))] * 3
    in_specs += [pl.BlockSpec((None, bb, N_STEPS) + tile, lambda i: (layer, i, 0, 0, 0))] * 3
    in_specs.append(pl.BlockSpec((N_GROUPS,) + tile, lambda i: (0, 0, 0)))
    return pl.pallas_call(
        functools.partial(_attn_sample_kernel, bb=bb),
        grid=(nb,),
        in_specs=in_specs,
        out_specs=pl.BlockSpec((bb, GROUP_COLS), lambda i: (i, 0)),
        out_shape=jax.ShapeDtypeStruct((bd, GROUP_COLS), F32),
        compiler_params=_params(1),
        name="attn_sample",
    )(*q_tiles, *new_tiles, *caches, slopes8)


def _post_kernel(x_ref, mod_ref, ya_ref, n1w_ref, wg1_ref, wg2_ref, lnw_ref, lnb_ref, ws_ref, bs_ref,
                 wpa_ref, wps_ref, wo_ref, n2w_ref, wfi_ref, wfo_ref, *refs,
                 tm, d_model, d_ff, chunked):
    if chunked:
        o_ref, mix_scr = refs
    else:
        o_ref, vs_out_ref, mix_scr = refs
    mod = mod_ref[0]
    g1 = mod[:, 2 * d_model:3 * d_model]
    g2 = mod[:, 5 * d_model:6 * d_model]
    x = x_ref[...]

    h = _modulated_norm(x, n1w_ref[...], mod, 0, d_model).astype(BF16)
    w = SGU_WIDTH
    u = _gelu(_dot(h, wg1_ref[:, 0:w]))
    vs = _gelu(_dot(h, wg1_ref[:, w:2 * w]))
    mu = jnp.mean(vs, axis=-1, keepdims=True)
    vc = vs - mu
    var = jnp.mean(vc * vc, axis=-1, keepdims=True)
    vs = vc * jax.lax.rsqrt(var + EPS) * lnw_ref[...] + lnb_ref[...]
    half = d_model // 2
    ga = jnp.concatenate([_sigmoid(_dot(h, wg1_ref[:, 2 * w:2 * w + half])),
                          _sigmoid(_dot(h, wg2_ref[:, 0:half]))], axis=-1)
    gb = _sigmoid(_dot(h, wg2_ref[:, half:half + d_model]))

    if chunked:
        row = jax.lax.broadcasted_iota(jnp.int32, (CHUNK, CHUNK), 0)
        col = jax.lax.broadcasted_iota(jnp.int32, (CHUNK, CHUNK), 1)
        tril = col <= row
        lane = jax.lax.broadcasted_iota(jnp.int32, (CHUNK, LANES), 1)
        low = lane < SGU_GROUP_DIM
        wms = [jnp.where(tril, ws_ref[gi], 0.0).astype(BF16) for gi in range(SGU_GROUPS)]
        vs_b = vs.astype(BF16)
        for c in range(tm // CHUNK):
            rows = slice(c * CHUNK, (c + 1) * CHUNK)
            for j in range(SGU_WIDTH // LANES):
                cols = slice(j * LANES, (j + 1) * LANES)
                v = vs_b[rows, cols]
                mix_scr[rows, cols] = jnp.where(low, _dot(wms[2 * j], v), _dot(wms[2 * j + 1], v))
        mixed = mix_scr[...] + jnp.concatenate([bs_ref[...]] * (tm // CHUNK), axis=0)
    else:
        vs_out_ref[...] = vs
        mixed = vs * ws_ref[...] + bs_ref[...]

    ys = (u * mixed).astype(BF16)
    xa = _dot(ya_ref[...].astype(BF16), wpa_ref[...])
    xs = _dot(ys, wps_ref[...])
    merged = (ga * xa + gb * xs).astype(BF16)
    x1 = x + g1 * _dot(merged, wo_ref[...])
    h2 = _modulated_norm(x1, n2w_ref[...], mod, 1, d_model).astype(BF16)
    a = _dot(h2, wfi_ref[:, 0:d_ff])
    b = _dot(h2, wfi_ref[:, d_ff:2 * d_ff])
    act = (a * _sigmoid(a) * b).astype(BF16)
    o_ref[...] = x1 + g2 * _dot(act, wfo_ref[...])


def _post_call(x, mod, ya, n1w, w_in, lnw, lnb, ws, bs, wpa, wps, wo, n2w, wfi, wfo, *,
               seq, tm, chunked, layer):
    t, d_model = x.shape
    d_ff = wfo.shape[1]
    tiles_per_seq = seq // tm
    gblk = QKV_COLS
    assert 2 * SGU_WIDTH + d_model // 2 == gblk and w_in.shape[2] == 5 * gblk

    def w_gate(j):
        return pl.BlockSpec((None, d_model, gblk), lambda i: (layer, 0, 3 + j),
                            pipeline_mode=pl.Buffered(1))

    def tok(cols):
        return pl.BlockSpec((tm, cols), lambda i: (i, 0))

    out_specs, out_shape = [tok(d_model)], [jax.ShapeDtypeStruct((t, d_model), F32)]
    if not chunked:
        out_specs.append(tok(SGU_WIDTH))
        out_shape.append(jax.ShapeDtypeStruct((t, SGU_WIDTH), F32))
    kern = functools.partial(_post_kernel, tm=tm, d_model=d_model, d_ff=d_ff, chunked=chunked)
    return pl.pallas_call(
        kern,
        grid=(t // tm,),
        in_specs=[tok(d_model),
                  pl.BlockSpec((1,) + mod.shape[1:], lambda i: (i // tiles_per_seq, 0, 0)),
                  tok(GROUP_COLS), _resident(n1w.shape), w_gate(0), w_gate(1),
                  _resident(lnw.shape), _resident(lnb.shape),
                  _resident(ws.shape), _resident(bs.shape),
                  _resident_layer(wpa.shape, layer), _resident_layer(wps.shape, layer),
                  _resident_layer(wo.shape, layer), _resident(n2w.shape),
                  _resident_layer(wfi.shape, layer), _resident_layer(wfo.shape, layer)],
        out_specs=out_specs,
        out_shape=out_shape,
        scratch_shapes=[pltpu.VMEM((tm, SGU_WIDTH), F32)],
        compiler_params=_params(1),
        name="post_chunked" if chunked else "post_tok",
    )(x, mod, ya, n1w, w_in, w_in, lnw, lnb, ws, bs, wpa, wps, wo, n2w, wfi, wfo)


def kernel(x_prompt, x_sample, cache_kv_w128, cache_kv_w512, cache_kv_w2048, c_prompt, c_sample,
           w_ada, b_ada, norm1_w, w_in, q_norm_w, k_norm_w, sgu_ln_w, sgu_ln_b, w_spatial, b_spatial,
           w_proj_att, w_proj_sgu, w_out, norm2_w, w_ffn_in, w_ffn_out):
    batch, seq, d_model = x_prompt.shape
    bd, dec_seq, _ = x_sample.shape
    depth = w_in.shape[0]
    assert dec_seq == 1 and seq % (N_STEPS * DILATIONS[-1]) == 0
    caches_in = (cache_kv_w128, cache_kv_w512, cache_kv_w2048)
    for g in range(N_GROUPS):
        assert caches_in[g].shape[2] == WINDOWS[g]
    tm = 512
    tm_in = 512

    n_c = batch + bd
    c_all = jnp.concatenate([c_prompt, c_sample], axis=0)
    c_all = jnp.pad(c_all, ((0, -n_c % 16), (0, 0)))
    mod = _ada_call(c_all, w_ada, b_ada)

    band_bias = jnp.asarray(_band_bias())
    tile = (2 * HEADS, HEAD_DIM)
    slopes8 = np.zeros((N_GROUPS,) + tile, np.float32)
    slopes8[:, :HEADS, :] = _alibi_slopes()[:, :, None]
    slopes8 = jnp.asarray(slopes8)
    caches = [caches_in[g].reshape((depth, bd, N_STEPS, DILATIONS[g] * tile[0], tile[1]))
              for g in range(N_GROUPS)]

    xp = x_prompt.reshape(batch * seq, d_model)
    xs = x_sample.reshape(bd, d_model)
    p_states = s_states = None
    s_v = []
    w_in_l = w_in.astype(BF16)
    wpa, wps, wo = w_proj_att.astype(BF16), w_proj_sgu.astype(BF16), w_out.astype(BF16)
    wfi, wfo = w_ffn_in.astype(BF16), w_ffn_out.astype(BF16)
    for l in range(depth):
        qw = jnp.tile(q_norm_w[l][:, None, :], (1, HEADS, 1)).reshape(1, QKV_COLS)
        kw = jnp.tile(k_norm_w[l][:, None, :], (1, HEADS, 1)).reshape(1, QKV_COLS)
        lnw, lnb = sgu_ln_w[l].reshape(1, -1), sgu_ln_b[l].reshape(1, -1)
        n1w, n2w = norm1_w[l].reshape(1, -1), norm2_w[l].reshape(1, -1)
        mod_p = mod[l, :batch].reshape(batch, 1, -1)
        mod_s = mod[l, batch:n_c].reshape(1, bd, -1)
        bs_p = jnp.repeat(b_spatial[l].T, SGU_GROUP_DIM, axis=1)
        ws_s = jnp.repeat(w_spatial[l][:, 0, 0], SGU_GROUP_DIM).reshape(1, SGU_WIDTH)
        bs_s = jnp.repeat(b_spatial[l][:, 0], SGU_GROUP_DIM).reshape(1, SGU_WIDTH)

        outs = _inproj_call(xp, mod_p, n1w, w_in_l, qw, kw,
                            seq=seq, tm=tm_in, deinterleave=True, act_dtype=BF16,
                            layer=l, depth=depth, prev_states=p_states)
        qkv, p_states = outs[:9], outs[9:]
        ya = _attn_call(qkv, band_bias, batch=batch, seq=seq)
        (xp,) = _post_call(xp, mod_p, ya, n1w, w_in_l, lnw, lnb, w_spatial[l], bs_p,
                           wpa, wps, wo, n2w, wfi, wfo, seq=seq, tm=tm, chunked=True, layer=l)

        outs = _inproj_call(xs, mod_s, n1w, w_in_l, qw, kw,
                            seq=bd, tm=bd, deinterleave=False, act_dtype=F32,
                            layer=l, depth=depth, prev_states=s_states)
        q_tiles, s_states = [o.reshape((bd,) + tile) for o in outs[:3]], outs[3:]
        new_tiles = [s.reshape((depth * bd,) + tile) for s in s_states]
        ya = _attn_sample_call(q_tiles, new_tiles, caches, l, slopes8)
        xs, vs = _post_call(xs, mod_s, ya, n1w, w_in_l, lnw, lnb, ws_s, bs_s,
                            wpa, wps, wo, n2w, wfi, wfo, seq=bd, tm=bd, chunked=False, layer=l)
        s_v.append(vs.reshape(bd, 1, SGU_WIDTH))

    p_states = [s.reshape(depth, batch, -1, 2, HEADS, HEAD_DIM) for s in p_states]
    s_states = [s.reshape(depth, bd, 1, 2, HEADS, HEAD_DIM) for s in s_states]
    return (xp.reshape(batch, seq, d_model), xs.reshape(bd, 1, d_model),
            p_states[0], p_states[1], p_states[2], s_states[0], s_states[1], s_states[2],
            jnp.stack(s_v))
```

```python
import functools
import math

import jax
import jax.numpy as jnp
import numpy as np
from jax.experimental import pallas as pl
from jax.experimental.pallas import tpu as pltpu

HEAD_DIM = 128
N_GROUPS = 3
HEADS = 4
GROUP_COLS = HEADS * HEAD_DIM
QKV_COLS = N_GROUPS * GROUP_COLS
WINDOWS = (128, 512, 2048)
DILATIONS = (1, 4, 16)
N_STEPS = 128
CHUNK = 128
SGU_WIDTH = 512
SGU_GROUPS = 8
SGU_GROUP_DIM = SGU_WIDTH // SGU_GROUPS
N_ADA = 6
EPS = 1e-6
NEG_INF = -1e30
QK_SCALE = HEAD_DIM ** -0.5

LANES = 128
RELAYOUT_STRIDE = 4
VMEM_LIMIT_BYTES = 56 * 1024 * 1024

BF16 = jnp.bfloat16
F32 = jnp.float32


def _params(n_axes, flags=None):
    return pltpu.CompilerParams(dimension_semantics=("arbitrary",) * n_axes,
                                vmem_limit_bytes=VMEM_LIMIT_BYTES, flags=flags)


def _resident(shape):
    nd = len(shape)
    return pl.BlockSpec(shape, lambda *_: (0,) * nd, pipeline_mode=pl.Buffered(1))


def _resident_layer(shape, layer):
    nd = len(shape)
    return pl.BlockSpec((None,) + tuple(shape[1:]), lambda *_: (layer,) + (0,) * (nd - 1),
                        pipeline_mode=pl.Buffered(1))


def _dot(a, b):
    return jnp.dot(a, b, preferred_element_type=F32)


def _dot_nt(a, b):
    return jax.lax.dot_general(a, b, (((1,), (1,)), ((), ())), preferred_element_type=F32)


def _rms(x):
    return x * jax.lax.rsqrt(jnp.mean(x * x, axis=-1, keepdims=True) + EPS)


def _gelu(x):
    return 0.5 * x * (1.0 + jax.lax.erf(x * (2.0 ** -0.5)))


def _sigmoid(x):
    return 1.0 / (1.0 + jnp.exp(-x))


def _ada_kernel(c_ref, w_ref, b_ref, o_ref):
    c = c_ref[...]
    a = c * _sigmoid(c)
    w = w_ref[...]
    a_hi = a.astype(BF16)
    a_lo = (a - a_hi.astype(F32)).astype(BF16)
    w_hi = w.astype(BF16)
    w_lo = (w - w_hi.astype(F32)).astype(BF16)
    rows = a.shape[0]
    both = _dot(jnp.concatenate([a_hi, a_lo], axis=0), w_hi)
    o_ref[...] = both[:rows] + both[rows:] + _dot(a_hi, w_lo) + b_ref[...]


def _ada_call(c_all, w_ada, b_ada):
    depth, d, n = w_ada.shape
    rows = c_all.shape[0]
    tn = 1024
    return pl.pallas_call(
        _ada_kernel,
        grid=(depth, n // tn),
        in_specs=[pl.BlockSpec((rows, d), lambda l, j: (0, 0)),
                  pl.BlockSpec((None, d, tn), lambda l, j: (l, 0, j)),
                  pl.BlockSpec((None, 1, tn), lambda l, j: (l, 0, j))],
        out_specs=pl.BlockSpec((None, rows, tn), lambda l, j: (l, 0, j)),
        out_shape=jax.ShapeDtypeStruct((depth, rows, n), F32),
        compiler_params=_params(2),
        name="adaln_mod",
    )(c_all, w_ada, b_ada.reshape(depth, 1, n))


def _modulated_norm(x, w, mod, which, d_model):
    shift = mod[:, 3 * which * d_model:(3 * which + 1) * d_model]
    scale = mod[:, (3 * which + 1) * d_model:(3 * which + 2) * d_model]
    return _rms(x) * w * (1.0 + scale) + shift


def _inproj_kernel(x_ref, mod_ref, n1w_ref, w_ref, qw_ref, kw_ref,
                   *refs, tm, d_model, dils, state_rows, tiles_per_seq, deinterleave, n_aliased):
    refs = refs[n_aliased:]
    if deinterleave:
        q_refs, k_refs, v_refs, st_refs = refs[0:9:3], refs[1:9:3], refs[2:9:3], refs[9:12]
        h_scr, slab, slab2, kv_stash = refs[12:]
    else:
        q_refs, st_refs = refs[0:3], refs[3:6]
        h_scr, slab, slab2, kv_stash = refs[6:]
    ti = pl.program_id(0) % tiles_per_seq
    tile_rows = 2 * HEADS

    h = _modulated_norm(x_ref[...], n1w_ref[...], mod_ref[0], 0, d_model)
    h_scr[...] = h.astype(BF16)

    def proj(c0, width):
        return _dot(h_scr[...], w_ref[:, c0:c0 + width])

    def put_tiles(ref, first_row, val):
        for hh in range(HEADS):
            ref[pl.ds(first_row + hh, tm, stride=tile_rows), :] = val[:, hh * HEAD_DIM:(hh + 1) * HEAD_DIM]

    def put(ref, g, val, slot, slabs=None):
        d = dils[g]
        if d == 1:
            ref[...] = val.astype(ref.dtype)
            return
        if slabs is None:
            slabs = [slab.at[g - 1, hh] for hh in range(HEADS)]
            for hh in range(HEADS):
                slabs[hh][...] = val[:, hh * HEAD_DIM:(hh + 1) * HEAD_DIM]
        if d <= RELAYOUT_STRIDE:
            for r in range(d):
                for hh in range(HEADS):
                    ref[0, r, :, hh * HEAD_DIM:(hh + 1) * HEAD_DIM] = (
                        slabs[hh][pl.ds(r, tm // d, stride=d), :].astype(ref.dtype))
            return
        s1, s2 = RELAYOUT_STRIDE, d // RELAYOUT_STRIDE
        part = tm // s1
        for hh in range(HEADS):
            for r1 in range(s1):
                slab2[slot, hh,r1 * part:(r1 + 1) * part, :] = slabs[hh][pl.ds(r1, part, stride=s1), :]
            for r1 in range(s1):
                for r2 in range(s2):
                    ref[0, r1 + s1 * r2, :, hh * HEAD_DIM:(hh + 1) * HEAD_DIM] = (
                        slab2[slot, hh,pl.ds(r1 * part + r2, tm // d, stride=s2), :].astype(ref.dtype))

    def head_rms(z, w):
        parts = [_rms(z[:, hh * HEAD_DIM:(hh + 1) * HEAD_DIM]) for hh in range(HEADS)]
        return jnp.concatenate(parts, axis=-1) * w

    def put_state(g, kn, v):
        if state_rows[g] is None:
            put_tiles(st_refs[g], 0, kn)
            put_tiles(st_refs[g], HEADS, v)
        else:
            for hh in range(HEADS):
                cols = slice(hh * HEAD_DIM, (hh + 1) * HEAD_DIM)
                kv_stash[g, hh] = kn[:, cols]
                kv_stash[g, HEADS + hh] = v[:, cols]

    def flush_state(g):
        st, rows = st_refs[g], min(state_rows[g], tm)
        first = tiles_per_seq - max(state_rows[g] // tm, 1)

        @pl.when(ti >= first)
        def _():
            for j in range(2 * HEADS):
                st[pl.ds(j, rows, stride=2 * HEADS), :] = kv_stash[g, j, tm - rows:, :]

    for g in reversed(range(N_GROUPS)):
        c0 = g * GROUP_COLS
        q = head_rms(proj(c0, GROUP_COLS), qw_ref[:, c0:c0 + GROUP_COLS] * QK_SCALE)
        kn = head_rms(proj(QKV_COLS + c0, GROUP_COLS), kw_ref[:, c0:c0 + GROUP_COLS])
        v = proj(2 * QKV_COLS + c0, GROUP_COLS)
        put_state(g, kn, v)
        if not deinterleave:
            put_tiles(q_refs[g], 0, q)
            put_tiles(q_refs[g], HEADS, jnp.zeros_like(q))
            continue
        put(q_refs[g], g, q, 0)
        put(k_refs[g], g, kn, 1, [kv_stash.at[g, hh] for hh in range(HEADS)])
        put(v_refs[g], g, v, 2, [kv_stash.at[g, HEADS + hh] for hh in range(HEADS)])

    for g in range(N_GROUPS):
        if state_rows[g] is not None:
            flush_state(g)


def _inproj_call(x, mod, n1w, w_in, qw, kw, *, seq, tm, deinterleave, act_dtype,
                 layer=0, depth=1, prev_states=None):
    t, d_model = x.shape
    batch = t // seq
    tiles_per_seq = seq // tm

    def tok(cols, dtype):
        return (jax.ShapeDtypeStruct((t, cols), dtype), pl.BlockSpec((tm, cols), lambda i: (i, 0)))

    def grp(g, dtype):
        d = DILATIONS[g]
        if d == 1:
            return tok(GROUP_COLS, dtype)
        assert tm % d == 0 and (tm // d) % 16 == 0
        return (jax.ShapeDtypeStruct((batch, d, seq // d, GROUP_COLS), dtype),
                pl.BlockSpec((1, d, tm // d, GROUP_COLS),
                             lambda i: (i // tiles_per_seq, 0, i % tiles_per_seq, 0)))

    tile_rows = 2 * HEADS
    outs = []
    for g in range(N_GROUPS):
        if deinterleave:
            outs += [grp(g, act_dtype)] * 3
        else:
            assert t == tm
            outs.append((jax.ShapeDtypeStruct((t * tile_rows, LANES), F32),
                         pl.BlockSpec((t * tile_rows, LANES), lambda i: (0, 0))))

    state_rows = []
    for g in range(N_GROUPS):
        if not deinterleave:
            state_rows.append(None)
            outs.append((jax.ShapeDtypeStruct((depth * t * tile_rows, LANES), F32),
                         pl.BlockSpec((t * tile_rows, LANES), lambda i: (layer, 0))))
            continue
        rows = min(WINDOWS[g], seq)
        state_rows.append(rows)
        blk_rows = min(rows, tm)
        assert rows % blk_rows == 0
        nblk = rows // blk_rows
        shape = jax.ShapeDtypeStruct((depth * batch * rows * 2 * HEADS, LANES), F32)
        spec = pl.BlockSpec(
            (blk_rows * 2 * HEADS, LANES),
            lambda i, nblk=nblk: ((layer * batch + i // tiles_per_seq) * nblk
                                  + jnp.maximum(i % tiles_per_seq - (tiles_per_seq - nblk), 0), 0))
        outs.append((shape, spec))

    aliased = list(prev_states) if prev_states is not None else []
    n_in = 6
    w_qkv = pl.BlockSpec((None, d_model, 3 * QKV_COLS), lambda i: (layer, 0, 0),
                         pipeline_mode=pl.Buffered(1))
    kern = functools.partial(_inproj_kernel, tm=tm, d_model=d_model, dils=DILATIONS,
                             state_rows=tuple(state_rows), tiles_per_seq=tiles_per_seq,
                             deinterleave=deinterleave, n_aliased=len(aliased))
    return pl.pallas_call(
        kern,
        grid=(t // tm,),
        in_specs=[pl.BlockSpec((tm, d_model), lambda i: (i, 0)),
                  pl.BlockSpec((1,) + mod.shape[1:], lambda i: (i // tiles_per_seq, 0, 0)),
                  _resident((1, d_model)),
                  w_qkv,
                  _resident((1, QKV_COLS)), _resident((1, QKV_COLS))]
                 + [pl.BlockSpec(memory_space=pl.ANY)] * len(aliased),
        out_specs=[o[1] for o in outs],
        out_shape=[o[0] for o in outs],
        input_output_aliases={n_in + g: len(outs) - N_GROUPS + g for g in range(len(aliased))},
        scratch_shapes=[pltpu.VMEM((tm, d_model), BF16),
                        pltpu.VMEM((N_GROUPS - 1, HEADS, tm, LANES), F32),
                        pltpu.VMEM((3, HEADS, tm, LANES), F32),
                        pltpu.VMEM((N_GROUPS, 2 * HEADS, tm, LANES), F32)],
        compiler_params=_params(1),
        name="inproj_deint" if deinterleave else "inproj_tok",
    )(x, mod, n1w, w_in, qw, kw, *aliased)


def _alibi_slopes():
    h = np.arange(1, N_GROUPS * HEADS + 1, dtype=np.float32)
    s = np.power(np.float32(2.0), -8.0 * h / (N_GROUPS * HEADS)).astype(np.float32)
    return s.reshape(N_GROUPS, HEADS)


def _band_bias():
    qi = np.arange(N_STEPS)[:, None]
    ki = np.arange(2 * N_STEPS)[None, :]
    dist = N_STEPS + qi - ki
    valid = (dist >= 0) & (dist <= N_STEPS)
    slopes = _alibi_slopes()
    out = np.empty((N_GROUPS, HEADS, 2, N_STEPS, 2 * N_STEPS), np.float32)
    for g in range(N_GROUPS):
        for hh in range(HEADS):
            b = np.where(valid, -slopes[g, hh] * (dist * DILATIONS[g]).astype(np.float32),
                         np.float32(NEG_INF)).astype(np.float32)
            out[g, hh, 1] = b
            first = b.copy()
            first[:, :N_STEPS] = NEG_INF
            out[g, hh, 0] = first
    return out


def _attn_block(q, k2, v2, bias):
    s = _dot_nt(q, k2) + bias
    m = jnp.max(s, axis=-1, keepdims=True)
    p = jnp.exp(s - m)
    l = jnp.sum(p, axis=-1, keepdims=True)
    return _dot(p.astype(v2.dtype), v2), m, l


ATTN_UNROLL = 32


def _attn_kernel(q0, k0, v0, q1, k1, v1, q2, k2, v2, bias_ref, o_ref, o_scr, m_scr, l_scr, *, seq):
    blk = N_STEPS

    def run_group(g, qr, kr, vr):
        d = DILATIONS[g]
        nblk = seq // d // blk

        def one_block(r, i):
            own = pl.multiple_of(i * blk, blk)
            prev = pl.multiple_of(jnp.maximum(i - 1, 0) * blk, blk)
            if d == 1:
                ld = lambda ref, start: ref[pl.ds(start, blk), :]
            else:
                ld = lambda ref, start: ref[0, r, pl.ds(start, blk), :]
            k2_ = jnp.concatenate([ld(kr, prev), ld(kr, own)], axis=0)
            v2_ = jnp.concatenate([ld(vr, prev), ld(vr, own)], axis=0)
            o, m, l = _attn_block(ld(qr, own), k2_, v2_, bias_ref[g, 0, jnp.minimum(i, 1)])
            m = jnp.broadcast_to(m, (blk, LANES))
            l = jnp.broadcast_to(l, (blk, LANES))
            if d > 1:
                rows = pl.ds(i * (blk * d) + r, blk, stride=d)
                o_scr[g - 1, rows, :] = o
                m_scr[g - 1, rows, :] = m
                l_scr[g - 1, rows, :] = l
                return
            rows = pl.ds(own, blk)
            ms = [m, m_scr[0, rows, :], m_scr[1, rows, :]]
            mx = jnp.maximum(jnp.maximum(ms[0], ms[1]), ms[2])
            ws = [jnp.exp(x - mx) for x in ms]
            num = ws[0] * o + ws[1] * o_scr[0, rows, :] + ws[2] * o_scr[1, rows, :]
            den = ws[0] * l + ws[1] * l_scr[0, rows, :] + ws[2] * l_scr[1, rows, :]
            o_ref[rows, :] = (num / den).astype(o_ref.dtype)

        ur = min(ATTN_UNROLL, d)
        ui = ATTN_UNROLL // ur
        assert d % ur == 0 and nblk % ui == 0

        def body(it, carry):
            i0 = (it // (d // ur)) * ui
            r0 = (it % (d // ur)) * ur
            for a in range(ui):
                for b in range(ur):
                    one_block(r0 + b, i0 + a)
            return carry

        jax.lax.fori_loop(0, (d // ur) * (nblk // ui), body, 0)

    run_group(1, q1, k1, v1)
    run_group(2, q2, k2, v2)
    run_group(0, q0, k0, v0)


def _attn_call(qkv, bias, *, batch, seq):
    in_specs = []
    for g in range(N_GROUPS):
        d = DILATIONS[g]
        if d == 1:
            spec = pl.BlockSpec((seq, HEAD_DIM), lambda b, h: (b, h))
        else:
            spec = pl.BlockSpec((1, d, seq // d, HEAD_DIM), lambda b, h: (b, 0, 0, h))
        in_specs += [spec] * 3
    in_specs.append(pl.BlockSpec((N_GROUPS, 1, 2, N_STEPS, 2 * N_STEPS), lambda b, h: (0, h, 0, 0, 0)))
    return pl.pallas_call(
        functools.partial(_attn_kernel, seq=seq),
        grid=(batch, HEADS),
        in_specs=in_specs,
        out_specs=pl.BlockSpec((seq, HEAD_DIM), lambda b, h: (b, h)),
        out_shape=jax.ShapeDtypeStruct((batch * seq, GROUP_COLS), BF16),
        scratch_shapes=[pltpu.VMEM((N_GROUPS - 1, seq, LANES), F32)] * 3,
        compiler_params=_params(2),
        name="attn_prompt",
    )(*qkv, bias)


def _attn_sample_kernel(q0, q1, q2, n0, n1, n2, c0, c1, c2, slope_ref, o_ref, *, bb):
    q_refs, n_refs, c_refs = (q0, q1, q2), (n0, n1, n2), (c0, c1, c2)
    back = (N_STEPS - jax.lax.broadcasted_iota(jnp.int32, (N_STEPS, 1, 1), 0)).astype(F32)

    def to_v_rows(stat):
        return pltpu.roll(jnp.broadcast_to(stat, (2 * HEADS, HEAD_DIM)), HEADS, axis=0)

    for b in range(bb):
        outs, lses = [], []
        for g in range(N_GROUPS):
            q8, kv8, x = q_refs[g][b], n_refs[g][b], c_refs[g][b]
            bias = (back * (-float(DILATIONS[g]))) * slope_ref[g][:, 0:1]
            s = jnp.sum(x * q8, axis=-1, keepdims=True) + bias
            s_self = jnp.sum(kv8 * q8, axis=-1, keepdims=True)
            m = jnp.maximum(jnp.max(s, axis=0), s_self)
            p = jnp.exp(s - m)
            p_self = jnp.exp(s_self - m)
            l = jnp.sum(p, axis=0) + p_self
            p_v = pltpu.roll(jnp.broadcast_to(p, x.shape), HEADS, axis=1)
            acc = jnp.sum(p_v * x, axis=0) + to_v_rows(p_self) * kv8
            outs.append(acc / to_v_rows(l))
            lses.append(to_v_rows(m + jnp.log(l)))
        mx = jnp.maximum(jnp.maximum(lses[0], lses[1]), lses[2])
        ws = [jnp.exp(v - mx) for v in lses]
        num = ws[0] * outs[0] + ws[1] * outs[1] + ws[2] * outs[2]
        o8 = num / (ws[0] + ws[1] + ws[2])
        for hh in range(HEADS):
            o_ref[b:b + 1, hh * HEAD_DIM:(hh + 1) * HEAD_DIM] = o8[HEADS + hh:HEADS + hh + 1, :]


def _attn_sample_call(q_tiles, new_tiles, caches, layer, slopes8):
    bd = q_tiles[0].shape[0]
    bb = 8
    nb = bd // bb
    tile = (2 * HEADS, HEAD_DIM)
    in_specs = [pl.BlockSpec((bb,) + tile, lambda i: (i, 0, 0))] * 3
    in_specs += [pl.BlockSpec((bb,) + tile, lambda i: (layer * nb + i, 0, 0))] * 3
    in_specs += [pl.BlockSpec((None, bb, N_STEPS) + tile, lambda i: (layer, i, 0, 0, 0))] * 3
    in_specs.append(pl.BlockSpec((N_GROUPS,) + tile, lambda i: (0, 0, 0)))
    return pl.pallas_call(
        functools.partial(_attn_sample_kernel, bb=bb),
        grid=(nb,),
        in_specs=in_specs,
        out_specs=pl.BlockSpec((bb, GROUP_COLS), lambda i: (i, 0)),
        out_shape=jax.ShapeDtypeStruct((bd, GROUP_COLS), F32),
        compiler_params=_params(1),
        name="attn_sample",
    )(*q_tiles, *new_tiles, *caches, slopes8)


def _post_kernel(x_ref, mod_ref, ya_ref, n1w_ref, wg1_ref, wg2_ref, lnw_ref, lnb_ref, ws_ref, bs_ref,
                 wpa_ref, wps_ref, wo_ref, n2w_ref, wfi_ref, wfo_ref, *refs,
                 tm, d_model, d_ff, chunked):
    if chunked:
        o_ref, mix_scr = refs
    else:
        o_ref, vs_out_ref, mix_scr = refs
    mod = mod_ref[0]
    g1 = mod[:, 2 * d_model:3 * d_model]
    g2 = mod[:, 5 * d_model:6 * d_model]
    x = x_ref[...]

    h = _modulated_norm(x, n1w_ref[...], mod, 0, d_model).astype(BF16)
    w = SGU_WIDTH
    u = _gelu(_dot(h, wg1_ref[:, 0:w]))
    vs = _gelu(_dot(h, wg1_ref[:, w:2 * w]))
    mu = jnp.mean(vs, axis=-1, keepdims=True)
    vc = vs - mu
    var = jnp.mean(vc * vc, axis=-1, keepdims=True)
    vs = vc * jax.lax.rsqrt(var + EPS) * lnw_ref[...] + lnb_ref[...]
    half = d_model // 2
    ga = jnp.concatenate([_sigmoid(_dot(h, wg1_ref[:, 2 * w:2 * w + half])),
                          _sigmoid(_dot(h, wg2_ref[:, 0:half]))], axis=-1)
    gb = _sigmoid(_dot(h, wg2_ref[:, half:half + d_model]))

    if chunked:
        row = jax.lax.broadcasted_iota(jnp.int32, (CHUNK, CHUNK), 0)
        col = jax.lax.broadcasted_iota(jnp.int32, (CHUNK, CHUNK), 1)
        tril = col <= row
        lane = jax.lax.broadcasted_iota(jnp.int32, (CHUNK, LANES), 1)
        low = lane < SGU_GROUP_DIM
        wms = [jnp.where(tril, ws_ref[gi], 0.0).astype(BF16) for gi in range(SGU_GROUPS)]
        vs_b = vs.astype(BF16)
        for c in range(tm // CHUNK):
            rows = slice(c * CHUNK, (c + 1) * CHUNK)
            for j in range(SGU_WIDTH // LANES):
                cols = slice(j * LANES, (j + 1) * LANES)
                v = vs_b[rows, cols]
                mix_scr[rows, cols] = jnp.where(low, _dot(wms[2 * j], v), _dot(wms[2 * j + 1], v))
        mixed = mix_scr[...] + jnp.concatenate([bs_ref[...]] * (tm // CHUNK), axis=0)
    else:
        vs_out_ref[...] = vs
        mixed = vs * ws_ref[...] + bs_ref[...]

    ys = (u * mixed).astype(BF16)
    xa = _dot(ya_ref[...].astype(BF16), wpa_ref[...])
    xs = _dot(ys, wps_ref[...])
    merged = (ga * xa + gb * xs).astype(BF16)
    x1 = x + g1 * _dot(merged, wo_ref[...])
    h2 = _modulated_norm(x1, n2w_ref[...], mod, 1, d_model).astype(BF16)
    a = _dot(h2, wfi_ref[:, 0:d_ff])
    b = _dot(h2, wfi_ref[:, d_ff:2 * d_ff])
    act = (a * _sigmoid(a) * b).astype(BF16)
    o_ref[...] = x1 + g2 * _dot(act, wfo_ref[...])


def _post_call(x, mod, ya, n1w, w_in, lnw, lnb, ws, bs, wpa, wps, wo, n2w, wfi, wfo, *,
               seq, tm, chunked, layer):
    t, d_model = x.shape
    d_ff = wfo.shape[1]
    tiles_per_seq = seq // tm
    gblk = QKV_COLS
    assert 2 * SGU_WIDTH + d_model // 2 == gblk and w_in.shape[2] == 5 * gblk

    def w_gate(j):
        return pl.BlockSpec((None, d_model, gblk), lambda i: (layer, 0, 3 + j),
                            pipeline_mode=pl.Buffered(1))

    def tok(cols):
        return pl.BlockSpec((tm, cols), lambda i: (i, 0))

    out_specs, out_shape = [tok(d_model)], [jax.ShapeDtypeStruct((t, d_model), F32)]
    if not chunked:
        out_specs.append(tok(SGU_WIDTH))
        out_shape.append(jax.ShapeDtypeStruct((t, SGU_WIDTH), F32))
    kern = functools.partial(_post_kernel, tm=tm, d_model=d_model, d_ff=d_ff, chunked=chunked)
    return pl.pallas_call(
        kern,
        grid=(t // tm,),
        in_specs=[tok(d_model),
                  pl.BlockSpec((1,) + mod.shape[1:], lambda i: (i // tiles_per_seq, 0, 0)),
                  tok(GROUP_COLS), _resident(n1w.shape), w_gate(0), w_gate(1),
                  _resident(lnw.shape), _resident(lnb.shape),
                  _resident(ws.shape), _resident(bs.shape),
                  _resident_layer(wpa.shape, layer), _resident_layer(wps.shape, layer),
                  _resident_layer(wo.shape, layer), _resident(n2w.shape),
                  _resident_layer(wfi.shape, layer), _resident_layer(wfo.shape, layer)],
        out_specs=out_specs,
        out_shape=out_shape,
        scratch_shapes=[pltpu.VMEM((tm, SGU_WIDTH), F32)],
        compiler_params=_params(1),
        name="post_chunked" if chunked else "post_tok",
    )(x, mod, ya, n1w, w_in, w_in, lnw, lnb, ws, bs, wpa, wps, wo, n2w, wfi, wfo)


def kernel(x_prompt, x_sample, cache_kv_w128, cache_kv_w512, cache_kv_w2048, c_prompt, c_sample,
           w_ada, b_ada, norm1_w, w_in, q_norm_w, k_norm_w, sgu_ln_w, sgu_ln_b, w_spatial, b_spatial,
           w_proj_att, w_proj_sgu, w_out, norm2_w, w_ffn_in, w_ffn_out):
    batch, seq, d_model = x_prompt.shape
    bd, dec_seq, _ = x_sample.shape
    depth = w_in.shape[0]
    assert dec_seq == 1 and seq % (N_STEPS * DILATIONS[-1]) == 0
    caches_in = (cache_kv_w128, cache_kv_w512, cache_kv_w2048)
    for g in range(N_GROUPS):
        assert caches_in[g].shape[2] == WINDOWS[g]
    tm = 512
    tm_in = 512

    n_c = batch + bd
    c_all = jnp.concatenate([c_prompt, c_sample], axis=0)
    c_all = jnp.pad(c_all, ((0, -n_c % 16), (0, 0)))
    mod = _ada_call(c_all, w_ada, b_ada)

    band_bias = jnp.asarray(_band_bias())
    tile = (2 * HEADS, HEAD_DIM)
    slopes8 = np.zeros((N_GROUPS,) + tile, np.float32)
    slopes8[:, :HEADS, :] = _alibi_slopes()[:, :, None]
    slopes8 = jnp.asarray(slopes8)
    caches = [caches_in[g].reshape((depth, bd, N_STEPS, DILATIONS[g] * tile[0], tile[1]))
              for g in range(N_GROUPS)]

    xp = x_prompt.reshape(batch * seq, d_model)
    xs = x_sample.reshape(bd, d_model)
    p_states = s_states = None
    s_v = []
    w_in_l = w_in.astype(BF16)
    wpa, wps, wo = w_proj_att.astype(BF16), w_proj_sgu.astype(BF16), w_out.astype(BF16)
    wfi, wfo = w_ffn_in.astype(BF16), w_ffn_out.astype(BF16)
    for l in range(depth):
        qw = jnp.tile(q_norm_w[l][:, None, :], (1, HEADS, 1)).reshape(1, QKV_COLS)
        kw = jnp.tile(k_norm_w[l][:, None, :], (1, HEADS, 1)).reshape(1, QKV_COLS)
        lnw, lnb = sgu_ln_w[l].reshape(1, -1), sgu_ln_b[l].reshape(1, -1)
        n1w, n2w = norm1_w[l].reshape(1, -1), norm2_w[l].reshape(1, -1)
        mod_p = mod[l, :batch].reshape(batch, 1, -1)
        mod_s = mod[l, batch:n_c].reshape(1, bd, -1)
        bs_p = jnp.repeat(b_spatial[l].T, SGU_GROUP_DIM, axis=1)
        ws_s = jnp.repeat(w_spatial[l][:, 0, 0], SGU_GROUP_DIM).reshape(1, SGU_WIDTH)
        bs_s = jnp.repeat(b_spatial[l][:, 0], SGU_GROUP_DIM).reshape(1, SGU_WIDTH)

        outs = _inproj_call(xp, mod_p, n1w, w_in_l, qw, kw,
                            seq=seq, tm=tm_in, deinterleave=True, act_dtype=BF16,
                            layer=l, depth=depth, prev_states=p_states)
        qkv, p_states = outs[:9], outs[9:]
        ya = _attn_call(qkv, band_bias, batch=batch, seq=seq)
        (xp,) = _post_call(xp, mod_p, ya, n1w, w_in_l, lnw, lnb, w_spatial[l], bs_p,
                           wpa, wps, wo, n2w, wfi, wfo, seq=seq, tm=tm, chunked=True, layer=l)

        outs = _inproj_call(xs, mod_s, n1w, w_in_l, qw, kw,
                            seq=bd, tm=bd, deinterleave=False, act_dtype=F32,
                            layer=l, depth=depth, prev_states=s_states)
        q_tiles, s_states = [o.reshape((bd,) + tile) for o in outs[:3]], outs[3:]
        new_tiles = [s.reshape((depth * bd,) + tile) for s in s_states]
        ya = _attn_sample_call(q_tiles, new_tiles, caches, l, slopes8)
        xs, vs = _post_call(xs, mod_s, ya, n1w, w_in_l, lnw, lnb, ws_s, bs_s,
                            wpa, wps, wo, n2w, wfi, wfo, seq=bd, tm=bd, chunked=False, layer=l)
        s_v.append(vs.reshape(bd, 1, SGU_WIDTH))

    p_states = [s.reshape(depth, batch, -1, 2, HEADS, HEAD_DIM) for s in p_states]
    s_states = [s.reshape(depth, bd, 1, 2, HEADS, HEAD_DIM) for s in s_states]
    return (xp.reshape(batch, seq, d_model), xs.reshape(bd, 1, d_model),
            p_states[0], p_states[1], p_states[2], s_states[0], s_states[1], s_states[2],
            jnp.stack(s_v))
```

```python
import functools
import math

import jax
import jax.numpy as jnp
import numpy as np
from jax.experimental import pallas as pl
from jax.experimental.pallas import tpu as pltpu

HEAD_DIM = 128
N_GROUPS = 3
HEADS = 4
GROUP_COLS = HEADS * HEAD_DIM
QKV_COLS = N_GROUPS * GROUP_COLS
WINDOWS = (128, 512, 2048)
DILATIONS = (1, 4, 16)
N_STEPS = 128
CHUNK = 128
SGU_WIDTH = 512
SGU_GROUPS = 8
SGU_GROUP_DIM = SGU_WIDTH // SGU_GROUPS
N_ADA = 6
EPS = 1e-6
NEG_INF = -1e30
QK_SCALE = HEAD_DIM ** -0.5

LANES = 128
RELAYOUT_STRIDE = 4
VMEM_LIMIT_BYTES = 56 * 1024 * 1024

BF16 = jnp.bfloat16
F32 = jnp.float32


def _params(n_axes, flags=None):
    return pltpu.CompilerParams(dimension_semantics=("arbitrary",) * n_axes,
                                vmem_limit_bytes=VMEM_LIMIT_BYTES, flags=flags)


def _resident(shape):
    nd = len(shape)
    return pl.BlockSpec(shape, lambda *_: (0,) * nd, pipeline_mode=pl.Buffered(1))


def _resident_layer(shape, layer):
    nd = len(shape)
    return pl.BlockSpec((None,) + tuple(shape[1:]), lambda *_: (layer,) + (0,) * (nd - 1),
                        pipeline_mode=pl.Buffered(1))


def _dot(a, b):
    return jnp.dot(a, b, preferred_element_type=F32)


def _dot_nt(a, b):
    return jax.lax.dot_general(a, b, (((1,), (1,)), ((), ())), preferred_element_type=F32)


def _rms(x):
    return x * jax.lax.rsqrt(jnp.mean(x * x, axis=-1, keepdims=True) + EPS)


def _gelu(x):
    return 0.5 * x * (1.0 + jax.lax.erf(x * (2.0 ** -0.5)))


def _sigmoid(x):
    return 1.0 / (1.0 + jnp.exp(-x))


def _ada_kernel(c_ref, w_ref, b_ref, o_ref):
    c = c_ref[...]
    a = c * _sigmoid(c)
    w = w_ref[...]
    a_hi = a.astype(BF16)
    a_lo = (a - a_hi.astype(F32)).astype(BF16)
    w_hi = w.astype(BF16)
    w_lo = (w - w_hi.astype(F32)).astype(BF16)
    rows = a.shape[0]
    both = _dot(jnp.concatenate([a_hi, a_lo], axis=0), w_hi)
    o_ref[...] = both[:rows] + both[rows:] + _dot(a_hi, w_lo) + b_ref[...]


def _ada_call(c_all, w_ada, b_ada):
    depth, d, n = w_ada.shape
    rows = c_all.shape[0]
    tn = 1024
    return pl.pallas_call(
        _ada_kernel,
        grid=(depth, n // tn),
        in_specs=[pl.BlockSpec((rows, d), lambda l, j: (0, 0)),
                  pl.BlockSpec((None, d, tn), lambda l, j: (l, 0, j)),
                  pl.BlockSpec((None, 1, tn), lambda l, j: (l, 0, j))],
        out_specs=pl.BlockSpec((None, rows, tn), lambda l, j: (l, 0, j)),
        out_shape=jax.ShapeDtypeStruct((depth, rows, n), F32),
        compiler_params=_params(2),
        name="adaln_mod",
    )(c_all, w_ada, b_ada.reshape(depth, 1, n))


def _modulated_norm(x, w, mod, which, d_model):
    shift = mod[:, 3 * which * d_model:(3 * which + 1) * d_model]
    scale = mod[:, (3 * which + 1) * d_model:(3 * which + 2) * d_model]
    return _rms(x) * w * (1.0 + scale) + shift


def _inproj_kernel(x_ref, mod_ref, n1w_ref, w_ref, qw_ref, kw_ref,
                   *refs, tm, d_model, dils, state_rows, tiles_per_seq, deinterleave, n_aliased):
    refs = refs[n_aliased:]
    if deinterleave:
        q_refs, k_refs, v_refs, st_refs = refs[0:9:3], refs[1:9:3], refs[2:9:3], refs[9:12]
        h_scr, slab, slab2, kv_stash = refs[12:]
    else:
        q_refs, st_refs = refs[0:3], refs[3:6]
        h_scr, slab, slab2, kv_stash = refs[6:]
    ti = pl.program_id(0) % tiles_per_seq
    tile_rows = 2 * HEADS

    h = _modulated_norm(x_ref[...], n1w_ref[...], mod_ref[0], 0, d_model)
    h_scr[...] = h.astype(BF16)

    def proj(c0, width):
        return _dot(h_scr[...], w_ref[:, c0:c0 + width])

    def put_tiles(ref, first_row, val):
        for hh in range(HEADS):
            ref[pl.ds(first_row + hh, tm, stride=tile_rows), :] = val[:, hh * HEAD_DIM:(hh + 1) * HEAD_DIM]

    def put(ref, g, val, slot, slabs=None):
        d = dils[g]
        if d == 1:
            ref[...] = val.astype(ref.dtype)
            return
        if slabs is None:
            slabs = [slab.at[g - 1, hh] for hh in range(HEADS)]
            for hh in range(HEADS):
                slabs[hh][...] = val[:, hh * HEAD_DIM:(hh + 1) * HEAD_DIM]
        if d <= RELAYOUT_STRIDE:
            for r in range(d):
                for hh in range(HEADS):
                    ref[0, r, :, hh * HEAD_DIM:(hh + 1) * HEAD_DIM] = (
                        slabs[hh][pl.ds(r, tm // d, stride=d), :].astype(ref.dtype))
            return
        s1, s2 = RELAYOUT_STRIDE, d // RELAYOUT_STRIDE
        part = tm // s1
        for hh in range(HEADS):
            for r1 in range(s1):
                slab2[slot, hh,r1 * part:(r1 + 1) * part, :] = slabs[hh][pl.ds(r1, part, stride=s1), :]
            for r1 in range(s1):
                for r2 in range(s2):
                    ref[0, r1 + s1 * r2, :, hh * HEAD_DIM:(hh + 1) * HEAD_DIM] = (
                        slab2[slot, hh,pl.ds(r1 * part + r2, tm // d, stride=s2), :].astype(ref.dtype))

    def head_rms(z, w):
        parts = [_rms(z[:, hh * HEAD_DIM:(hh + 1) * HEAD_DIM]) for hh in range(HEADS)]
        return jnp.concatenate(parts, axis=-1) * w

    def put_state(g, kn, v):
        if state_rows[g] is None:
            put_tiles(st_refs[g], 0, kn)
            put_tiles(st_refs[g], HEADS, v)
        else:
            for hh in range(HEADS):
                cols = slice(hh * HEAD_DIM, (hh + 1) * HEAD_DIM)
                kv_stash[g, hh] = kn[:, cols]
                kv_stash[g, HEADS + hh] = v[:, cols]

    def flush_state(g):
        st, rows = st_refs[g], min(state_rows[g], tm)
        first = tiles_per_seq - max(state_rows[g] // tm, 1)

        @pl.when(ti >= first)
        def _():
            for j in range(2 * HEADS):
                st[pl.ds(j, rows, stride=2 * HEADS), :] = kv_stash[g, j, tm - rows:, :]

    for g in reversed(range(N_GROUPS)):
        c0 = g * GROUP_COLS
        q = head_rms(proj(c0, GROUP_COLS), qw_ref[:, c0:c0 + GROUP_COLS] * QK_SCALE)
        kn = head_rms(proj(QKV_COLS + c0, GROUP_COLS), kw_ref[:, c0:c0 + GROUP_COLS])
        v = proj(2 * QKV_COLS + c0, GROUP_COLS)
        put_state(g, kn, v)
        if not deinterleave:
            put_tiles(q_refs[g], 0, q)
            put_tiles(q_refs[g], HEADS, jnp.zeros_like(q))
            continue
        put(q_refs[g], g, q, 0)
        put(k_refs[g], g, kn, 1, [kv_stash.at[g, hh] for hh in range(HEADS)])
        put(v_refs[g], g, v, 2, [kv_stash.at[g, HEADS + hh] for hh in range(HEADS)])

    for g in range(N_GROUPS):
        if state_rows[g] is not None:
            flush_state(g)


def _inproj_call(x, mod, n1w, w_in, qw, kw, *, seq, tm, deinterleave, act_dtype,
                 layer=0, depth=1, prev_states=None):
    t, d_model = x.shape
    batch = t // seq
    tiles_per_seq = seq // tm

    def tok(cols, dtype):
        return (jax.ShapeDtypeStruct((t, cols), dtype), pl.BlockSpec((tm, cols), lambda i: (i, 0)))

    def grp(g, dtype):
        d = DILATIONS[g]
        if d == 1:
            return tok(GROUP_COLS, dtype)
        assert tm % d == 0 and (tm // d) % 16 == 0
        return (jax.ShapeDtypeStruct((batch, d, seq // d, GROUP_COLS), dtype),
                pl.BlockSpec((1, d, tm // d, GROUP_COLS),
                             lambda i: (i // tiles_per_seq, 0, i % tiles_per_seq, 0)))

    tile_rows = 2 * HEADS
    outs = []
    for g in range(N_GROUPS):
        if deinterleave:
            outs += [grp(g, act_dtype)] * 3
        else:
            assert t == tm
            outs.append((jax.ShapeDtypeStruct((t * tile_rows, LANES), F32),
                         pl.BlockSpec((t * tile_rows, LANES), lambda i: (0, 0))))

    state_rows = []
    for g in range(N_GROUPS):
        if not deinterleave:
            state_rows.append(None)
            outs.append((jax.ShapeDtypeStruct((depth * t * tile_rows, LANES), F32),
                         pl.BlockSpec((t * tile_rows, LANES), lambda i: (layer, 0))))
            continue
        rows = min(WINDOWS[g], seq)
        state_rows.append(rows)
        blk_rows = min(rows, tm)
        assert rows % blk_rows == 0
        nblk = rows // blk_rows
        shape = jax.ShapeDtypeStruct((depth * batch * rows * 2 * HEADS, LANES), F32)
        spec = pl.BlockSpec(
            (blk_rows * 2 * HEADS, LANES),
            lambda i, nblk=nblk: ((layer * batch + i // tiles_per_seq) * nblk
                                  + jnp.maximum(i % tiles_per_seq - (tiles_per_seq - nblk), 0), 0))
        outs.append((shape, spec))

    aliased = list(prev_states) if prev_states is not None else []
    n_in = 6
    w_qkv = pl.BlockSpec((None, d_model, 3 * QKV_COLS), lambda i: (layer, 0, 0),
                         pipeline_mode=pl.Buffered(1))
    kern = functools.partial(_inproj_kernel, tm=tm, d_model=d_model, dils=DILATIONS,
                             state_rows=tuple(state_rows), tiles_per_seq=tiles_per_seq,
                             deinterleave=deinterleave, n_aliased=len(aliased))
    return pl.pallas_call(
        kern,
        grid=(t // tm,),
        in_specs=[pl.BlockSpec((tm, d_model), lambda i: (i, 0)),
                  pl.BlockSpec((1,) + mod.shape[1:], lambda i: (i // tiles_per_seq, 0, 0)),
                  _resident((1, d_model)),
                  w_qkv,
                  _resident((1, QKV_COLS)), _resident((1, QKV_COLS))]
                 + [pl.BlockSpec(memory_space=pl.ANY)] * len(aliased),
        out_specs=[o[1] for o in outs],
        out_shape=[o[0] for o in outs],
        input_output_aliases={n_in + g: len(outs) - N_GROUPS + g for g in range(len(aliased))},
        scratch_shapes=[pltpu.VMEM((tm, d_model), BF16),
                        pltpu.VMEM((N_GROUPS - 1, HEADS, tm, LANES), F32),
                        pltpu.VMEM((3, HEADS, tm, LANES), F32),
                        pltpu.VMEM((N_GROUPS, 2 * HEADS, tm, LANES), F32)],
        compiler_params=_params(1),
        name="inproj_deint" if deinterleave else "inproj_tok",
    )(x, mod, n1w, w_in, qw, kw, *aliased)


def _alibi_slopes():
    h = np.arange(1, N_GROUPS * HEADS + 1, dtype=np.float32)
    s = np.power(np.float32(2.0), -8.0 * h / (N_GROUPS * HEADS)).astype(np.float32)
    return s.reshape(N_GROUPS, HEADS)


def _band_bias():
    qi = np.arange(N_STEPS)[:, None]
    ki = np.arange(2 * N_STEPS)[None, :]
    dist = N_STEPS + qi - ki
    valid = (dist >= 0) & (dist <= N_STEPS)
    slopes = _alibi_slopes()
    out = np.empty((N_GROUPS, HEADS, 2, N_STEPS, 2 * N_STEPS), np.float32)
    for g in range(N_GROUPS):
        for hh in range(HEADS):
            b = np.where(valid, -slopes[g, hh] * (dist * DILATIONS[g]).astype(np.float32),
                         np.float32(NEG_INF)).astype(np.float32)
            out[g, hh, 1] = b
            first = b.copy()
            first[:, :N_STEPS] = NEG_INF
            out[g, hh, 0] = first
    return out


def _attn_block(q, k2, v2, bias):
    s = _dot_nt(q, k2) + bias
    m = jnp.max(s, axis=-1, keepdims=True)
    p = jnp.exp(s - m)
    l = jnp.sum(p, axis=-1, keepdims=True)
    return _dot(p.astype(v2.dtype), v2), m, l


ATTN_UNROLL = 32


def _attn_kernel(q0, k0, v0, q1, k1, v1, q2, k2, v2, bias_ref, o_ref, o_scr, m_scr, l_scr, stage_scr,
                 *, seq):
    blk = N_STEPS
    s1 = RELAYOUT_STRIDE

    def run_group(g, qr, kr, vr):
        d = DILATIONS[g]
        nblk = seq // d // blk

        def one_block(r, i):
            own = pl.multiple_of(i * blk, blk)
            prev = pl.multiple_of(jnp.maximum(i - 1, 0) * blk, blk)
            if d == 1:
                ld = lambda ref, start: ref[pl.ds(start, blk), :]
            else:
                ld = lambda ref, start: ref[0, r, pl.ds(start, blk), :]
            k2_ = jnp.concatenate([ld(kr, prev), ld(kr, own)], axis=0)
            v2_ = jnp.concatenate([ld(vr, prev), ld(vr, own)], axis=0)
            o, m, l = _attn_block(ld(qr, own), k2_, v2_, bias_ref[g, 0, jnp.minimum(i, 1)])
            m = jnp.broadcast_to(m, (blk, LANES))
            l = jnp.broadcast_to(l, (blk, LANES))
            if 1 < d <= s1:
                rows = pl.ds(i * (blk * d) + r, blk, stride=d)
                o_scr[g - 1, rows, :] = o
                m_scr[g - 1, rows, :] = m
                l_scr[g - 1, rows, :] = l
                return
            if d > s1:
                s2 = d // s1
                rows = pl.ds((r % s1) * (seq // s1) + i * (blk * s2) + r // s1, blk, stride=s2)
                stage_scr[0, rows, :] = o
                stage_scr[1, rows, :] = m
                stage_scr[2, rows, :] = l
                return
            rows = pl.ds(own, blk)
            ms = [m, m_scr[0, rows, :], m_scr[1, rows, :]]
            mx = jnp.maximum(jnp.maximum(ms[0], ms[1]), ms[2])
            ws = [jnp.exp(x - mx) for x in ms]
            num = ws[0] * o + ws[1] * o_scr[0, rows, :] + ws[2] * o_scr[1, rows, :]
            den = ws[0] * l + ws[1] * l_scr[0, rows, :] + ws[2] * l_scr[1, rows, :]
            o_ref[rows, :] = (num / den).astype(o_ref.dtype)

        ur = min(ATTN_UNROLL, d)
        ui = ATTN_UNROLL // ur
        assert d % ur == 0 and nblk % ui == 0

        def body(it, carry):
            i0 = (it // (d // ur)) * ui
            r0 = (it % (d // ur)) * ur
            for a in range(ui):
                for b in range(ur):
                    one_block(r0 + b, i0 + a)
            return carry

        jax.lax.fori_loop(0, (d // ur) * (nblk // ui), body, 0)

    run_group(1, q1, k1, v1)
    run_group(2, q2, k2, v2)
    g = N_GROUPS - 1
    assert DILATIONS[g] > s1 and all(d <= s1 for d in DILATIONS[:g])
    per_stream = seq // s1
    for a, dst in enumerate((o_scr, m_scr, l_scr)):
        for r1 in range(s1):
            for c in range(per_stream // blk):
                dst[g - 1, pl.ds(s1 * c * blk + r1, blk, stride=s1), :] = (
                    stage_scr[a, r1 * per_stream + c * blk:r1 * per_stream + (c + 1) * blk, :])
    run_group(0, q0, k0, v0)


def _attn_call(qkv, bias, *, batch, seq):
    in_specs = []
    for g in range(N_GROUPS):
        d = DILATIONS[g]
        if d == 1:
            spec = pl.BlockSpec((seq, HEAD_DIM), lambda b, h: (b, h))
        else:
            spec = pl.BlockSpec((1, d, seq // d, HEAD_DIM), lambda b, h: (b, 0, 0, h))
        in_specs += [spec] * 3
    in_specs.append(pl.BlockSpec((N_GROUPS, 1, 2, N_STEPS, 2 * N_STEPS), lambda b, h: (0, h, 0, 0, 0)))
    return pl.pallas_call(
        functools.partial(_attn_kernel, seq=seq),
        grid=(batch, HEADS),
        in_specs=in_specs,
        out_specs=pl.BlockSpec((seq, HEAD_DIM), lambda b, h: (b, h)),
        out_shape=jax.ShapeDtypeStruct((batch * seq, GROUP_COLS), BF16),
        scratch_shapes=[pltpu.VMEM((N_GROUPS - 1, seq, LANES), F32)] * 3
                       + [pltpu.VMEM((3, seq, LANES), F32)],
        compiler_params=_params(2),
        name="attn_prompt",
    )(*qkv, bias)


def _attn_sample_kernel(q0, q1, q2, n0, n1, n2, c0, c1, c2, slope_ref, o_ref, *, bb):
    q_refs, n_refs, c_refs = (q0, q1, q2), (n0, n1, n2), (c0, c1, c2)
    back = (N_STEPS - jax.lax.broadcasted_iota(jnp.int32, (N_STEPS, 1, 1), 0)).astype(F32)

    def to_v_rows(stat):
        return pltpu.roll(jnp.broadcast_to(stat, (2 * HEADS, HEAD_DIM)), HEADS, axis=0)

    for b in range(bb):
        outs, lses = [], []
        for g in range(N_GROUPS):
            q8, kv8, x = q_refs[g][b], n_refs[g][b], c_refs[g][b]
            bias = (back * (-float(DILATIONS[g]))) * slope_ref[g][:, 0:1]
            s = jnp.sum(x * q8, axis=-1, keepdims=True) + bias
            s_self = jnp.sum(kv8 * q8, axis=-1, keepdims=True)
            m = jnp.maximum(jnp.max(s, axis=0), s_self)
            p = jnp.exp(s - m)
            p_self = jnp.exp(s_self - m)
            l = jnp.sum(p, axis=0) + p_self
            p_v = pltpu.roll(jnp.broadcast_to(p, x.shape), HEADS, axis=1)
            acc = jnp.sum(p_v * x, axis=0) + to_v_rows(p_self) * kv8
            outs.append(acc / to_v_rows(l))
            lses.append(to_v_rows(m + jnp.log(l)))
        mx = jnp.maximum(jnp.maximum(lses[0], lses[1]), lses[2])
        ws = [jnp.exp(v - mx) for v in lses]
        num = ws[0] * outs[0] + ws[1] * outs[1] + ws[2] * outs[2]
        o8 = num / (ws[0] + ws[1] + ws[2])
        for hh in range(HEADS):
            o_ref[b:b + 1, hh * HEAD_DIM:(hh + 1) * HEAD_DIM] = o8[HEADS + hh:HEADS + hh + 1, :]


def _attn_sample_call(q_tiles, new_tiles, caches, layer, slopes8):
    bd = q_tiles[0].shape[0]
    bb = 8
    nb = bd // bb
    tile = (2 * HEADS, HEAD_DIM)
    in_specs = [pl.BlockSpec((bb,) + tile, lambda i: (i, 0, 0))] * 3
    in_specs += [pl.BlockSpec((bb,) + tile, lambda i: (layer * nb + i, 0, 0))] * 3
    in_specs += [pl.BlockSpec((None, bb, N_STEPS) + tile, lambda i: (layer, i, 0, 0, 0))] * 3
    in_specs.append(pl.BlockSpec((N_GROUPS,) + tile, lambda i: (0, 0, 0)))
    return pl.pallas_call(
        functools.partial(_attn_sample_kernel, bb=bb),
        grid=(nb,),
        in_specs=in_specs,
        out_specs=pl.BlockSpec((bb, GROUP_COLS), lambda i: (i, 0)),
        out_shape=jax.ShapeDtypeStruct((bd, GROUP_COLS), F32),
        compiler_params=_params(1),
        name="attn_sample",
    )(*q_tiles, *new_tiles, *caches, slopes8)


def _post_kernel(x_ref, mod_ref, ya_ref, n1w_ref, wg1_ref, wg2_ref, lnw_ref, lnb_ref, ws_ref, bs_ref,
                 wpa_ref, wps_ref, wo_ref, n2w_ref, wfi_ref, wfo_ref, *refs,
                 tm, d_model, d_ff, chunked):
    if chunked:
        o_ref, mix_scr = refs
    else:
        o_ref, vs_out_ref, mix_scr = refs
    mod = mod_ref[0]
    g1 = mod[:, 2 * d_model:3 * d_model]
    g2 = mod[:, 5 * d_model:6 * d_model]
    x = x_ref[...]

    h = _modulated_norm(x, n1w_ref[...], mod, 0, d_model).astype(BF16)
    w = SGU_WIDTH
    u = _gelu(_dot(h, wg1_ref[:, 0:w]))
    vs = _gelu(_dot(h, wg1_ref[:, w:2 * w]))
    mu = jnp.mean(vs, axis=-1, keepdims=True)
    vc = vs - mu
    var = jnp.mean(vc * vc, axis=-1, keepdims=True)
    vs = vc * jax.lax.rsqrt(var + EPS) * lnw_ref[...] + lnb_ref[...]
    half = d_model // 2
    ga = jnp.concatenate([_sigmoid(_dot(h, wg1_ref[:, 2 * w:2 * w + half])),
                          _sigmoid(_dot(h, wg2_ref[:, 0:half]))], axis=-1)
    gb = _sigmoid(_dot(h, wg2_ref[:, half:half + d_model]))

    if chunked:
        row = jax.lax.broadcasted_iota(jnp.int32, (CHUNK, CHUNK), 0)
        col = jax.lax.broadcasted_iota(jnp.int32, (CHUNK, CHUNK), 1)
        tril = col <= row
        lane = jax.lax.broadcasted_iota(jnp.int32, (CHUNK, LANES), 1)
        low = lane < SGU_GROUP_DIM
        wms = [jnp.where(tril, ws_ref[gi], 0.0).astype(BF16) for gi in range(SGU_GROUPS)]
        vs_b = vs.astype(BF16)
        for c in range(tm // CHUNK):
            rows = slice(c * CHUNK, (c + 1) * CHUNK)
            for j in range(SGU_WIDTH // LANES):
                cols = slice(j * LANES, (j + 1) * LANES)
                v = vs_b[rows, cols]
                mix_scr[rows, cols] = jnp.where(low, _dot(wms[2 * j], v), _dot(wms[2 * j + 1], v))
        mixed = mix_scr[...] + jnp.concatenate([bs_ref[...]] * (tm // CHUNK), axis=0)
    else:
        vs_out_ref[...] = vs
        mixed = vs * ws_ref[...] + bs_ref[...]

    ys = (u * mixed).astype(BF16)
    xa = _dot(ya_ref[...].astype(BF16), wpa_ref[...])
    xs = _dot(ys, wps_ref[...])
    merged = (ga * xa + gb * xs).astype(BF16)
    x1 = x + g1 * _dot(merged, wo_ref[...])
    h2 = _modulated_norm(x1, n2w_ref[...], mod, 1, d_model).astype(BF16)
    a = _dot(h2, wfi_ref[:, 0:d_ff])
    b = _dot(h2, wfi_ref[:, d_ff:2 * d_ff])
    act = (a * _sigmoid(a) * b).astype(BF16)
    o_ref[...] = x1 + g2 * _dot(act, wfo_ref[...])


def _post_call(x, mod, ya, n1w, w_in, lnw, lnb, ws, bs, wpa, wps, wo, n2w, wfi, wfo, *,
               seq, tm, chunked, layer):
    t, d_model = x.shape
    d_ff = wfo.shape[1]
    tiles_per_seq = seq // tm
    gblk = QKV_COLS
    assert 2 * SGU_WIDTH + d_model // 2 == gblk and w_in.shape[2] == 5 * gblk

    def w_gate(j):
        return pl.BlockSpec((None, d_model, gblk), lambda i: (layer, 0, 3 + j),
                            pipeline_mode=pl.Buffered(1))

    def tok(cols):
        return pl.BlockSpec((tm, cols), lambda i: (i, 0))

    out_specs, out_shape = [tok(d_model)], [jax.ShapeDtypeStruct((t, d_model), F32)]
    if not chunked:
        out_specs.append(tok(SGU_WIDTH))
        out_shape.append(jax.ShapeDtypeStruct((t, SGU_WIDTH), F32))
    kern = functools.partial(_post_kernel, tm=tm, d_model=d_model, d_ff=d_ff, chunked=chunked)
    return pl.pallas_call(
        kern,
        grid=(t // tm,),
        in_specs=[tok(d_model),
                  pl.BlockSpec((1,) + mod.shape[1:], lambda i: (i // tiles_per_seq, 0, 0)),
                  tok(GROUP_COLS), _resident(n1w.shape), w_gate(0), w_gate(1),
                  _resident(lnw.shape), _resident(lnb.shape),
                  _resident(ws.shape), _resident(bs.shape),
                  _resident_layer(wpa.shape, layer), _resident_layer(wps.shape, layer),
                  _resident_layer(wo.shape, layer), _resident(n2w.shape),
                  _resident_layer(wfi.shape, layer), _resident_layer(wfo.shape, layer)],
        out_specs=out_specs,
        out_shape=out_shape,
        scratch_shapes=[pltpu.VMEM((tm, SGU_WIDTH), F32)],
        compiler_params=_params(1),
        name="post_chunked" if chunked else "post_tok",
    )(x, mod, ya, n1w, w_in, w_in, lnw, lnb, ws, bs, wpa, wps, wo, n2w, wfi, wfo)


def kernel(x_prompt, x_sample, cache_kv_w128, cache_kv_w512, cache_kv_w2048, c_prompt, c_sample,
           w_ada, b_ada, norm1_w, w_in, q_norm_w, k_norm_w, sgu_ln_w, sgu_ln_b, w_spatial, b_spatial,
           w_proj_att, w_proj_sgu, w_out, norm2_w, w_ffn_in, w_ffn_out):
    batch, seq, d_model = x_prompt.shape
    bd, dec_seq, _ = x_sample.shape
    depth = w_in.shape[0]
    assert dec_seq == 1 and seq % (N_STEPS * DILATIONS[-1]) == 0
    caches_in = (cache_kv_w128, cache_kv_w512, cache_kv_w2048)
    for g in range(N_GROUPS):
        assert caches_in[g].shape[2] == WINDOWS[g]
    tm = 512
    tm_in = 512

    n_c = batch + bd
    c_all = jnp.concatenate([c_prompt, c_sample], axis=0)
    c_all = jnp.pad(c_all, ((0, -n_c % 16), (0, 0)))
    mod = _ada_call(c_all, w_ada, b_ada)

    band_bias = jnp.asarray(_band_bias())
    tile = (2 * HEADS, HEAD_DIM)
    slopes8 = np.zeros((N_GROUPS,) + tile, np.float32)
    slopes8[:, :HEADS, :] = _alibi_slopes()[:, :, None]
    slopes8 = jnp.asarray(slopes8)
    caches = [caches_in[g].reshape((depth, bd, N_STEPS, DILATIONS[g] * tile[0], tile[1]))
              for g in range(N_GROUPS)]

    xp = x_prompt.reshape(batch * seq, d_model)
    xs = x_sample.reshape(bd, d_model)
    p_states = s_states = None
    s_v = []
    w_in_l = w_in.astype(BF16)
    wpa, wps, wo = w_proj_att.astype(BF16), w_proj_sgu.astype(BF16), w_out.astype(BF16)
    wfi, wfo = w_ffn_in.astype(BF16), w_ffn_out.astype(BF16)
    for l in range(depth):
        qw = jnp.tile(q_norm_w[l][:, None, :], (1, HEADS, 1)).reshape(1, QKV_COLS)
        kw = jnp.tile(k_norm_w[l][:, None, :], (1, HEADS, 1)).reshape(1, QKV_COLS)
        lnw, lnb = sgu_ln_w[l].reshape(1, -1), sgu_ln_b[l].reshape(1, -1)
        n1w, n2w = norm1_w[l].reshape(1, -1), norm2_w[l].reshape(1, -1)
        mod_p = mod[l, :batch].reshape(batch, 1, -1)
        mod_s = mod[l, batch:n_c].reshape(1, bd, -1)
        bs_p = jnp.repeat(b_spatial[l].T, SGU_GROUP_DIM, axis=1)
        ws_s = jnp.repeat(w_spatial[l][:, 0, 0], SGU_GROUP_DIM).reshape(1, SGU_WIDTH)
        bs_s = jnp.repeat(b_spatial[l][:, 0], SGU_GROUP_DIM).reshape(1, SGU_WIDTH)

        outs = _inproj_call(xp, mod_p, n1w, w_in_l, qw, kw,
                            seq=seq, tm=tm_in, deinterleave=True, act_dtype=BF16,
                            layer=l, depth=depth, prev_states=p_states)
        qkv, p_states = outs[:9], outs[9:]
        ya = _attn_call(qkv, band_bias, batch=batch, seq=seq)
        (xp,) = _post_call(xp, mod_p, ya, n1w, w_in_l, lnw, lnb, w_spatial[l], bs_p,
                           wpa, wps, wo, n2w, wfi, wfo, seq=seq, tm=tm, chunked=True, layer=l)

        outs = _inproj_call(xs, mod_s, n1w, w_in_l, qw, kw,
                            seq=bd, tm=bd, deinterleave=False, act_dtype=F32,
                            layer=l, depth=depth, prev_states=s_states)
        q_tiles, s_states = [o.reshape((bd,) + tile) for o in outs[:3]], outs[3:]
        new_tiles = [s.reshape((depth * bd,) + tile) for s in s_states]
        ya = _attn_sample_call(q_tiles, new_tiles, caches, l, slopes8)
        xs, vs = _post_call(xs, mod_s, ya, n1w, w_in_l, lnw, lnb, ws_s, bs_s,
                            wpa, wps, wo, n2w, wfi, wfo, seq=bd, tm=bd, chunked=False, layer=l)
        s_v.append(vs.reshape(bd, 1, SGU_WIDTH))

    p_states = [s.reshape(depth, batch, -1, 2, HEADS, HEAD_DIM) for s in p_states]
    s_states = [s.reshape(depth, bd, 1, 2, HEADS, HEAD_DIM) for s in s_states]
    return (xp.reshape(batch, seq, d_model), xs.reshape(bd, 1, d_model),
            p_states[0], p_states[1], p_states[2], s_states[0], s_states[1], s_states[2],
            jnp.stack(s_v))
```

```python
import functools
import math

import jax
import jax.numpy as jnp
import numpy as np
from jax.experimental import pallas as pl
from jax.experimental.pallas import tpu as pltpu

HEAD_DIM = 128
N_GROUPS = 3
HEADS = 4
GROUP_COLS = HEADS * HEAD_DIM
QKV_COLS = N_GROUPS * GROUP_COLS
WINDOWS = (128, 512, 2048)
DILATIONS = (1, 4, 16)
N_STEPS = 128
CHUNK = 128
SGU_WIDTH = 512
SGU_GROUPS = 8
SGU_GROUP_DIM = SGU_WIDTH // SGU_GROUPS
N_ADA = 6
EPS = 1e-6
NEG_INF = -1e30
LOG2E = math.log2(math.e)
QK_SCALE = HEAD_DIM ** -0.5 * LOG2E

LANES = 128
RELAYOUT_STRIDE = 4
VMEM_LIMIT_BYTES = 56 * 1024 * 1024

BF16 = jnp.bfloat16
F32 = jnp.float32


def _params(n_axes, flags=None):
    return pltpu.CompilerParams(dimension_semantics=("arbitrary",) * n_axes,
                                vmem_limit_bytes=VMEM_LIMIT_BYTES, flags=flags)


def _resident(shape):
    nd = len(shape)
    return pl.BlockSpec(shape, lambda *_: (0,) * nd, pipeline_mode=pl.Buffered(1))


def _resident_layer(shape, layer):
    nd = len(shape)
    return pl.BlockSpec((None,) + tuple(shape[1:]), lambda *_: (layer,) + (0,) * (nd - 1),
                        pipeline_mode=pl.Buffered(1))


def _dot(a, b):
    return jnp.dot(a, b, preferred_element_type=F32)


def _dot_nt(a, b):
    return jax.lax.dot_general(a, b, (((1,), (1,)), ((), ())), preferred_element_type=F32)


def _rms(x):
    return x * jax.lax.rsqrt(jnp.mean(x * x, axis=-1, keepdims=True) + EPS)


def _gelu(x):
    return 0.5 * x * (1.0 + jax.lax.erf(x * (2.0 ** -0.5)))


def _sigmoid(x):
    return 1.0 / (1.0 + jnp.exp(-x))


def _ada_kernel(c_ref, w_ref, b_ref, o_ref):
    c = c_ref[...]
    a = c * _sigmoid(c)
    w = w_ref[...]
    a_hi = a.astype(BF16)
    a_lo = (a - a_hi.astype(F32)).astype(BF16)
    w_hi = w.astype(BF16)
    w_lo = (w - w_hi.astype(F32)).astype(BF16)
    rows = a.shape[0]
    both = _dot(jnp.concatenate([a_hi, a_lo], axis=0), w_hi)
    o_ref[...] = both[:rows] + both[rows:] + _dot(a_hi, w_lo) + b_ref[...]


def _ada_call(c_all, w_ada, b_ada):
    depth, d, n = w_ada.shape
    rows = c_all.shape[0]
    tn = 1024
    return pl.pallas_call(
        _ada_kernel,
        grid=(depth, n // tn),
        in_specs=[pl.BlockSpec((rows, d), lambda l, j: (0, 0)),
                  pl.BlockSpec((None, d, tn), lambda l, j: (l, 0, j)),
                  pl.BlockSpec((None, 1, tn), lambda l, j: (l, 0, j))],
        out_specs=pl.BlockSpec((None, rows, tn), lambda l, j: (l, 0, j)),
        out_shape=jax.ShapeDtypeStruct((depth, rows, n), F32),
        compiler_params=_params(2),
        name="adaln_mod",
    )(c_all, w_ada, b_ada.reshape(depth, 1, n))


def _modulated_norm(x, w, mod, which, d_model):
    shift = mod[:, 3 * which * d_model:(3 * which + 1) * d_model]
    scale = mod[:, (3 * which + 1) * d_model:(3 * which + 2) * d_model]
    return _rms(x) * w * (1.0 + scale) + shift


def _inproj_kernel(x_ref, mod_ref, n1w_ref, w_ref, qw_ref, kw_ref,
                   *refs, tm, d_model, dils, state_rows, tiles_per_seq, deinterleave, n_aliased):
    refs = refs[n_aliased:]
    if deinterleave:
        q_refs, k_refs, v_refs, st_refs = refs[0:9:3], refs[1:9:3], refs[2:9:3], refs[9:12]
        h_scr, slab, slab2, kv_stash = refs[12:]
    else:
        q_refs, st_refs = refs[0:3], refs[3:6]
        h_scr, slab, slab2, kv_stash = refs[6:]
    ti = pl.program_id(0) % tiles_per_seq
    tile_rows = 2 * HEADS

    h = _modulated_norm(x_ref[...], n1w_ref[...], mod_ref[0], 0, d_model)
    h_scr[...] = h.astype(BF16)

    def proj(c0, width):
        return _dot(h_scr[...], w_ref[:, c0:c0 + width])

    def put_tiles(ref, first_row, val):
        for hh in range(HEADS):
            ref[pl.ds(first_row + hh, tm, stride=tile_rows), :] = val[:, hh * HEAD_DIM:(hh + 1) * HEAD_DIM]

    def put(ref, g, val, slot, slabs=None):
        d = dils[g]
        if d == 1:
            ref[...] = val.astype(ref.dtype)
            return
        if slabs is None:
            slabs = [slab.at[g - 1, hh] for hh in range(HEADS)]
            for hh in range(HEADS):
                slabs[hh][...] = val[:, hh * HEAD_DIM:(hh + 1) * HEAD_DIM]
        if d <= RELAYOUT_STRIDE:
            for r in range(d):
                for hh in range(HEADS):
                    ref[0, r, :, hh * HEAD_DIM:(hh + 1) * HEAD_DIM] = (
                        slabs[hh][pl.ds(r, tm // d, stride=d), :].astype(ref.dtype))
            return
        s1, s2 = RELAYOUT_STRIDE, d // RELAYOUT_STRIDE
        part = tm // s1
        for hh in range(HEADS):
            for r1 in range(s1):
                slab2[slot, hh,r1 * part:(r1 + 1) * part, :] = slabs[hh][pl.ds(r1, part, stride=s1), :]
            for r1 in range(s1):
                for r2 in range(s2):
                    ref[0, r1 + s1 * r2, :, hh * HEAD_DIM:(hh + 1) * HEAD_DIM] = (
                        slab2[slot, hh,pl.ds(r1 * part + r2, tm // d, stride=s2), :].astype(ref.dtype))

    def head_rms(z, w):
        parts = [_rms(z[:, hh * HEAD_DIM:(hh + 1) * HEAD_DIM]) for hh in range(HEADS)]
        return jnp.concatenate(parts, axis=-1) * w

    def put_state(g, kn, v):
        if state_rows[g] is None:
            put_tiles(st_refs[g], 0, kn)
            put_tiles(st_refs[g], HEADS, v)
        else:
            for hh in range(HEADS):
                cols = slice(hh * HEAD_DIM, (hh + 1) * HEAD_DIM)
                kv_stash[g, hh] = kn[:, cols]
                kv_stash[g, HEADS + hh] = v[:, cols]

    def flush_state(g):
        st, rows = st_refs[g], min(state_rows[g], tm)
        first = tiles_per_seq - max(state_rows[g] // tm, 1)

        @pl.when(ti >= first)
        def _():
            for j in range(2 * HEADS):
                st[pl.ds(j, rows, stride=2 * HEADS), :] = kv_stash[g, j, tm - rows:, :]

    for g in reversed(range(N_GROUPS)):
        c0 = g * GROUP_COLS
        q = head_rms(proj(c0, GROUP_COLS), qw_ref[:, c0:c0 + GROUP_COLS] * QK_SCALE)
        kn = head_rms(proj(QKV_COLS + c0, GROUP_COLS), kw_ref[:, c0:c0 + GROUP_COLS])
        v = proj(2 * QKV_COLS + c0, GROUP_COLS)
        put_state(g, kn, v)
        if not deinterleave:
            put_tiles(q_refs[g], 0, q)
            put_tiles(q_refs[g], HEADS, jnp.zeros_like(q))
            continue
        put(q_refs[g], g, q, 0)
        put(k_refs[g], g, kn, 1, [kv_stash.at[g, hh] for hh in range(HEADS)])
        put(v_refs[g], g, v, 2, [kv_stash.at[g, HEADS + hh] for hh in range(HEADS)])

    for g in range(N_GROUPS):
        if state_rows[g] is not None:
            flush_state(g)


def _inproj_call(x, mod, n1w, w_in, qw, kw, *, seq, tm, deinterleave, act_dtype,
                 layer=0, depth=1, prev_states=None):
    t, d_model = x.shape
    batch = t // seq
    tiles_per_seq = seq // tm

    def tok(cols, dtype):
        return (jax.ShapeDtypeStruct((t, cols), dtype), pl.BlockSpec((tm, cols), lambda i: (i, 0)))

    def grp(g, dtype):
        d = DILATIONS[g]
        if d == 1:
            return tok(GROUP_COLS, dtype)
        assert tm % d == 0 and (tm // d) % 16 == 0
        return (jax.ShapeDtypeStruct((batch, d, seq // d, GROUP_COLS), dtype),
                pl.BlockSpec((1, d, tm // d, GROUP_COLS),
                             lambda i: (i // tiles_per_seq, 0, i % tiles_per_seq, 0)))

    tile_rows = 2 * HEADS
    outs = []
    for g in range(N_GROUPS):
        if deinterleave:
            outs += [grp(g, act_dtype)] * 3
        else:
            assert t == tm
            outs.append((jax.ShapeDtypeStruct((t * tile_rows, LANES), F32),
                         pl.BlockSpec((t * tile_rows, LANES), lambda i: (0, 0))))

    state_rows = []
    for g in range(N_GROUPS):
        if not deinterleave:
            state_rows.append(None)
            outs.append((jax.ShapeDtypeStruct((depth * t * tile_rows, LANES), F32),
                         pl.BlockSpec((t * tile_rows, LANES), lambda i: (layer, 0))))
            continue
        rows = min(WINDOWS[g], seq)
        state_rows.append(rows)
        blk_rows = min(rows, tm)
        assert rows % blk_rows == 0
        nblk = rows // blk_rows
        shape = jax.ShapeDtypeStruct((depth * batch * rows * 2 * HEADS, LANES), F32)
        spec = pl.BlockSpec(
            (blk_rows * 2 * HEADS, LANES),
            lambda i, nblk=nblk: ((layer * batch + i // tiles_per_seq) * nblk
                                  + jnp.maximum(i % tiles_per_seq - (tiles_per_seq - nblk), 0), 0))
        outs.append((shape, spec))

    aliased = list(prev_states) if prev_states is not None else []
    n_in = 6
    w_qkv = pl.BlockSpec((None, d_model, 3 * QKV_COLS), lambda i: (layer, 0, 0),
                         pipeline_mode=pl.Buffered(1))
    kern = functools.partial(_inproj_kernel, tm=tm, d_model=d_model, dils=DILATIONS,
                             state_rows=tuple(state_rows), tiles_per_seq=tiles_per_seq,
                             deinterleave=deinterleave, n_aliased=len(aliased))
    return pl.pallas_call(
        kern,
        grid=(t // tm,),
        in_specs=[pl.BlockSpec((tm, d_model), lambda i: (i, 0)),
                  pl.BlockSpec((1,) + mod.shape[1:], lambda i: (i // tiles_per_seq, 0, 0)),
                  _resident((1, d_model)),
                  w_qkv,
                  _resident((1, QKV_COLS)), _resident((1, QKV_COLS))]
                 + [pl.BlockSpec(memory_space=pl.ANY)] * len(aliased),
        out_specs=[o[1] for o in outs],
        out_shape=[o[0] for o in outs],
        input_output_aliases={n_in + g: len(outs) - N_GROUPS + g for g in range(len(aliased))},
        scratch_shapes=[pltpu.VMEM((tm, d_model), BF16),
                        pltpu.VMEM((N_GROUPS - 1, HEADS, tm, LANES), F32),
                        pltpu.VMEM((3, HEADS, tm, LANES), F32),
                        pltpu.VMEM((N_GROUPS, 2 * HEADS, tm, LANES), F32)],
        compiler_params=_params(1),
        name="inproj_deint" if deinterleave else "inproj_tok",
    )(x, mod, n1w, w_in, qw, kw, *aliased)


def _alibi_slopes():
    h = np.arange(1, N_GROUPS * HEADS + 1, dtype=np.float32)
    s = np.power(np.float32(2.0), -8.0 * h / (N_GROUPS * HEADS)).astype(np.float32)
    return (s.astype(np.float64) * LOG2E).astype(np.float32).reshape(N_GROUPS, HEADS)


def _band_bias():
    qi = np.arange(N_STEPS)[:, None]
    ki = np.arange(2 * N_STEPS)[None, :]
    dist = N_STEPS + qi - ki
    valid = (dist >= 0) & (dist <= N_STEPS)
    slopes = _alibi_slopes()
    out = np.empty((N_GROUPS, HEADS, 2, N_STEPS, 2 * N_STEPS), np.float32)
    for g in range(N_GROUPS):
        for hh in range(HEADS):
            b = np.where(valid, -slopes[g, hh] * (dist * DILATIONS[g]).astype(np.float32),
                         np.float32(NEG_INF)).astype(np.float32)
            out[g, hh, 1] = b
            first = b.copy()
            first[:, :N_STEPS] = NEG_INF
            out[g, hh, 0] = first
    return out


def _attn_block(q, k2, v2, bias):
    s = _dot_nt(q, k2) + bias
    m = jnp.max(s, axis=-1, keepdims=True)
    p = jnp.exp2(s - m)
    l = jnp.sum(p, axis=-1, keepdims=True)
    return _dot(p.astype(v2.dtype), v2), m, l


ATTN_UNROLL = 32


def _attn_kernel(q0, k0, v0, q1, k1, v1, q2, k2, v2, bias_ref, o_ref, o_scr, m_scr, l_scr, stage_scr,
                 *, seq):
    blk = N_STEPS
    s1 = RELAYOUT_STRIDE

    def run_group(g, qr, kr, vr):
        d = DILATIONS[g]
        nblk = seq // d // blk

        def one_block(r, i):
            own = pl.multiple_of(i * blk, blk)
            prev = pl.multiple_of(jnp.maximum(i - 1, 0) * blk, blk)
            if d == 1:
                ld = lambda ref, start: ref[pl.ds(start, blk), :]
            else:
                ld = lambda ref, start: ref[0, r, pl.ds(start, blk), :]
            k2_ = jnp.concatenate([ld(kr, prev), ld(kr, own)], axis=0)
            v2_ = jnp.concatenate([ld(vr, prev), ld(vr, own)], axis=0)
            o, m, l = _attn_block(ld(qr, own), k2_, v2_, bias_ref[g, 0, jnp.minimum(i, 1)])
            m = jnp.broadcast_to(m, (blk, LANES))
            l = jnp.broadcast_to(l, (blk, LANES))
            if 1 < d <= s1:
                rows = pl.ds(i * (blk * d) + r, blk, stride=d)
                o_scr[g - 1, rows, :] = o
                m_scr[g - 1, rows, :] = m
                l_scr[g - 1, rows, :] = l
                return
            if d > s1:
                s2 = d // s1
                rows = pl.ds((r % s1) * (seq // s1) + i * (blk * s2) + r // s1, blk, stride=s2)
                stage_scr[0, rows, :] = o
                stage_scr[1, rows, :] = m
                stage_scr[2, rows, :] = l
                return
            rows = pl.ds(own, blk)
            ms = [m, m_scr[0, rows, :], m_scr[1, rows, :]]
            mx = jnp.maximum(jnp.maximum(ms[0], ms[1]), ms[2])
            ws = [jnp.exp2(x - mx) for x in ms]
            num = ws[0] * o + ws[1] * o_scr[0, rows, :] + ws[2] * o_scr[1, rows, :]
            den = ws[0] * l + ws[1] * l_scr[0, rows, :] + ws[2] * l_scr[1, rows, :]
            o_ref[rows, :] = (num / den).astype(o_ref.dtype)

        ur = min(ATTN_UNROLL, d)
        ui = ATTN_UNROLL // ur
        assert d % ur == 0 and nblk % ui == 0

        def body(it, carry):
            i0 = (it // (d // ur)) * ui
            r0 = (it % (d // ur)) * ur
            for a in range(ui):
                for b in range(ur):
                    one_block(r0 + b, i0 + a)
            return carry

        jax.lax.fori_loop(0, (d // ur) * (nblk // ui), body, 0)

    run_group(1, q1, k1, v1)
    run_group(2, q2, k2, v2)
    g = N_GROUPS - 1
    assert DILATIONS[g] > s1 and all(d <= s1 for d in DILATIONS[:g])
    per_stream = seq // s1
    for a, dst in enumerate((o_scr, m_scr, l_scr)):
        for r1 in range(s1):
            for c in range(per_stream // blk):
                dst[g - 1, pl.ds(s1 * c * blk + r1, blk, stride=s1), :] = (
                    stage_scr[a, r1 * per_stream + c * blk:r1 * per_stream + (c + 1) * blk, :])
    run_group(0, q0, k0, v0)


def _attn_call(qkv, bias, *, batch, seq):
    in_specs = []
    for g in range(N_GROUPS):
        d = DILATIONS[g]
        if d == 1:
            spec = pl.BlockSpec((seq, HEAD_DIM), lambda b, h: (b, h))
        else:
            spec = pl.BlockSpec((1, d, seq // d, HEAD_DIM), lambda b, h: (b, 0, 0, h))
        in_specs += [spec] * 3
    in_specs.append(pl.BlockSpec((N_GROUPS, 1, 2, N_STEPS, 2 * N_STEPS), lambda b, h: (0, h, 0, 0, 0)))
    return pl.pallas_call(
        functools.partial(_attn_kernel, seq=seq),
        grid=(batch, HEADS),
        in_specs=in_specs,
        out_specs=pl.BlockSpec((seq, HEAD_DIM), lambda b, h: (b, h)),
        out_shape=jax.ShapeDtypeStruct((batch * seq, GROUP_COLS), BF16),
        scratch_shapes=[pltpu.VMEM((N_GROUPS - 1, seq, LANES), F32)] * 3
                       + [pltpu.VMEM((3, seq, LANES), F32)],
        compiler_params=_params(2),
        name="attn_prompt",
    )(*qkv, bias)


def _attn_sample_kernel(q0, q1, q2, n0, n1, n2, c0, c1, c2, slope_ref, o_ref, *, bb):
    q_refs, n_refs, c_refs = (q0, q1, q2), (n0, n1, n2), (c0, c1, c2)
    back = (N_STEPS - jax.lax.broadcasted_iota(jnp.int32, (N_STEPS, 1, 1), 0)).astype(F32)

    def to_v_rows(stat):
        return pltpu.roll(jnp.broadcast_to(stat, (2 * HEADS, HEAD_DIM)), HEADS, axis=0)

    for b in range(bb):
        outs, lses = [], []
        for g in range(N_GROUPS):
            q8, kv8, x = q_refs[g][b], n_refs[g][b], c_refs[g][b]
            bias = (back * (-float(DILATIONS[g]))) * slope_ref[g][:, 0:1]
            s = jnp.sum(x * q8, axis=-1, keepdims=True) + bias
            s_self = jnp.sum(kv8 * q8, axis=-1, keepdims=True)
            m = jnp.maximum(jnp.max(s, axis=0), s_self)
            p = jnp.exp2(s - m)
            p_self = jnp.exp2(s_self - m)
            l = jnp.sum(p, axis=0) + p_self
            p_v = pltpu.roll(jnp.broadcast_to(p, x.shape), HEADS, axis=1)
            acc = jnp.sum(p_v * x, axis=0) + to_v_rows(p_self) * kv8
            outs.append(acc / to_v_rows(l))
            lses.append(to_v_rows(m + jnp.log(l) * LOG2E))
        mx = jnp.maximum(jnp.maximum(lses[0], lses[1]), lses[2])
        ws = [jnp.exp2(v - mx) for v in lses]
        num = ws[0] * outs[0] + ws[1] * outs[1] + ws[2] * outs[2]
        o8 = num / (ws[0] + ws[1] + ws[2])
        for hh in range(HEADS):
            o_ref[b:b + 1, hh * HEAD_DIM:(hh + 1) * HEAD_DIM] = o8[HEADS + hh:HEADS + hh + 1, :]


def _attn_sample_call(q_tiles, new_tiles, caches, layer, slopes8):
    bd = q_tiles[0].shape[0]
    bb = 8
    nb = bd // bb
    tile = (2 * HEADS, HEAD_DIM)
    in_specs = [pl.BlockSpec((bb,) + tile, lambda i: (i, 0, 0))] * 3
    in_specs += [pl.BlockSpec((bb,) + tile, lambda i: (layer * nb + i, 0, 0))] * 3
    in_specs += [pl.BlockSpec((None, bb, N_STEPS) + tile, lambda i: (layer, i, 0, 0, 0))] * 3
    in_specs.append(pl.BlockSpec((N_GROUPS,) + tile, lambda i: (0, 0, 0)))
    return pl.pallas_call(
        functools.partial(_attn_sample_kernel, bb=bb),
        grid=(nb,),
        in_specs=in_specs,
        out_specs=pl.BlockSpec((bb, GROUP_COLS), lambda i: (i, 0)),
        out_shape=jax.ShapeDtypeStruct((bd, GROUP_COLS), F32),
        compiler_params=_params(1),
        name="attn_sample",
    )(*q_tiles, *new_tiles, *caches, slopes8)


def _post_kernel(x_ref, mod_ref, ya_ref, n1w_ref, wg1_ref, wg2_ref, lnw_ref, lnb_ref, ws_ref, bs_ref,
                 wpa_ref, wps_ref, wo_ref, n2w_ref, wfi_ref, wfo_ref, *refs,
                 tm, d_model, d_ff, chunked):
    if chunked:
        o_ref, mix_scr = refs
    else:
        o_ref, vs_out_ref, mix_scr = refs
    mod = mod_ref[0]
    g1 = mod[:, 2 * d_model:3 * d_model]
    g2 = mod[:, 5 * d_model:6 * d_model]
    x = x_ref[...]

    h = _modulated_norm(x, n1w_ref[...], mod, 0, d_model).astype(BF16)
    w = SGU_WIDTH
    u = _gelu(_dot(h, wg1_ref[:, 0:w]))
    vs = _gelu(_dot(h, wg1_ref[:, w:2 * w]))
    mu = jnp.mean(vs, axis=-1, keepdims=True)
    vc = vs - mu
    var = jnp.mean(vc * vc, axis=-1, keepdims=True)
    vs = vc * jax.lax.rsqrt(var + EPS) * lnw_ref[...] + lnb_ref[...]
    half = d_model // 2
    ga = jnp.concatenate([_sigmoid(_dot(h, wg1_ref[:, 2 * w:2 * w + half])),
                          _sigmoid(_dot(h, wg2_ref[:, 0:half]))], axis=-1)
    gb = _sigmoid(_dot(h, wg2_ref[:, half:half + d_model]))

    if chunked:
        row = jax.lax.broadcasted_iota(jnp.int32, (CHUNK, CHUNK), 0)
        col = jax.lax.broadcasted_iota(jnp.int32, (CHUNK, CHUNK), 1)
        tril = col <= row
        lane = jax.lax.broadcasted_iota(jnp.int32, (CHUNK, LANES), 1)
        low = lane < SGU_GROUP_DIM
        wms = [jnp.where(tril, ws_ref[gi], 0.0).astype(BF16) for gi in range(SGU_GROUPS)]
        pairs = [jnp.concatenate([wms[2 * j], wms[2 * j + 1]], axis=1)
                 for j in range(SGU_WIDTH // LANES)]
        vs_b = vs.astype(BF16)
        zero = jnp.zeros((CHUNK, 2 * LANES), BF16)
        low2 = jnp.concatenate([low, low], axis=1)
        assert (tm // CHUNK) % 2 == 0
        for c in range(0, tm // CHUNK, 2):
            rows_a = slice(c * CHUNK, (c + 1) * CHUNK)
            rows_b = slice((c + 1) * CHUNK, (c + 2) * CHUNK)
            for j in range(SGU_WIDTH // LANES):
                cols = slice(j * LANES, (j + 1) * LANES)
                v = jnp.concatenate([vs_b[rows_a, cols], vs_b[rows_b, cols]], axis=1)
                split = jnp.concatenate([jnp.where(low2, v, zero), jnp.where(low2, zero, v)], axis=0)
                out = _dot(pairs[j], split)
                mix_scr[rows_a, cols] = out[:, :LANES]
                mix_scr[rows_b, cols] = out[:, LANES:]
        mixed = mix_scr[...] + jnp.concatenate([bs_ref[...]] * (tm // CHUNK), axis=0)
    else:
        vs_out_ref[...] = vs
        mixed = vs * ws_ref[...] + bs_ref[...]

    ys = (u * mixed).astype(BF16)
    xa = _dot(ya_ref[...].astype(BF16), wpa_ref[...])
    xs = _dot(ys, wps_ref[...])
    merged = (ga * xa + gb * xs).astype(BF16)
    x1 = x + g1 * _dot(merged, wo_ref[...])
    h2 = _modulated_norm(x1, n2w_ref[...], mod, 1, d_model).astype(BF16)
    a = _dot(h2, wfi_ref[:, 0:d_ff])
    b = _dot(h2, wfi_ref[:, d_ff:2 * d_ff])
    act = (a * _sigmoid(a) * b).astype(BF16)
    o_ref[...] = x1 + g2 * _dot(act, wfo_ref[...])


def _post_call(x, mod, ya, n1w, w_in, lnw, lnb, ws, bs, wpa, wps, wo, n2w, wfi, wfo, *,
               seq, tm, chunked, layer):
    t, d_model = x.shape
    d_ff = wfo.shape[1]
    tiles_per_seq = seq // tm
    gblk = QKV_COLS
    assert 2 * SGU_WIDTH + d_model // 2 == gblk and w_in.shape[2] == 5 * gblk

    def w_gate(j):
        return pl.BlockSpec((None, d_model, gblk), lambda i: (layer, 0, 3 + j),
                            pipeline_mode=pl.Buffered(1))

    def tok(cols):
        return pl.BlockSpec((tm, cols), lambda i: (i, 0))

    out_specs, out_shape = [tok(d_model)], [jax.ShapeDtypeStruct((t, d_model), F32)]
    if not chunked:
        out_specs.append(tok(SGU_WIDTH))
        out_shape.append(jax.ShapeDtypeStruct((t, SGU_WIDTH), F32))
    kern = functools.partial(_post_kernel, tm=tm, d_model=d_model, d_ff=d_ff, chunked=chunked)
    return pl.pallas_call(
        kern,
        grid=(t // tm,),
        in_specs=[tok(d_model),
                  pl.BlockSpec((1,) + mod.shape[1:], lambda i: (i // tiles_per_seq, 0, 0)),
                  tok(GROUP_COLS), _resident(n1w.shape), w_gate(0), w_gate(1),
                  _resident(lnw.shape), _resident(lnb.shape),
                  _resident(ws.shape), _resident(bs.shape),
                  _resident_layer(wpa.shape, layer), _resident_layer(wps.shape, layer),
                  _resident_layer(wo.shape, layer), _resident(n2w.shape),
                  _resident_layer(wfi.shape, layer), _resident_layer(wfo.shape, layer)],
        out_specs=out_specs,
        out_shape=out_shape,
        scratch_shapes=[pltpu.VMEM((tm, SGU_WIDTH), F32)],
        compiler_params=_params(1),
        name="post_chunked" if chunked else "post_tok",
    )(x, mod, ya, n1w, w_in, w_in, lnw, lnb, ws, bs, wpa, wps, wo, n2w, wfi, wfo)


def kernel(x_prompt, x_sample, cache_kv_w128, cache_kv_w512, cache_kv_w2048, c_prompt, c_sample,
           w_ada, b_ada, norm1_w, w_in, q_norm_w, k_norm_w, sgu_ln_w, sgu_ln_b, w_spatial, b_spatial,
           w_proj_att, w_proj_sgu, w_out, norm2_w, w_ffn_in, w_ffn_out):
    batch, seq, d_model = x_prompt.shape
    bd, dec_seq, _ = x_sample.shape
    depth = w_in.shape[0]
    assert dec_seq == 1 and seq % (N_STEPS * DILATIONS[-1]) == 0
    caches_in = (cache_kv_w128, cache_kv_w512, cache_kv_w2048)
    for g in range(N_GROUPS):
        assert caches_in[g].shape[2] == WINDOWS[g]
    tm = 512
    tm_in = 512

    n_c = batch + bd
    c_all = jnp.concatenate([c_prompt, c_sample], axis=0)
    c_all = jnp.pad(c_all, ((0, -n_c % 16), (0, 0)))
    mod = _ada_call(c_all, w_ada, b_ada)

    band_bias = jnp.asarray(_band_bias())
    tile = (2 * HEADS, HEAD_DIM)
    slopes8 = np.zeros((N_GROUPS,) + tile, np.float32)
    slopes8[:, :HEADS, :] = _alibi_slopes()[:, :, None]
    slopes8 = jnp.asarray(slopes8)
    caches = [caches_in[g].reshape((depth, bd, N_STEPS, DILATIONS[g] * tile[0], tile[1]))
              for g in range(N_GROUPS)]

    xp = x_prompt.reshape(batch * seq, d_model)
    xs = x_sample.reshape(bd, d_model)
    p_states = s_states = None
    s_v = []
    w_in_l = w_in.astype(BF16)
    wpa, wps, wo = w_proj_att.astype(BF16), w_proj_sgu.astype(BF16), w_out.astype(BF16)
    wfi, wfo = w_ffn_in.astype(BF16), w_ffn_out.astype(BF16)
    for l in range(depth):
        qw = jnp.tile(q_norm_w[l][:, None, :], (1, HEADS, 1)).reshape(1, QKV_COLS)
        kw = jnp.tile(k_norm_w[l][:, None, :], (1, HEADS, 1)).reshape(1, QKV_COLS)
        lnw, lnb = sgu_ln_w[l].reshape(1, -1), sgu_ln_b[l].reshape(1, -1)
        n1w, n2w = norm1_w[l].reshape(1, -1), norm2_w[l].reshape(1, -1)
        mod_p = mod[l, :batch].reshape(batch, 1, -1)
        mod_s = mod[l, batch:n_c].reshape(1, bd, -1)
        bs_p = jnp.repeat(b_spatial[l].T, SGU_GROUP_DIM, axis=1)
        ws_s = jnp.repeat(w_spatial[l][:, 0, 0], SGU_GROUP_DIM).reshape(1, SGU_WIDTH)
        bs_s = jnp.repeat(b_spatial[l][:, 0], SGU_GROUP_DIM).reshape(1, SGU_WIDTH)

        outs = _inproj_call(xp, mod_p, n1w, w_in_l, qw, kw,
                            seq=seq, tm=tm_in, deinterleave=True, act_dtype=BF16,
                            layer=l, depth=depth, prev_states=p_states)
        qkv, p_states = outs[:9], outs[9:]
        ya = _attn_call(qkv, band_bias, batch=batch, seq=seq)
        (xp,) = _post_call(xp, mod_p, ya, n1w, w_in_l, lnw, lnb, w_spatial[l], bs_p,
                           wpa, wps, wo, n2w, wfi, wfo, seq=seq, tm=tm, chunked=True, layer=l)

        outs = _inproj_call(xs, mod_s, n1w, w_in_l, qw, kw,
                            seq=bd, tm=bd, deinterleave=False, act_dtype=F32,
                            layer=l, depth=depth, prev_states=s_states)
        q_tiles, s_states = [o.reshape((bd,) + tile) for o in outs[:3]], outs[3:]
        new_tiles = [s.reshape((depth * bd,) + tile) for s in s_states]
        ya = _attn_sample_call(q_tiles, new_tiles, caches, l, slopes8)
        xs, vs = _post_call(xs, mod_s, ya, n1w, w_in_l, lnw, lnb, ws_s, bs_s,
                            wpa, wps, wo, n2w, wfi, wfo, seq=bd, tm=bd, chunked=False, layer=l)
        s_v.append(vs.reshape(bd, 1, SGU_WIDTH))

    p_states = [s.reshape(depth, batch, -1, 2, HEADS, HEAD_DIM) for s in p_states]
    s_states = [s.reshape(depth, bd, 1, 2, HEADS, HEAD_DIM) for s in s_states]
    return (xp.reshape(batch, seq, d_model), xs.reshape(bd, 1, d_model),
            p_states[0], p_states[1], p_states[2], s_states[0], s_states[1], s_states[2],
            jnp.stack(s_v))
```

```python
import functools
import math

import jax
import jax.numpy as jnp
import numpy as np
from jax.experimental import pallas as pl
from jax.experimental.pallas import tpu as pltpu

HEAD_DIM = 128
N_GROUPS = 3
HEADS = 4
GROUP_COLS = HEADS * HEAD_DIM
QKV_COLS = N_GROUPS * GROUP_COLS
WINDOWS = (128, 512, 2048)
DILATIONS = (1, 4, 16)
N_STEPS = 128
CHUNK = 128
SGU_WIDTH = 512
SGU_GROUPS = 8
SGU_GROUP_DIM = SGU_WIDTH // SGU_GROUPS
N_ADA = 6
EPS = 1e-6
NEG_INF = -1e30
LOG2E = math.log2(math.e)
QK_SCALE = HEAD_DIM ** -0.5 * LOG2E

LANES = 128
RELAYOUT_STRIDE = 4
VMEM_LIMIT_BYTES = 56 * 1024 * 1024

BF16 = jnp.bfloat16
F32 = jnp.float32


def _params(n_axes, flags=None):
    return pltpu.CompilerParams(dimension_semantics=("arbitrary",) * n_axes,
                                vmem_limit_bytes=VMEM_LIMIT_BYTES, flags=flags)


def _resident(shape):
    nd = len(shape)
    return pl.BlockSpec(shape, lambda *_: (0,) * nd, pipeline_mode=pl.Buffered(1))


def _resident_layer(shape, layer):
    nd = len(shape)
    return pl.BlockSpec((None,) + tuple(shape[1:]), lambda *_: (layer,) + (0,) * (nd - 1),
                        pipeline_mode=pl.Buffered(1))


def _dot(a, b):
    return jnp.dot(a, b, preferred_element_type=F32)


def _dot_nt(a, b):
    return jax.lax.dot_general(a, b, (((1,), (1,)), ((), ())), preferred_element_type=F32)


def _rms(x):
    return x * jax.lax.rsqrt(jnp.mean(x * x, axis=-1, keepdims=True) + EPS)


def _gelu(x):
    return 0.5 * x * (1.0 + jax.lax.erf(x * (2.0 ** -0.5)))


def _sigmoid(x):
    return 1.0 / (1.0 + jnp.exp(-x))


def _ada_kernel(c_ref, w_ref, b_ref, o_ref):
    c = c_ref[...]
    a = c * _sigmoid(c)
    w = w_ref[...]
    a_hi = a.astype(BF16)
    a_lo = (a - a_hi.astype(F32)).astype(BF16)
    w_hi = w.astype(BF16)
    w_lo = (w - w_hi.astype(F32)).astype(BF16)
    rows = a.shape[0]
    both = _dot(jnp.concatenate([a_hi, a_lo], axis=0), w_hi)
    o_ref[...] = both[:rows] + both[rows:] + _dot(a_hi, w_lo) + b_ref[...]


def _ada_call(c_all, w_ada, b_ada):
    depth, d, n = w_ada.shape
    rows = c_all.shape[0]
    tn = 1024
    return pl.pallas_call(
        _ada_kernel,
        grid=(depth, n // tn),
        in_specs=[pl.BlockSpec((rows, d), lambda l, j: (0, 0)),
                  pl.BlockSpec((None, d, tn), lambda l, j: (l, 0, j)),
                  pl.BlockSpec((None, 1, tn), lambda l, j: (l, 0, j))],
        out_specs=pl.BlockSpec((None, rows, tn), lambda l, j: (l, 0, j)),
        out_shape=jax.ShapeDtypeStruct((depth, rows, n), F32),
        compiler_params=_params(2),
        name="adaln_mod",
    )(c_all, w_ada, b_ada.reshape(depth, 1, n))


def _modulated_norm(x, w, mod, which, d_model):
    shift = mod[:, 3 * which * d_model:(3 * which + 1) * d_model]
    scale = mod[:, (3 * which + 1) * d_model:(3 * which + 2) * d_model]
    return _rms(x) * w * (1.0 + scale) + shift


def _inproj_kernel(x_ref, mod_ref, n1w_ref, w_ref, qw_ref, kw_ref,
                   *refs, tm, d_model, dils, state_rows, tiles_per_seq, deinterleave, n_aliased):
    refs = refs[n_aliased:]
    if deinterleave:
        q_refs, k_refs, v_refs, st_refs = refs[0:9:3], refs[1:9:3], refs[2:9:3], refs[9:12]
        h_scr, slab, slab2, kv_stash = refs[12:]
    else:
        q_refs, st_refs = refs[0:3], refs[3:6]
        h_scr, slab, slab2, kv_stash = refs[6:]
    ti = pl.program_id(0) % tiles_per_seq
    tile_rows = 2 * HEADS

    h = _modulated_norm(x_ref[...], n1w_ref[...], mod_ref[0], 0, d_model)
    h_scr[...] = h.astype(BF16)

    def proj(c0, width):
        return _dot(h_scr[...], w_ref[:, c0:c0 + width])

    def put_tiles(ref, first_row, val):
        for hh in range(HEADS):
            ref[pl.ds(first_row + hh, tm, stride=tile_rows), :] = val[:, hh * HEAD_DIM:(hh + 1) * HEAD_DIM]

    def put(ref, g, val, slot, slabs=None):
        d = dils[g]
        if d == 1:
            ref[...] = val.astype(ref.dtype)
            return
        if slabs is None:
            slabs = [slab.at[g - 1, hh] for hh in range(HEADS)]
            for hh in range(HEADS):
                slabs[hh][...] = val[:, hh * HEAD_DIM:(hh + 1) * HEAD_DIM]
        if d <= RELAYOUT_STRIDE:
            for r in range(d):
                for hh in range(HEADS):
                    ref[0, r, :, hh * HEAD_DIM:(hh + 1) * HEAD_DIM] = (
                        slabs[hh][pl.ds(r, tm // d, stride=d), :].astype(ref.dtype))
            return
        s1, s2 = RELAYOUT_STRIDE, d // RELAYOUT_STRIDE
        part = tm // s1
        for hh in range(HEADS):
            for r1 in range(s1):
                slab2[slot, hh,r1 * part:(r1 + 1) * part, :] = slabs[hh][pl.ds(r1, part, stride=s1), :]
            for r1 in range(s1):
                for r2 in range(s2):
                    ref[0, r1 + s1 * r2, :, hh * HEAD_DIM:(hh + 1) * HEAD_DIM] = (
                        slab2[slot, hh,pl.ds(r1 * part + r2, tm // d, stride=s2), :].astype(ref.dtype))

    def head_rms(z, w):
        parts = [_rms(z[:, hh * HEAD_DIM:(hh + 1) * HEAD_DIM]) for hh in range(HEADS)]
        return jnp.concatenate(parts, axis=-1) * w

    def put_state(g, kn, v):
        if state_rows[g] is None:
            put_tiles(st_refs[g], 0, kn)
            put_tiles(st_refs[g], HEADS, v)
        else:
            for hh in range(HEADS):
                cols = slice(hh * HEAD_DIM, (hh + 1) * HEAD_DIM)
                kv_stash[g, hh] = kn[:, cols]
                kv_stash[g, HEADS + hh] = v[:, cols]

    def flush_state(g):
        st, rows = st_refs[g], min(state_rows[g], tm)
        first = tiles_per_seq - max(state_rows[g] // tm, 1)

        @pl.when(ti >= first)
        def _():
            for j in range(2 * HEADS):
                st[pl.ds(j, rows, stride=2 * HEADS), :] = kv_stash[g, j, tm - rows:, :]

    for g in reversed(range(N_GROUPS)):
        c0 = g * GROUP_COLS
        q = head_rms(proj(c0, GROUP_COLS), qw_ref[:, c0:c0 + GROUP_COLS] * QK_SCALE)
        kn = head_rms(proj(QKV_COLS + c0, GROUP_COLS), kw_ref[:, c0:c0 + GROUP_COLS])
        v = proj(2 * QKV_COLS + c0, GROUP_COLS)
        put_state(g, kn, v)
        if not deinterleave:
            put_tiles(q_refs[g], 0, q)
            put_tiles(q_refs[g], HEADS, jnp.zeros_like(q))
            continue
        put(q_refs[g], g, q, 0)
        put(k_refs[g], g, kn, 1, [kv_stash.at[g, hh] for hh in range(HEADS)])
        put(v_refs[g], g, v, 2, [kv_stash.at[g, HEADS + hh] for hh in range(HEADS)])

    for g in range(N_GROUPS):
        if state_rows[g] is not None:
            flush_state(g)


def _inproj_call(x, mod, n1w, w_in, qw, kw, *, seq, tm, deinterleave, act_dtype,
                 layer=0, depth=1, prev_states=None):
    t, d_model = x.shape
    batch = t // seq
    tiles_per_seq = seq // tm

    def tok(cols, dtype):
        return (jax.ShapeDtypeStruct((t, cols), dtype), pl.BlockSpec((tm, cols), lambda i: (i, 0)))

    def grp(g, dtype):
        d = DILATIONS[g]
        if d == 1:
            return tok(GROUP_COLS, dtype)
        assert tm % d == 0 and (tm // d) % 16 == 0
        return (jax.ShapeDtypeStruct((batch, d, seq // d, GROUP_COLS), dtype),
                pl.BlockSpec((1, d, tm // d, GROUP_COLS),
                             lambda i: (i // tiles_per_seq, 0, i % tiles_per_seq, 0)))

    tile_rows = 2 * HEADS
    outs = []
    for g in range(N_GROUPS):
        if deinterleave:
            outs += [grp(g, act_dtype)] * 3
        else:
            assert t == tm
            outs.append((jax.ShapeDtypeStruct((t * tile_rows, LANES), F32),
                         pl.BlockSpec((t * tile_rows, LANES), lambda i: (0, 0))))

    state_rows = []
    for g in range(N_GROUPS):
        if not deinterleave:
            state_rows.append(None)
            outs.append((jax.ShapeDtypeStruct((depth * t * tile_rows, LANES), F32),
                         pl.BlockSpec((t * tile_rows, LANES), lambda i: (layer, 0))))
            continue
        rows = min(WINDOWS[g], seq)
        state_rows.append(rows)
        blk_rows = min(rows, tm)
        assert rows % blk_rows == 0
        nblk = rows // blk_rows
        shape = jax.ShapeDtypeStruct((depth * batch * rows * 2 * HEADS, LANES), F32)
        spec = pl.BlockSpec(
            (blk_rows * 2 * HEADS, LANES),
            lambda i, nblk=nblk: ((layer * batch + i // tiles_per_seq) * nblk
                                  + jnp.maximum(i % tiles_per_seq - (tiles_per_seq - nblk), 0), 0))
        outs.append((shape, spec))

    aliased = list(prev_states) if prev_states is not None else []
    n_in = 6
    w_qkv = pl.BlockSpec((None, d_model, 3 * QKV_COLS), lambda i: (layer, 0, 0),
                         pipeline_mode=pl.Buffered(1))
    kern = functools.partial(_inproj_kernel, tm=tm, d_model=d_model, dils=DILATIONS,
                             state_rows=tuple(state_rows), tiles_per_seq=tiles_per_seq,
                             deinterleave=deinterleave, n_aliased=len(aliased))
    return pl.pallas_call(
        kern,
        grid=(t // tm,),
        in_specs=[pl.BlockSpec((tm, d_model), lambda i: (i, 0)),
                  pl.BlockSpec((1,) + mod.shape[1:], lambda i: (i // tiles_per_seq, 0, 0)),
                  _resident((1, d_model)),
                  w_qkv,
                  _resident((1, QKV_COLS)), _resident((1, QKV_COLS))]
                 + [pl.BlockSpec(memory_space=pl.ANY)] * len(aliased),
        out_specs=[o[1] for o in outs],
        out_shape=[o[0] for o in outs],
        input_output_aliases={n_in + g: len(outs) - N_GROUPS + g for g in range(len(aliased))},
        scratch_shapes=[pltpu.VMEM((tm, d_model), BF16),
                        pltpu.VMEM((N_GROUPS - 1, HEADS, tm, LANES), F32),
                        pltpu.VMEM((3, HEADS, tm, LANES), F32),
                        pltpu.VMEM((N_GROUPS, 2 * HEADS, tm, LANES), F32)],
        compiler_params=_params(1),
        name="inproj_deint" if deinterleave else "inproj_tok",
    )(x, mod, n1w, w_in, qw, kw, *aliased)


def _alibi_slopes():
    h = np.arange(1, N_GROUPS * HEADS + 1, dtype=np.float32)
    s = np.power(np.float32(2.0), -8.0 * h / (N_GROUPS * HEADS)).astype(np.float32)
    return (s.astype(np.float64) * LOG2E).astype(np.float32).reshape(N_GROUPS, HEADS)


def _band_bias():
    qi = np.arange(N_STEPS)[:, None]
    ki = np.arange(2 * N_STEPS)[None, :]
    dist = N_STEPS + qi - ki
    valid = (dist >= 0) & (dist <= N_STEPS)
    slopes = _alibi_slopes()
    out = np.empty((N_GROUPS, HEADS, 2, N_STEPS, 2 * N_STEPS), np.float32)
    for g in range(N_GROUPS):
        for hh in range(HEADS):
            b = np.where(valid, -slopes[g, hh] * (dist * DILATIONS[g]).astype(np.float32),
                         np.float32(NEG_INF)).astype(np.float32)
            out[g, hh, 1] = b
            first = b.copy()
            first[:, :N_STEPS] = NEG_INF
            out[g, hh, 0] = first
    return out


def _attn_block(q, k2, v2, bias):
    s = _dot_nt(q, k2) + bias
    m = jnp.max(s, axis=-1, keepdims=True)
    p = jnp.exp2(s - m)
    l = jnp.sum(p, axis=-1, keepdims=True)
    return _dot(p.astype(v2.dtype), v2), m, l


ATTN_UNROLL = 32


def _attn_kernel(q0, k0, v0, q1, k1, v1, q2, k2, v2, bias_ref, o_ref, o_scr, m_scr, l_scr, stage_scr,
                 *, seq):
    blk = N_STEPS
    s1 = RELAYOUT_STRIDE

    def run_group(g, qr, kr, vr):
        d = DILATIONS[g]
        nblk = seq // d // blk

        def one_block(r, i):
            own = pl.multiple_of(i * blk, blk)
            prev = pl.multiple_of(jnp.maximum(i - 1, 0) * blk, blk)
            if d == 1:
                ld = lambda ref, start: ref[pl.ds(start, blk), :]
            else:
                ld = lambda ref, start: ref[0, r, pl.ds(start, blk), :]
            k2_ = jnp.concatenate([ld(kr, prev), ld(kr, own)], axis=0)
            v2_ = jnp.concatenate([ld(vr, prev), ld(vr, own)], axis=0)
            o, m, l = _attn_block(ld(qr, own), k2_, v2_, bias_ref[g, 0, jnp.minimum(i, 1)])
            m = jnp.broadcast_to(m, (blk, LANES))
            l = jnp.broadcast_to(l, (blk, LANES))
            if 1 < d <= s1:
                rows = pl.ds(i * (blk * d) + r, blk, stride=d)
                o_scr[g - 1, rows, :] = o
                m_scr[g - 1, rows, :] = m
                l_scr[g - 1, rows, :] = l
                return
            if d > s1:
                s2 = d // s1
                rows = pl.ds((r % s1) * (seq // s1) + i * (blk * s2) + r // s1, blk, stride=s2)
                stage_scr[0, rows, :] = o
                stage_scr[1, rows, :] = m
                stage_scr[2, rows, :] = l
                return
            rows = pl.ds(own, blk)
            ms = [m, m_scr[0, rows, :], m_scr[1, rows, :]]
            mx = jnp.maximum(jnp.maximum(ms[0], ms[1]), ms[2])
            ws = [jnp.exp2(x - mx) for x in ms]
            num = ws[0] * o + ws[1] * o_scr[0, rows, :] + ws[2] * o_scr[1, rows, :]
            den = ws[0] * l + ws[1] * l_scr[0, rows, :] + ws[2] * l_scr[1, rows, :]
            o_ref[rows, :] = (num / den).astype(o_ref.dtype)

        ur = min(ATTN_UNROLL, d)
        ui = ATTN_UNROLL // ur
        assert d % ur == 0 and nblk % ui == 0

        def body(it, carry):
            i0 = (it // (d // ur)) * ui
            r0 = (it % (d // ur)) * ur
            for a in range(ui):
                for b in range(ur):
                    one_block(r0 + b, i0 + a)
            return carry

        jax.lax.fori_loop(0, (d // ur) * (nblk // ui), body, 0)

    run_group(1, q1, k1, v1)
    run_group(2, q2, k2, v2)
    g = N_GROUPS - 1
    assert DILATIONS[g] > s1 and all(d <= s1 for d in DILATIONS[:g])
    per_stream = seq // s1
    for a, dst in enumerate((o_scr, m_scr, l_scr)):
        for r1 in range(s1):
            for c in range(per_stream // blk):
                dst[g - 1, pl.ds(s1 * c * blk + r1, blk, stride=s1), :] = (
                    stage_scr[a, r1 * per_stream + c * blk:r1 * per_stream + (c + 1) * blk, :])
    run_group(0, q0, k0, v0)


def _attn_call(qkv, bias, *, batch, seq):
    in_specs = []
    for g in range(N_GROUPS):
        d = DILATIONS[g]
        if d == 1:
            spec = pl.BlockSpec((seq, HEAD_DIM), lambda b, h: (b, h))
        else:
            spec = pl.BlockSpec((1, d, seq // d, HEAD_DIM), lambda b, h: (b, 0, 0, h))
        in_specs += [spec] * 3
    in_specs.append(pl.BlockSpec((N_GROUPS, 1, 2, N_STEPS, 2 * N_STEPS), lambda b, h: (0, h, 0, 0, 0)))
    return pl.pallas_call(
        functools.partial(_attn_kernel, seq=seq),
        grid=(batch, HEADS),
        in_specs=in_specs,
        out_specs=pl.BlockSpec((seq, HEAD_DIM), lambda b, h: (b, h)),
        out_shape=jax.ShapeDtypeStruct((batch * seq, GROUP_COLS), BF16),
        scratch_shapes=[pltpu.VMEM((N_GROUPS - 1, seq, LANES), F32)] * 3
                       + [pltpu.VMEM((3, seq, LANES), F32)],
        compiler_params=_params(2),
        name="attn_prompt",
    )(*qkv, bias)


def _attn_sample_kernel(q0, q1, q2, n0, n1, n2, c0, c1, c2, slope_ref, o_ref, *, bb):
    q_refs, n_refs, c_refs = (q0, q1, q2), (n0, n1, n2), (c0, c1, c2)
    half = N_STEPS // 2
    tile = (2 * HEADS, HEAD_DIM)
    top = jax.lax.broadcasted_iota(jnp.int32, tile, 0) < HEADS
    steps_a = (N_STEPS - jax.lax.broadcasted_iota(jnp.int32, (half, 1, 1), 0)).astype(F32)
    back = jnp.where(top[:, 0:1], steps_a, steps_a - float(half))

    def both_halves(t):
        return jnp.where(top, t, pltpu.roll(t, HEADS, axis=0))

    for b in range(bb):
        outs, lses = [], []
        for g in range(N_GROUPS):
            q8, kv8 = q_refs[g][b], n_refs[g][b]
            xa, xb = c_refs[g][b, 0:half], c_refs[g][b, half:N_STEPS]
            slope = both_halves(slope_ref[g])[:, 0:1]
            bias = (back * (-float(DILATIONS[g]))) * slope
            prod = jnp.where(top, xa * q8, pltpu.roll(xb * q8, HEADS, axis=1))
            s = jnp.sum(prod, axis=-1, keepdims=True) + bias
            s_self = both_halves(jnp.broadcast_to(jnp.sum(kv8 * q8, axis=-1, keepdims=True), tile))
            m = jnp.broadcast_to(jnp.max(s, axis=0), tile)
            m = jnp.maximum(jnp.maximum(m, pltpu.roll(m, HEADS, axis=0)), s_self)
            p = jnp.exp2(s - m[:, 0:1])
            p_self = jnp.exp2(s_self - m)
            l = jnp.broadcast_to(jnp.sum(p, axis=0), tile)
            l = l + pltpu.roll(l, HEADS, axis=0) + p_self
            p_b = jnp.broadcast_to(p, xb.shape)
            acc = jnp.sum(pltpu.roll(p_b, HEADS, axis=1) * xa + p_b * xb, axis=0) + p_self * kv8
            outs.append(acc / l)
            lses.append(m + jnp.log(l) * LOG2E)
        mx = jnp.maximum(jnp.maximum(lses[0], lses[1]), lses[2])
        ws = [jnp.exp2(v - mx) for v in lses]
        num = ws[0] * outs[0] + ws[1] * outs[1] + ws[2] * outs[2]
        o8 = num / (ws[0] + ws[1] + ws[2])
        for hh in range(HEADS):
            o_ref[b:b + 1, hh * HEAD_DIM:(hh + 1) * HEAD_DIM] = o8[HEADS + hh:HEADS + hh + 1, :]


def _attn_sample_call(q_tiles, new_tiles, caches, layer, slopes8):
    bd = q_tiles[0].shape[0]
    bb = 8
    nb = bd // bb
    tile = (2 * HEADS, HEAD_DIM)
    in_specs = [pl.BlockSpec((bb,) + tile, lambda i: (i, 0, 0))] * 3
    in_specs += [pl.BlockSpec((bb,) + tile, lambda i: (layer * nb + i, 0, 0))] * 3
    in_specs += [pl.BlockSpec((None, bb, N_STEPS) + tile, lambda i: (layer, i, 0, 0, 0))] * 3
    in_specs.append(pl.BlockSpec((N_GROUPS,) + tile, lambda i: (0, 0, 0)))
    return pl.pallas_call(
        functools.partial(_attn_sample_kernel, bb=bb),
        grid=(nb,),
        in_specs=in_specs,
        out_specs=pl.BlockSpec((bb, GROUP_COLS), lambda i: (i, 0)),
        out_shape=jax.ShapeDtypeStruct((bd, GROUP_COLS), F32),
        compiler_params=_params(1),
        name="attn_sample",
    )(*q_tiles, *new_tiles, *caches, slopes8)


def _post_kernel(x_ref, mod_ref, ya_ref, n1w_ref, wg1_ref, wg2_ref, lnw_ref, lnb_ref, ws_ref, bs_ref,
                 wpa_ref, wps_ref, wo_ref, n2w_ref, wfi_ref, wfo_ref, *refs,
                 tm, d_model, d_ff, chunked):
    if chunked:
        o_ref, mix_scr = refs
    else:
        o_ref, vs_out_ref, mix_scr = refs
    mod = mod_ref[0]
    g1 = mod[:, 2 * d_model:3 * d_model]
    g2 = mod[:, 5 * d_model:6 * d_model]
    x = x_ref[...]

    h = _modulated_norm(x, n1w_ref[...], mod, 0, d_model).astype(BF16)
    w = SGU_WIDTH
    u = _gelu(_dot(h, wg1_ref[:, 0:w]))
    vs = _gelu(_dot(h, wg1_ref[:, w:2 * w]))
    mu = jnp.mean(vs, axis=-1, keepdims=True)
    vc = vs - mu
    var = jnp.mean(vc * vc, axis=-1, keepdims=True)
    vs = vc * jax.lax.rsqrt(var + EPS) * lnw_ref[...] + lnb_ref[...]
    half = d_model // 2
    ga = jnp.concatenate([_sigmoid(_dot(h, wg1_ref[:, 2 * w:2 * w + half])),
                          _sigmoid(_dot(h, wg2_ref[:, 0:half]))], axis=-1)
    gb = _sigmoid(_dot(h, wg2_ref[:, half:half + d_model]))

    if chunked:
        row = jax.lax.broadcasted_iota(jnp.int32, (CHUNK, CHUNK), 0)
        col = jax.lax.broadcasted_iota(jnp.int32, (CHUNK, CHUNK), 1)
        tril = col <= row
        lane = jax.lax.broadcasted_iota(jnp.int32, (CHUNK, LANES), 1)
        low = lane < SGU_GROUP_DIM
        wms = [jnp.where(tril, ws_ref[gi], 0.0).astype(BF16) for gi in range(SGU_GROUPS)]
        vs_b = vs.astype(BF16)
        for c in range(tm // CHUNK):
            rows = slice(c * CHUNK, (c + 1) * CHUNK)
            for j in range(SGU_WIDTH // LANES):
                cols = slice(j * LANES, (j + 1) * LANES)
                v = vs_b[rows, cols]
                mix_scr[rows, cols] = jnp.where(low, _dot(wms[2 * j], v), _dot(wms[2 * j + 1], v))
        mixed = mix_scr[...] + jnp.concatenate([bs_ref[...]] * (tm // CHUNK), axis=0)
    else:
        vs_out_ref[...] = vs
        mixed = vs * ws_ref[...] + bs_ref[...]

    ys = (u * mixed).astype(BF16)
    xa = _dot(ya_ref[...].astype(BF16), wpa_ref[...])
    xs = _dot(ys, wps_ref[...])
    merged = (ga * xa + gb * xs).astype(BF16)
    x1 = x + g1 * _dot(merged, wo_ref[...])
    h2 = _modulated_norm(x1, n2w_ref[...], mod, 1, d_model).astype(BF16)
    a = _dot(h2, wfi_ref[:, 0:d_ff])
    b = _dot(h2, wfi_ref[:, d_ff:2 * d_ff])
    act = (a * _sigmoid(a) * b).astype(BF16)
    o_ref[...] = x1 + g2 * _dot(act, wfo_ref[...])


def _post_call(x, mod, ya, n1w, w_in, lnw, lnb, ws, bs, wpa, wps, wo, n2w, wfi, wfo, *,
               seq, tm, chunked, layer):
    t, d_model = x.shape
    d_ff = wfo.shape[1]
    tiles_per_seq = seq // tm
    gblk = QKV_COLS
    assert 2 * SGU_WIDTH + d_model // 2 == gblk and w_in.shape[2] == 5 * gblk

    def w_gate(j):
        return pl.BlockSpec((None, d_model, gblk), lambda i: (layer, 0, 3 + j),
                            pipeline_mode=pl.Buffered(1))

    def tok(cols):
        return pl.BlockSpec((tm, cols), lambda i: (i, 0))

    out_specs, out_shape = [tok(d_model)], [jax.ShapeDtypeStruct((t, d_model), F32)]
    if not chunked:
        out_specs.append(tok(SGU_WIDTH))
        out_shape.append(jax.ShapeDtypeStruct((t, SGU_WIDTH), F32))
    kern = functools.partial(_post_kernel, tm=tm, d_model=d_model, d_ff=d_ff, chunked=chunked)
    return pl.pallas_call(
        kern,
        grid=(t // tm,),
        in_specs=[tok(d_model),
                  pl.BlockSpec((1,) + mod.shape[1:], lambda i: (i // tiles_per_seq, 0, 0)),
                  tok(GROUP_COLS), _resident(n1w.shape), w_gate(0), w_gate(1),
                  _resident(lnw.shape), _resident(lnb.shape),
                  _resident(ws.shape), _resident(bs.shape),
                  _resident_layer(wpa.shape, layer), _resident_layer(wps.shape, layer),
                  _resident_layer(wo.shape, layer), _resident(n2w.shape),
                  _resident_layer(wfi.shape, layer), _resident_layer(wfo.shape, layer)],
        out_specs=out_specs,
        out_shape=out_shape,
        scratch_shapes=[pltpu.VMEM((tm, SGU_WIDTH), F32)],
        compiler_params=_params(1),
        name="post_chunked" if chunked else "post_tok",
    )(x, mod, ya, n1w, w_in, w_in, lnw, lnb, ws, bs, wpa, wps, wo, n2w, wfi, wfo)


def kernel(x_prompt, x_sample, cache_kv_w128, cache_kv_w512, cache_kv_w2048, c_prompt, c_sample,
           w_ada, b_ada, norm1_w, w_in, q_norm_w, k_norm_w, sgu_ln_w, sgu_ln_b, w_spatial, b_spatial,
           w_proj_att, w_proj_sgu, w_out, norm2_w, w_ffn_in, w_ffn_out):
    batch, seq, d_model = x_prompt.shape
    bd, dec_seq, _ = x_sample.shape
    depth = w_in.shape[0]
    assert dec_seq == 1 and seq % (N_STEPS * DILATIONS[-1]) == 0
    caches_in = (cache_kv_w128, cache_kv_w512, cache_kv_w2048)
    for g in range(N_GROUPS):
        assert caches_in[g].shape[2] == WINDOWS[g]
    tm = 512
    tm_in = 512

    n_c = batch + bd
    c_all = jnp.concatenate([c_prompt, c_sample], axis=0)
    c_all = jnp.pad(c_all, ((0, -n_c % 16), (0, 0)))
    mod = _ada_call(c_all, w_ada, b_ada)

    band_bias = jnp.asarray(_band_bias())
    tile = (2 * HEADS, HEAD_DIM)
    slopes8 = np.zeros((N_GROUPS,) + tile, np.float32)
    slopes8[:, :HEADS, :] = _alibi_slopes()[:, :, None]
    slopes8 = jnp.asarray(slopes8)
    caches = [caches_in[g].reshape((depth, bd, N_STEPS, DILATIONS[g] * tile[0], tile[1]))
              for g in range(N_GROUPS)]

    xp = x_prompt.reshape(batch * seq, d_model)
    xs = x_sample.reshape(bd, d_model)
    p_states = s_states = None
    s_v = []
    w_in_l = w_in.astype(BF16)
    wpa, wps, wo = w_proj_att.astype(BF16), w_proj_sgu.astype(BF16), w_out.astype(BF16)
    wfi, wfo = w_ffn_in.astype(BF16), w_ffn_out.astype(BF16)
    for l in range(depth):
        qw = jnp.tile(q_norm_w[l][:, None, :], (1, HEADS, 1)).reshape(1, QKV_COLS)
        kw = jnp.tile(k_norm_w[l][:, None, :], (1, HEADS, 1)).reshape(1, QKV_COLS)
        lnw, lnb = sgu_ln_w[l].reshape(1, -1), sgu_ln_b[l].reshape(1, -1)
        n1w, n2w = norm1_w[l].reshape(1, -1), norm2_w[l].reshape(1, -1)
        mod_p = mod[l, :batch].reshape(batch, 1, -1)
        mod_s = mod[l, batch:n_c].reshape(1, bd, -1)
        bs_p = jnp.repeat(b_spatial[l].T, SGU_GROUP_DIM, axis=1)
        ws_s = jnp.repeat(w_spatial[l][:, 0, 0], SGU_GROUP_DIM).reshape(1, SGU_WIDTH)
        bs_s = jnp.repeat(b_spatial[l][:, 0], SGU_GROUP_DIM).reshape(1, SGU_WIDTH)

        outs = _inproj_call(xp, mod_p, n1w, w_in_l, qw, kw,
                            seq=seq, tm=tm_in, deinterleave=True, act_dtype=BF16,
                            layer=l, depth=depth, prev_states=p_states)
        qkv, p_states = outs[:9], outs[9:]
        ya = _attn_call(qkv, band_bias, batch=batch, seq=seq)
        (xp,) = _post_call(xp, mod_p, ya, n1w, w_in_l, lnw, lnb, w_spatial[l], bs_p,
                           wpa, wps, wo, n2w, wfi, wfo, seq=seq, tm=tm, chunked=True, layer=l)

        outs = _inproj_call(xs, mod_s, n1w, w_in_l, qw, kw,
                            seq=bd, tm=bd, deinterleave=False, act_dtype=F32,
                            layer=l, depth=depth, prev_states=s_states)
        q_tiles, s_states = [o.reshape((bd,) + tile) for o in outs[:3]], outs[3:]
        new_tiles = [s.reshape((depth * bd,) + tile) for s in s_states]
        ya = _attn_sample_call(q_tiles, new_tiles, caches, l, slopes8)
        xs, vs = _post_call(xs, mod_s, ya, n1w, w_in_l, lnw, lnb, ws_s, bs_s,
                            wpa, wps, wo, n2w, wfi, wfo, seq=bd, tm=bd, chunked=False, layer=l)
        s_v.append(vs.reshape(bd, 1, SGU_WIDTH))

    p_states = [s.reshape(depth, batch, -1, 2, HEADS, HEAD_DIM) for s in p_states]
    s_states = [s.reshape(depth, bd, 1, 2, HEADS, HEAD_DIM) for s in s_states]
    return (xp.reshape(batch, seq, d_model), xs.reshape(bd, 1, d_model),
            p_states[0], p_states[1], p_states[2], s_states[0], s_states[1], s_states[2],
            jnp.stack(s_v))
```

```python
import functools
import math

import jax
import jax.numpy as jnp
import numpy as np
from jax.experimental import pallas as pl
from jax.experimental.pallas import tpu as pltpu

HEAD_DIM = 128
N_GROUPS = 3
HEADS = 4
GROUP_COLS = HEADS * HEAD_DIM
QKV_COLS = N_GROUPS * GROUP_COLS
WINDOWS = (128, 512, 2048)
DILATIONS = (1, 4, 16)
N_STEPS = 128
CHUNK = 128
SGU_WIDTH = 512
SGU_GROUPS = 8
SGU_GROUP_DIM = SGU_WIDTH // SGU_GROUPS
N_ADA = 6
EPS = 1e-6
NEG_INF = -1e30
LOG2E = math.log2(math.e)
QK_SCALE = HEAD_DIM ** -0.5 * LOG2E

LANES = 128
RELAYOUT_STRIDE = 4
VMEM_LIMIT_BYTES = 56 * 1024 * 1024

BF16 = jnp.bfloat16
F32 = jnp.float32


def _params(n_axes, flags=None):
    return pltpu.CompilerParams(dimension_semantics=("arbitrary",) * n_axes,
                                vmem_limit_bytes=VMEM_LIMIT_BYTES, flags=flags)


def _resident(shape):
    nd = len(shape)
    return pl.BlockSpec(shape, lambda *_: (0,) * nd, pipeline_mode=pl.Buffered(1))


def _resident_layer(shape, layer):
    nd = len(shape)
    return pl.BlockSpec((None,) + tuple(shape[1:]), lambda *_: (layer,) + (0,) * (nd - 1),
                        pipeline_mode=pl.Buffered(1))


def _dot(a, b):
    return jnp.dot(a, b, preferred_element_type=F32)


def _dot_nt(a, b):
    return jax.lax.dot_general(a, b, (((1,), (1,)), ((), ())), preferred_element_type=F32)


def _rms(x):
    return x * jax.lax.rsqrt(jnp.mean(x * x, axis=-1, keepdims=True) + EPS)


def _gelu(x):
    return 0.5 * x * (1.0 + jax.lax.erf(x * (2.0 ** -0.5)))


def _sigmoid(x):
    return 1.0 / (1.0 + jnp.exp(-x))


def _ada_kernel(c_ref, w_ref, b_ref, o_ref):
    c = c_ref[...]
    a = c * _sigmoid(c)
    w = w_ref[...]
    a_hi = a.astype(BF16)
    a_lo = (a - a_hi.astype(F32)).astype(BF16)
    w_hi = w.astype(BF16)
    w_lo = (w - w_hi.astype(F32)).astype(BF16)
    rows = a.shape[0]
    both = _dot(jnp.concatenate([a_hi, a_lo], axis=0), w_hi)
    o_ref[...] = both[:rows] + both[rows:] + _dot(a_hi, w_lo) + b_ref[...]


def _ada_call(c_all, w_ada, b_ada):
    depth, d, n = w_ada.shape
    rows = c_all.shape[0]
    tn = 1024
    return pl.pallas_call(
        _ada_kernel,
        grid=(depth, n // tn),
        in_specs=[pl.BlockSpec((rows, d), lambda l, j: (0, 0)),
                  pl.BlockSpec((None, d, tn), lambda l, j: (l, 0, j)),
                  pl.BlockSpec((None, 1, tn), lambda l, j: (l, 0, j))],
        out_specs=pl.BlockSpec((None, rows, tn), lambda l, j: (l, 0, j)),
        out_shape=jax.ShapeDtypeStruct((depth, rows, n), F32),
        compiler_params=_params(2),
        name="adaln_mod",
    )(c_all, w_ada, b_ada.reshape(depth, 1, n))


def _modulated_norm(x, w, mod, which, d_model):
    shift = mod[:, 3 * which * d_model:(3 * which + 1) * d_model]
    scale = mod[:, (3 * which + 1) * d_model:(3 * which + 2) * d_model]
    return _rms(x) * w * (1.0 + scale) + shift


def _inproj_kernel(x_ref, mod_ref, n1w_ref, w_ref, qw_ref, kw_ref,
                   *refs, tm, d_model, dils, state_rows, tiles_per_seq, deinterleave, n_aliased):
    refs = refs[n_aliased:]
    if deinterleave:
        q_refs, k_refs, v_refs, st_refs = refs[0:9:3], refs[1:9:3], refs[2:9:3], refs[9:12]
        h_scr, slab, slab2, kv_stash = refs[12:]
    else:
        q_refs, st_refs = refs[0:3], refs[3:6]
        h_scr, slab, slab2, kv_stash = refs[6:]
    ti = pl.program_id(0) % tiles_per_seq
    tile_rows = 2 * HEADS

    h = _modulated_norm(x_ref[...], n1w_ref[...], mod_ref[0], 0, d_model)
    h_scr[...] = h.astype(BF16)

    def proj(c0, width):
        return _dot(h_scr[...], w_ref[:, c0:c0 + width])

    def put_tiles(ref, first_row, val):
        for hh in range(HEADS):
            ref[pl.ds(first_row + hh, tm, stride=tile_rows), :] = val[:, hh * HEAD_DIM:(hh + 1) * HEAD_DIM]

    def put(ref, g, val, slot, slabs=None):
        d = dils[g]
        if d == 1:
            ref[...] = val.astype(ref.dtype)
            return
        if slabs is None:
            slabs = [slab.at[g - 1, hh] for hh in range(HEADS)]
            for hh in range(HEADS):
                slabs[hh][...] = val[:, hh * HEAD_DIM:(hh + 1) * HEAD_DIM]
        if d <= RELAYOUT_STRIDE:
            for r in range(d):
                for hh in range(HEADS):
                    ref[0, r, :, hh * HEAD_DIM:(hh + 1) * HEAD_DIM] = (
                        slabs[hh][pl.ds(r, tm // d, stride=d), :].astype(ref.dtype))
            return
        s1, s2 = RELAYOUT_STRIDE, d // RELAYOUT_STRIDE
        part = tm // s1
        for hh in range(HEADS):
            for r1 in range(s1):
                slab2[slot, hh,r1 * part:(r1 + 1) * part, :] = slabs[hh][pl.ds(r1, part, stride=s1), :]
            for r1 in range(s1):
                for r2 in range(s2):
                    ref[0, r1 + s1 * r2, :, hh * HEAD_DIM:(hh + 1) * HEAD_DIM] = (
                        slab2[slot, hh,pl.ds(r1 * part + r2, tm // d, stride=s2), :].astype(ref.dtype))

    def head_rms(z, w):
        parts = [_rms(z[:, hh * HEAD_DIM:(hh + 1) * HEAD_DIM]) for hh in range(HEADS)]
        return jnp.concatenate(parts, axis=-1) * w

    def put_state(g, kn, v):
        if state_rows[g] is None:
            put_tiles(st_refs[g], 0, kn)
            put_tiles(st_refs[g], HEADS, v)
        else:
            for hh in range(HEADS):
                cols = slice(hh * HEAD_DIM, (hh + 1) * HEAD_DIM)
                kv_stash[g, hh] = kn[:, cols]
                kv_stash[g, HEADS + hh] = v[:, cols]

    def flush_state(g):
        st, rows = st_refs[g], min(state_rows[g], tm)
        first = tiles_per_seq - max(state_rows[g] // tm, 1)

        @pl.when(ti >= first)
        def _():
            for j in range(2 * HEADS):
                st[pl.ds(j, rows, stride=2 * HEADS), :] = kv_stash[g, j, tm - rows:, :]

    for g in reversed(range(N_GROUPS)):
        c0 = g * GROUP_COLS
        q = head_rms(proj(c0, GROUP_COLS), qw_ref[:, c0:c0 + GROUP_COLS] * QK_SCALE)
        kn = head_rms(proj(QKV_COLS + c0, GROUP_COLS), kw_ref[:, c0:c0 + GROUP_COLS])
        v = proj(2 * QKV_COLS + c0, GROUP_COLS)
        put_state(g, kn, v)
        if not deinterleave:
            put_tiles(q_refs[g], 0, q)
            put_tiles(q_refs[g], HEADS, jnp.zeros_like(q))
            continue
        put(q_refs[g], g, q, 0)
        put(k_refs[g], g, kn, 1, [kv_stash.at[g, hh] for hh in range(HEADS)])
        put(v_refs[g], g, v, 2, [kv_stash.at[g, HEADS + hh] for hh in range(HEADS)])

    for g in range(N_GROUPS):
        if state_rows[g] is not None:
            flush_state(g)


def _inproj_call(x, mod, n1w, w_in, qw, kw, *, seq, tm, deinterleave, act_dtype,
                 layer=0, depth=1, prev_states=None):
    t, d_model = x.shape
    batch = t // seq
    tiles_per_seq = seq // tm

    def tok(cols, dtype):
        return (jax.ShapeDtypeStruct((t, cols), dtype), pl.BlockSpec((tm, cols), lambda i: (i, 0)))

    def grp(g, dtype):
        d = DILATIONS[g]
        if d == 1:
            return tok(GROUP_COLS, dtype)
        assert tm % d == 0 and (tm // d) % 16 == 0
        return (jax.ShapeDtypeStruct((batch, d, seq // d, GROUP_COLS), dtype),
                pl.BlockSpec((1, d, tm // d, GROUP_COLS),
                             lambda i: (i // tiles_per_seq, 0, i % tiles_per_seq, 0)))

    tile_rows = 2 * HEADS
    outs = []
    for g in range(N_GROUPS):
        if deinterleave:
            outs += [grp(g, act_dtype)] * 3
        else:
            assert t == tm
            outs.append((jax.ShapeDtypeStruct((t * tile_rows, LANES), F32),
                         pl.BlockSpec((t * tile_rows, LANES), lambda i: (0, 0))))

    state_rows = []
    for g in range(N_GROUPS):
        if not deinterleave:
            state_rows.append(None)
            outs.append((jax.ShapeDtypeStruct((depth * t * tile_rows, LANES), F32),
                         pl.BlockSpec((t * tile_rows, LANES), lambda i: (layer, 0))))
            continue
        rows = min(WINDOWS[g], seq)
        state_rows.append(rows)
        blk_rows = min(rows, tm)
        assert rows % blk_rows == 0
        nblk = rows // blk_rows
        shape = jax.ShapeDtypeStruct((depth * batch * rows * 2 * HEADS, LANES), F32)
        spec = pl.BlockSpec(
            (blk_rows * 2 * HEADS, LANES),
            lambda i, nblk=nblk: ((layer * batch + i // tiles_per_seq) * nblk
                                  + jnp.maximum(i % tiles_per_seq - (tiles_per_seq - nblk), 0), 0))
        outs.append((shape, spec))

    aliased = list(prev_states) if prev_states is not None else []
    n_in = 6
    w_qkv = pl.BlockSpec((d_model, 3 * QKV_COLS), lambda i: (0, 0),
                         pipeline_mode=pl.Buffered(1))
    kern = functools.partial(_inproj_kernel, tm=tm, d_model=d_model, dils=DILATIONS,
                             state_rows=tuple(state_rows), tiles_per_seq=tiles_per_seq,
                             deinterleave=deinterleave, n_aliased=len(aliased))
    return pl.pallas_call(
        kern,
        grid=(t // tm,),
        in_specs=[pl.BlockSpec((tm, d_model), lambda i: (i, 0)),
                  pl.BlockSpec((1,) + mod.shape[1:], lambda i: (i // tiles_per_seq, 0, 0)),
                  _resident((1, d_model)),
                  w_qkv,
                  _resident((1, QKV_COLS)), _resident((1, QKV_COLS))]
                 + [pl.BlockSpec(memory_space=pl.ANY)] * len(aliased),
        out_specs=[o[1] for o in outs],
        out_shape=[o[0] for o in outs],
        input_output_aliases={n_in + g: len(outs) - N_GROUPS + g for g in range(len(aliased))},
        scratch_shapes=[pltpu.VMEM((tm, d_model), BF16),
                        pltpu.VMEM((N_GROUPS - 1, HEADS, tm, LANES), F32),
                        pltpu.VMEM((3, HEADS, tm, LANES), F32),
                        pltpu.VMEM((N_GROUPS, 2 * HEADS, tm, LANES), F32)],
        compiler_params=_params(1),
        name="inproj_deint" if deinterleave else "inproj_tok",
    )(x, mod, n1w, w_in, qw, kw, *aliased)


def _alibi_slopes():
    h = np.arange(1, N_GROUPS * HEADS + 1, dtype=np.float32)
    s = np.power(np.float32(2.0), -8.0 * h / (N_GROUPS * HEADS)).astype(np.float32)
    return (s.astype(np.float64) * LOG2E).astype(np.float32).reshape(N_GROUPS, HEADS)


def _band_bias():
    qi = np.arange(N_STEPS)[:, None]
    ki = np.arange(2 * N_STEPS)[None, :]
    dist = N_STEPS + qi - ki
    valid = (dist >= 0) & (dist <= N_STEPS)
    slopes = _alibi_slopes()
    out = np.empty((N_GROUPS, HEADS, 2, N_STEPS, 2 * N_STEPS), np.float32)
    for g in range(N_GROUPS):
        for hh in range(HEADS):
            b = np.where(valid, -slopes[g, hh] * (dist * DILATIONS[g]).astype(np.float32),
                         np.float32(NEG_INF)).astype(np.float32)
            out[g, hh, 1] = b
            first = b.copy()
            first[:, :N_STEPS] = NEG_INF
            out[g, hh, 0] = first
    return out


def _attn_block(q, k2, v2, bias):
    s = _dot_nt(q, k2) + bias
    m = jnp.max(s, axis=-1, keepdims=True)
    p = jnp.exp2(s - m)
    l = jnp.sum(p, axis=-1, keepdims=True)
    return _dot(p.astype(v2.dtype), v2), m, l


ATTN_UNROLL = 32


def _attn_kernel(q0, k0, v0, q1, k1, v1, q2, k2, v2, bias_ref, *refs, seq, n_cast):
    cast_src, refs = refs[:n_cast], refs[n_cast:]
    o_ref, cast_dst = refs[0], refs[1:1 + n_cast]
    o_scr, m_scr, l_scr, stage_scr = refs[1 + n_cast:]
    blk = N_STEPS
    s1 = RELAYOUT_STRIDE

    for src, dst in zip(cast_src, cast_dst):
        dst[...] = src[...].astype(dst.dtype)

    def run_group(g, qr, kr, vr):
        d = DILATIONS[g]
        nblk = seq // d // blk

        def one_block(r, i):
            own = pl.multiple_of(i * blk, blk)
            prev = pl.multiple_of(jnp.maximum(i - 1, 0) * blk, blk)
            if d == 1:
                ld = lambda ref, start: ref[pl.ds(start, blk), :]
            else:
                ld = lambda ref, start: ref[0, r, pl.ds(start, blk), :]
            k2_ = jnp.concatenate([ld(kr, prev), ld(kr, own)], axis=0)
            v2_ = jnp.concatenate([ld(vr, prev), ld(vr, own)], axis=0)
            o, m, l = _attn_block(ld(qr, own), k2_, v2_, bias_ref[g, 0, jnp.minimum(i, 1)])
            m = jnp.broadcast_to(m, (blk, LANES))
            l = jnp.broadcast_to(l, (blk, LANES))
            if 1 < d <= s1:
                rows = pl.ds(i * (blk * d) + r, blk, stride=d)
                o_scr[g - 1, rows, :] = o
                m_scr[g - 1, rows, :] = m
                l_scr[g - 1, rows, :] = l
                return
            if d > s1:
                s2 = d // s1
                rows = pl.ds((r % s1) * (seq // s1) + i * (blk * s2) + r // s1, blk, stride=s2)
                stage_scr[0, rows, :] = o
                stage_scr[1, rows, :] = m
                stage_scr[2, rows, :] = l
                return
            rows = pl.ds(own, blk)
            ms = [m, m_scr[0, rows, :], m_scr[1, rows, :]]
            mx = jnp.maximum(jnp.maximum(ms[0], ms[1]), ms[2])
            ws = [jnp.exp2(x - mx) for x in ms]
            num = ws[0] * o + ws[1] * o_scr[0, rows, :] + ws[2] * o_scr[1, rows, :]
            den = ws[0] * l + ws[1] * l_scr[0, rows, :] + ws[2] * l_scr[1, rows, :]
            o_ref[rows, :] = (num / den).astype(o_ref.dtype)

        ur = min(ATTN_UNROLL, d)
        ui = ATTN_UNROLL // ur
        assert d % ur == 0 and nblk % ui == 0

        def body(it, carry):
            i0 = (it // (d // ur)) * ui
            r0 = (it % (d // ur)) * ur
            for a in range(ui):
                for b in range(ur):
                    one_block(r0 + b, i0 + a)
            return carry

        jax.lax.fori_loop(0, (d // ur) * (nblk // ui), body, 0)

    run_group(1, q1, k1, v1)
    run_group(2, q2, k2, v2)
    g = N_GROUPS - 1
    assert DILATIONS[g] > s1 and all(d <= s1 for d in DILATIONS[:g])
    per_stream = seq // s1
    for a, dst in enumerate((o_scr, m_scr, l_scr)):
        for r1 in range(s1):
            for c in range(per_stream // blk):
                dst[g - 1, pl.ds(s1 * c * blk + r1, blk, stride=s1), :] = (
                    stage_scr[a, r1 * per_stream + c * blk:r1 * per_stream + (c + 1) * blk, :])
    run_group(0, q0, k0, v0)


def _attn_call(qkv, bias, casts, *, batch, seq):
    n_steps = batch * HEADS
    cast_in, cast_out, cast_shape = [], [], []
    for w, layer in casts:
        _, k_dim, n_dim = w.shape
        band = k_dim // n_steps
        assert band * n_steps == k_dim and band % 16 == 0
        cast_in.append(pl.BlockSpec((None, band, n_dim),
                                    lambda b, h, layer=layer: (layer, b * HEADS + h, 0)))
        cast_out.append(pl.BlockSpec((band, n_dim), lambda b, h: (b * HEADS + h, 0)))
        cast_shape.append(jax.ShapeDtypeStruct((k_dim, n_dim), BF16))
    in_specs = []
    for g in range(N_GROUPS):
        d = DILATIONS[g]
        if d == 1:
            spec = pl.BlockSpec((seq, HEAD_DIM), lambda b, h: (b, h))
        else:
            spec = pl.BlockSpec((1, d, seq // d, HEAD_DIM), lambda b, h: (b, 0, 0, h))
        in_specs += [spec] * 3
    in_specs.append(pl.BlockSpec((N_GROUPS, 1, 2, N_STEPS, 2 * N_STEPS), lambda b, h: (0, h, 0, 0, 0)))
    return pl.pallas_call(
        functools.partial(_attn_kernel, seq=seq, n_cast=len(casts)),
        grid=(batch, HEADS),
        in_specs=in_specs + cast_in,
        out_specs=[pl.BlockSpec((seq, HEAD_DIM), lambda b, h: (b, h))] + cast_out,
        out_shape=[jax.ShapeDtypeStruct((batch * seq, GROUP_COLS), BF16)] + cast_shape,
        scratch_shapes=[pltpu.VMEM((N_GROUPS - 1, seq, LANES), F32)] * 3
                       + [pltpu.VMEM((3, seq, LANES), F32)],
        compiler_params=_params(2),
        name="attn_prompt",
    )(*qkv, bias, *[w for w, _ in casts])


def _attn_sample_kernel(q0, q1, q2, n0, n1, n2, c0, c1, c2, slope_ref, o_ref, *, bb):
    q_refs, n_refs, c_refs = (q0, q1, q2), (n0, n1, n2), (c0, c1, c2)
    half = N_STEPS // 2
    tile = (2 * HEADS, HEAD_DIM)
    top = jax.lax.broadcasted_iota(jnp.int32, tile, 0) < HEADS
    steps_a = (N_STEPS - jax.lax.broadcasted_iota(jnp.int32, (half, 1, 1), 0)).astype(F32)
    back = jnp.where(top[:, 0:1], steps_a, steps_a - float(half))

    def both_halves(t):
        return jnp.where(top, t, pltpu.roll(t, HEADS, axis=0))

    for b in range(bb):
        outs, lses = [], []
        for g in range(N_GROUPS):
            q8, kv8 = q_refs[g][b], n_refs[g][b]
            xa, xb = c_refs[g][b, 0:half], c_refs[g][b, half:N_STEPS]
            slope = both_halves(slope_ref[g])[:, 0:1]
            bias = (back * (-float(DILATIONS[g]))) * slope
            prod = jnp.where(top, xa * q8, pltpu.roll(xb * q8, HEADS, axis=1))
            s = jnp.sum(prod, axis=-1, keepdims=True) + bias
            s_self = both_halves(jnp.broadcast_to(jnp.sum(kv8 * q8, axis=-1, keepdims=True), tile))
            m = jnp.broadcast_to(jnp.max(s, axis=0), tile)
            m = jnp.maximum(jnp.maximum(m, pltpu.roll(m, HEADS, axis=0)), s_self)
            p = jnp.exp2(s - m[:, 0:1])
            p_self = jnp.exp2(s_self - m)
            l = jnp.broadcast_to(jnp.sum(p, axis=0), tile)
            l = l + pltpu.roll(l, HEADS, axis=0) + p_self
            p_b = jnp.broadcast_to(p, xb.shape)
            acc = jnp.sum(pltpu.roll(p_b, HEADS, axis=1) * xa + p_b * xb, axis=0) + p_self * kv8
            outs.append(acc / l)
            lses.append(m + jnp.log(l) * LOG2E)
        mx = jnp.maximum(jnp.maximum(lses[0], lses[1]), lses[2])
        ws = [jnp.exp2(v - mx) for v in lses]
        num = ws[0] * outs[0] + ws[1] * outs[1] + ws[2] * outs[2]
        o8 = num / (ws[0] + ws[1] + ws[2])
        for hh in range(HEADS):
            o_ref[b:b + 1, hh * HEAD_DIM:(hh + 1) * HEAD_DIM] = o8[HEADS + hh:HEADS + hh + 1, :]


def _attn_sample_call(q_tiles, new_tiles, caches, layer, slopes8):
    bd = q_tiles[0].shape[0]
    bb = 8
    nb = bd // bb
    tile = (2 * HEADS, HEAD_DIM)
    in_specs = [pl.BlockSpec((bb,) + tile, lambda i: (i, 0, 0))] * 3
    in_specs += [pl.BlockSpec((bb,) + tile, lambda i: (layer * nb + i, 0, 0))] * 3
    in_specs += [pl.BlockSpec((None, bb, N_STEPS) + tile, lambda i: (layer, i, 0, 0, 0))] * 3
    in_specs.append(pl.BlockSpec((N_GROUPS,) + tile, lambda i: (0, 0, 0)))
    return pl.pallas_call(
        functools.partial(_attn_sample_kernel, bb=bb),
        grid=(nb,),
        in_specs=in_specs,
        out_specs=pl.BlockSpec((bb, GROUP_COLS), lambda i: (i, 0)),
        out_shape=jax.ShapeDtypeStruct((bd, GROUP_COLS), F32),
        compiler_params=_params(1),
        name="attn_sample",
    )(*q_tiles, *new_tiles, *caches, slopes8)


def _post_kernel(x_ref, mod_ref, ya_ref, n1w_ref, wg1_ref, wg2_ref, lnw_ref, lnb_ref, ws_ref, bs_ref,
                 wpa_ref, wps_ref, wo_ref, n2w_ref, wfi_ref, wfo_ref, *refs,
                 tm, d_model, d_ff, chunked):
    if chunked:
        o_ref, mix_scr = refs
    else:
        o_ref, vs_out_ref, mix_scr = refs
    mod = mod_ref[0]
    g1 = mod[:, 2 * d_model:3 * d_model]
    g2 = mod[:, 5 * d_model:6 * d_model]
    x = x_ref[...]

    h = _modulated_norm(x, n1w_ref[...], mod, 0, d_model).astype(BF16)
    w = SGU_WIDTH
    u = _gelu(_dot(h, wg1_ref[:, 0:w]))
    vs = _gelu(_dot(h, wg1_ref[:, w:2 * w]))
    mu = jnp.mean(vs, axis=-1, keepdims=True)
    vc = vs - mu
    var = jnp.mean(vc * vc, axis=-1, keepdims=True)
    vs = vc * jax.lax.rsqrt(var + EPS) * lnw_ref[...] + lnb_ref[...]
    half = d_model // 2
    ga = jnp.concatenate([_sigmoid(_dot(h, wg1_ref[:, 2 * w:2 * w + half])),
                          _sigmoid(_dot(h, wg2_ref[:, 0:half]))], axis=-1)
    gb = _sigmoid(_dot(h, wg2_ref[:, half:half + d_model]))

    if chunked:
        row = jax.lax.broadcasted_iota(jnp.int32, (CHUNK, CHUNK), 0)
        col = jax.lax.broadcasted_iota(jnp.int32, (CHUNK, CHUNK), 1)
        tril = col <= row
        lane = jax.lax.broadcasted_iota(jnp.int32, (CHUNK, LANES), 1)
        low = lane < SGU_GROUP_DIM
        wms = [jnp.where(tril, ws_ref[gi], 0.0).astype(BF16) for gi in range(SGU_GROUPS)]
        vs_b = vs.astype(BF16)
        for c in range(tm // CHUNK):
            rows = slice(c * CHUNK, (c + 1) * CHUNK)
            for j in range(SGU_WIDTH // LANES):
                cols = slice(j * LANES, (j + 1) * LANES)
                v = vs_b[rows, cols]
                mix_scr[rows, cols] = jnp.where(low, _dot(wms[2 * j], v), _dot(wms[2 * j + 1], v))
        mixed = mix_scr[...] + jnp.concatenate([bs_ref[...]] * (tm // CHUNK), axis=0)
    else:
        vs_out_ref[...] = vs
        mixed = vs * ws_ref[...] + bs_ref[...]

    ys = (u * mixed).astype(BF16)
    xa = _dot(ya_ref[...].astype(BF16), wpa_ref[...])
    xs = _dot(ys, wps_ref[...])
    merged = (ga * xa + gb * xs).astype(BF16)
    x1 = x + g1 * _dot(merged, wo_ref[...])
    h2 = _modulated_norm(x1, n2w_ref[...], mod, 1, d_model).astype(BF16)
    a = _dot(h2, wfi_ref[:, 0:d_ff])
    b = _dot(h2, wfi_ref[:, d_ff:2 * d_ff])
    act = (a * _sigmoid(a) * b).astype(BF16)
    o_ref[...] = x1 + g2 * _dot(act, wfo_ref[...])


def _post_call(x, mod, ya, n1w, w_in, lnw, lnb, ws, bs, wpa, wps, wo, n2w, wfi, wfo, *,
               seq, tm, chunked, layer):
    t, d_model = x.shape
    d_ff = wfo.shape[0]
    tiles_per_seq = seq // tm
    gblk = QKV_COLS
    assert 2 * SGU_WIDTH + d_model // 2 == gblk and w_in.shape[1] == 5 * gblk

    def w_gate(j):
        return pl.BlockSpec((d_model, gblk), lambda i: (0, 3 + j), pipeline_mode=pl.Buffered(1))

    def tok(cols):
        return pl.BlockSpec((tm, cols), lambda i: (i, 0))

    out_specs, out_shape = [tok(d_model)], [jax.ShapeDtypeStruct((t, d_model), F32)]
    if not chunked:
        out_specs.append(tok(SGU_WIDTH))
        out_shape.append(jax.ShapeDtypeStruct((t, SGU_WIDTH), F32))
    kern = functools.partial(_post_kernel, tm=tm, d_model=d_model, d_ff=d_ff, chunked=chunked)
    return pl.pallas_call(
        kern,
        grid=(t // tm,),
        in_specs=[tok(d_model),
                  pl.BlockSpec((1,) + mod.shape[1:], lambda i: (i // tiles_per_seq, 0, 0)),
                  tok(GROUP_COLS), _resident(n1w.shape), w_gate(0), w_gate(1),
                  _resident(lnw.shape), _resident(lnb.shape),
                  _resident(ws.shape), _resident(bs.shape),
                  _resident_layer(wpa.shape, layer), _resident_layer(wps.shape, layer),
                  _resident_layer(wo.shape, layer), _resident(n2w.shape),
                  _resident(wfi.shape), _resident(wfo.shape)],
        out_specs=out_specs,
        out_shape=out_shape,
        scratch_shapes=[pltpu.VMEM((tm, SGU_WIDTH), F32)],
        compiler_params=_params(1),
        name="post_chunked" if chunked else "post_tok",
    )(x, mod, ya, n1w, w_in, w_in, lnw, lnb, ws, bs, wpa, wps, wo, n2w, wfi, wfo)


def kernel(x_prompt, x_sample, cache_kv_w128, cache_kv_w512, cache_kv_w2048, c_prompt, c_sample,
           w_ada, b_ada, norm1_w, w_in, q_norm_w, k_norm_w, sgu_ln_w, sgu_ln_b, w_spatial, b_spatial,
           w_proj_att, w_proj_sgu, w_out, norm2_w, w_ffn_in, w_ffn_out):
    batch, seq, d_model = x_prompt.shape
    bd, dec_seq, _ = x_sample.shape
    depth = w_in.shape[0]
    assert dec_seq == 1 and seq % (N_STEPS * DILATIONS[-1]) == 0
    caches_in = (cache_kv_w128, cache_kv_w512, cache_kv_w2048)
    for g in range(N_GROUPS):
        assert caches_in[g].shape[2] == WINDOWS[g]
    tm = 512
    tm_in = 512

    n_c = batch + bd
    c_all = jnp.concatenate([c_prompt, c_sample], axis=0)
    c_all = jnp.pad(c_all, ((0, -n_c % 16), (0, 0)))
    mod = _ada_call(c_all, w_ada, b_ada)

    band_bias = jnp.asarray(_band_bias())
    tile = (2 * HEADS, HEAD_DIM)
    slopes8 = np.zeros((N_GROUPS,) + tile, np.float32)
    slopes8[:, :HEADS, :] = _alibi_slopes()[:, :, None]
    slopes8 = jnp.asarray(slopes8)
    caches = [caches_in[g].reshape((depth, bd, N_STEPS, DILATIONS[g] * tile[0], tile[1]))
              for g in range(N_GROUPS)]

    xp = x_prompt.reshape(batch * seq, d_model)
    xs = x_sample.reshape(bd, d_model)
    p_states = s_states = None
    s_v = []
    wpa, wps, wo = w_proj_att.astype(BF16), w_proj_sgu.astype(BF16), w_out.astype(BF16)
    w_in_l = w_in[0].astype(BF16)
    for l in range(depth):
        qw = jnp.tile(q_norm_w[l][:, None, :], (1, HEADS, 1)).reshape(1, QKV_COLS)
        kw = jnp.tile(k_norm_w[l][:, None, :], (1, HEADS, 1)).reshape(1, QKV_COLS)
        lnw, lnb = sgu_ln_w[l].reshape(1, -1), sgu_ln_b[l].reshape(1, -1)
        n1w, n2w = norm1_w[l].reshape(1, -1), norm2_w[l].reshape(1, -1)
        mod_p = mod[l, :batch].reshape(batch, 1, -1)
        mod_s = mod[l, batch:n_c].reshape(1, bd, -1)
        bs_p = jnp.repeat(b_spatial[l].T, SGU_GROUP_DIM, axis=1)
        ws_s = jnp.repeat(w_spatial[l][:, 0, 0], SGU_GROUP_DIM).reshape(1, SGU_WIDTH)
        bs_s = jnp.repeat(b_spatial[l][:, 0], SGU_GROUP_DIM).reshape(1, SGU_WIDTH)

        outs = _inproj_call(xp, mod_p, n1w, w_in_l, qw, kw,
                            seq=seq, tm=tm_in, deinterleave=True, act_dtype=BF16,
                            layer=l, depth=depth, prev_states=p_states)
        qkv, p_states = outs[:9], outs[9:]
        casts = [(w_ffn_in, l), (w_ffn_out, l)] + ([(w_in, l + 1)] if l + 1 < depth else [])
        ya, wfi, wfo, *w_in_next = _attn_call(qkv, band_bias, casts, batch=batch, seq=seq)
        (xp,) = _post_call(xp, mod_p, ya, n1w, w_in_l, lnw, lnb, w_spatial[l], bs_p,
                           wpa, wps, wo, n2w, wfi, wfo, seq=seq, tm=tm, chunked=True, layer=l)

        outs = _inproj_call(xs, mod_s, n1w, w_in_l, qw, kw,
                            seq=bd, tm=bd, deinterleave=False, act_dtype=F32,
                            layer=l, depth=depth, prev_states=s_states)
        q_tiles, s_states = [o.reshape((bd,) + tile) for o in outs[:3]], outs[3:]
        new_tiles = [s.reshape((depth * bd,) + tile) for s in s_states]
        ya = _attn_sample_call(q_tiles, new_tiles, caches, l, slopes8)
        xs, vs = _post_call(xs, mod_s, ya, n1w, w_in_l, lnw, lnb, ws_s, bs_s,
                            wpa, wps, wo, n2w, wfi, wfo, seq=bd, tm=bd, chunked=False, layer=l)
        s_v.append(vs.reshape(bd, 1, SGU_WIDTH))
        if w_in_next:
            (w_in_l,) = w_in_next

    p_states =[s.reshape(depth, batch, -1, 2, HEADS, HEAD_DIM) for s in p_states]
    s_states = [s.reshape(depth, bd, 1, 2, HEADS, HEAD_DIM) for s in s_states]
    return (xp.reshape(batch, seq, d_model), xs.reshape(bd, 1, d_model),
            p_states[0], p_states[1], p_states[2], s_states[0], s_states[1], s_states[2],
            jnp.stack(s_v))
```

```python
import functools
import math

import jax
import jax.numpy as jnp
import numpy as np
from jax.experimental import pallas as pl
from jax.experimental.pallas import tpu as pltpu

HEAD_DIM = 128
N_GROUPS = 3
HEADS = 4
GROUP_COLS = HEADS * HEAD_DIM
QKV_COLS = N_GROUPS * GROUP_COLS
WINDOWS = (128, 512, 2048)
DILATIONS = (1, 4, 16)
N_STEPS = 128
CHUNK = 128
SGU_WIDTH = 512
SGU_GROUPS = 8
SGU_GROUP_DIM = SGU_WIDTH // SGU_GROUPS
N_ADA = 6
EPS = 1e-6
NEG_INF = -1e30
LOG2E = math.log2(math.e)
QK_SCALE = HEAD_DIM ** -0.5 * LOG2E

LANES = 128
RELAYOUT_STRIDE = 4
VMEM_LIMIT_BYTES = 56 * 1024 * 1024

BF16 = jnp.bfloat16
F32 = jnp.float32


def _params(n_axes, flags=None):
    return pltpu.CompilerParams(dimension_semantics=("arbitrary",) * n_axes,
                                vmem_limit_bytes=VMEM_LIMIT_BYTES, flags=flags)


def _resident(shape):
    nd = len(shape)
    return pl.BlockSpec(shape, lambda *_: (0,) * nd, pipeline_mode=pl.Buffered(1))


def _resident_layer(shape, layer):
    nd = len(shape)
    return pl.BlockSpec((None,) + tuple(shape[1:]), lambda *_: (layer,) + (0,) * (nd - 1),
                        pipeline_mode=pl.Buffered(1))


def _dot(a, b):
    return jnp.dot(a, b, preferred_element_type=F32)


def _dot_nt(a, b):
    return jax.lax.dot_general(a, b, (((1,), (1,)), ((), ())), preferred_element_type=F32)


def _rms(x):
    return x * jax.lax.rsqrt(jnp.mean(x * x, axis=-1, keepdims=True) + EPS)


def _gelu(x):
    return 0.5 * x * (1.0 + jax.lax.erf(x * (2.0 ** -0.5)))


def _sigmoid(x):
    return 1.0 / (1.0 + jnp.exp(-x))


def _ada_kernel(c_ref, w_ref, b_ref, o_ref):
    c = c_ref[...]
    a = c * _sigmoid(c)
    w = w_ref[...]
    a_hi = a.astype(BF16)
    a_lo = (a - a_hi.astype(F32)).astype(BF16)
    w_hi = w.astype(BF16)
    w_lo = (w - w_hi.astype(F32)).astype(BF16)
    rows = a.shape[0]
    both = _dot(jnp.concatenate([a_hi, a_lo], axis=0), w_hi)
    o_ref[...] = both[:rows] + both[rows:] + _dot(a_hi, w_lo) + b_ref[...]


def _ada_call(c_all, w_ada, b_ada):
    depth, d, n = w_ada.shape
    rows = c_all.shape[0]
    tn = 2048
    return pl.pallas_call(
        _ada_kernel,
        grid=(depth, n // tn),
        in_specs=[pl.BlockSpec((rows, d), lambda l, j: (0, 0)),
                  pl.BlockSpec((None, d, tn), lambda l, j: (l, 0, j)),
                  pl.BlockSpec((None, 1, tn), lambda l, j: (l, 0, j))],
        out_specs=pl.BlockSpec((None, rows, tn), lambda l, j: (l, 0, j)),
        out_shape=jax.ShapeDtypeStruct((depth, rows, n), F32),
        compiler_params=_params(2),
        name="adaln_mod",
    )(c_all, w_ada, b_ada.reshape(depth, 1, n))


def _modulated_norm(x, w, mod, which, d_model):
    shift = mod[:, 3 * which * d_model:(3 * which + 1) * d_model]
    scale = mod[:, (3 * which + 1) * d_model:(3 * which + 2) * d_model]
    return _rms(x) * w * (1.0 + scale) + shift


def _inproj_kernel(x_ref, mod_ref, n1w_ref, w_ref, qw_ref, kw_ref,
                   *refs, tm, d_model, dils, state_rows, tiles_per_seq, deinterleave, n_aliased):
    refs = refs[n_aliased:]
    if deinterleave:
        q_refs, k_refs, v_refs, st_refs = refs[0:9:3], refs[1:9:3], refs[2:9:3], refs[9:12]
        h_scr, slab, slab2, kv_stash = refs[12:]
    else:
        q_refs, st_refs = refs[0:3], refs[3:6]
        h_scr, slab, slab2, kv_stash = refs[6:]
    ti = pl.program_id(0) % tiles_per_seq
    tile_rows = 2 * HEADS

    h = _modulated_norm(x_ref[...], n1w_ref[...], mod_ref[0], 0, d_model)
    h_scr[...] = h.astype(BF16)

    def proj(c0, width):
        return _dot(h_scr[...], w_ref[:, c0:c0 + width])

    def put_tiles(ref, first_row, val):
        for hh in range(HEADS):
            ref[pl.ds(first_row + hh, tm, stride=tile_rows), :] = val[:, hh * HEAD_DIM:(hh + 1) * HEAD_DIM]

    def put(ref, g, val, slot, slabs=None):
        d = dils[g]
        if d == 1:
            ref[...] = val.astype(ref.dtype)
            return
        if slabs is None:
            slabs = [slab.at[g - 1, hh] for hh in range(HEADS)]
            for hh in range(HEADS):
                slabs[hh][...] = val[:, hh * HEAD_DIM:(hh + 1) * HEAD_DIM]
        if d <= RELAYOUT_STRIDE:
            for r in range(d):
                for hh in range(HEADS):
                    ref[0, r, :, hh * HEAD_DIM:(hh + 1) * HEAD_DIM] = (
                        slabs[hh][pl.ds(r, tm // d, stride=d), :].astype(ref.dtype))
            return
        s1, s2 = RELAYOUT_STRIDE, d // RELAYOUT_STRIDE
        part = tm // s1
        for hh in range(HEADS):
            for r1 in range(s1):
                slab2[slot, hh,r1 * part:(r1 + 1) * part, :] = slabs[hh][pl.ds(r1, part, stride=s1), :]
            for r1 in range(s1):
                for r2 in range(s2):
                    ref[0, r1 + s1 * r2, :, hh * HEAD_DIM:(hh + 1) * HEAD_DIM] = (
                        slab2[slot, hh,pl.ds(r1 * part + r2, tm // d, stride=s2), :].astype(ref.dtype))

    def head_rms(z, w):
        parts = [_rms(z[:, hh * HEAD_DIM:(hh + 1) * HEAD_DIM]) for hh in range(HEADS)]
        return jnp.concatenate(parts, axis=-1) * w

    def put_state(g, kn, v):
        if state_rows[g] is None:
            put_tiles(st_refs[g], 0, kn)
            put_tiles(st_refs[g], HEADS, v)
        else:
            for hh in range(HEADS):
                cols = slice(hh * HEAD_DIM, (hh + 1) * HEAD_DIM)
                kv_stash[g, hh] = kn[:, cols]
                kv_stash[g, HEADS + hh] = v[:, cols]

    def flush_state(g):
        st, rows = st_refs[g], min(state_rows[g], tm)
        first = tiles_per_seq - max(state_rows[g] // tm, 1)

        @pl.when(ti >= first)
        def _():
            for j in range(2 * HEADS):
                st[pl.ds(j, rows, stride=2 * HEADS), :] = kv_stash[g, j, tm - rows:, :]

    for g in reversed(range(N_GROUPS)):
        c0 = g * GROUP_COLS
        q = head_rms(proj(c0, GROUP_COLS), qw_ref[:, c0:c0 + GROUP_COLS] * QK_SCALE)
        kn = head_rms(proj(QKV_COLS + c0, GROUP_COLS), kw_ref[:, c0:c0 + GROUP_COLS])
        v = proj(2 * QKV_COLS + c0, GROUP_COLS)
        put_state(g, kn, v)
        if not deinterleave:
            put_tiles(q_refs[g], 0, q)
            put_tiles(q_refs[g], HEADS, jnp.zeros_like(q))
            continue
        put(q_refs[g], g, q, 0)
        put(k_refs[g], g, kn, 1, [kv_stash.at[g, hh] for hh in range(HEADS)])
        put(v_refs[g], g, v, 2, [kv_stash.at[g, HEADS + hh] for hh in range(HEADS)])

    for g in range(N_GROUPS):
        if state_rows[g] is not None:
            flush_state(g)


def _inproj_call(x, mod, n1w, w_in, qw, kw, *, seq, tm, deinterleave, act_dtype,
                 layer=0, depth=1, prev_states=None):
    t, d_model = x.shape
    batch = t // seq
    tiles_per_seq = seq // tm

    def tok(cols, dtype):
        return (jax.ShapeDtypeStruct((t, cols), dtype), pl.BlockSpec((tm, cols), lambda i: (i, 0)))

    def grp(g, dtype):
        d = DILATIONS[g]
        if d == 1:
            return tok(GROUP_COLS, dtype)
        assert tm % d == 0 and (tm // d) % 16 == 0
        return (jax.ShapeDtypeStruct((batch, d, seq // d, GROUP_COLS), dtype),
                pl.BlockSpec((1, d, tm // d, GROUP_COLS),
                             lambda i: (i // tiles_per_seq, 0, i % tiles_per_seq, 0)))

    tile_rows = 2 * HEADS
    outs = []
    for g in range(N_GROUPS):
        if deinterleave:
            outs += [grp(g, act_dtype)] * 3
        else:
            assert t == tm
            outs.append((jax.ShapeDtypeStruct((t * tile_rows, LANES), F32),
                         pl.BlockSpec((t * tile_rows, LANES), lambda i: (0, 0))))

    state_rows = []
    for g in range(N_GROUPS):
        if not deinterleave:
            state_rows.append(None)
            outs.append((jax.ShapeDtypeStruct((depth * t * tile_rows, LANES), F32),
                         pl.BlockSpec((t * tile_rows, LANES), lambda i: (layer, 0))))
            continue
        rows = min(WINDOWS[g], seq)
        state_rows.append(rows)
        blk_rows = min(rows, tm)
        assert rows % blk_rows == 0
        nblk = rows // blk_rows
        shape = jax.ShapeDtypeStruct((depth * batch * rows * 2 * HEADS, LANES), F32)
        spec = pl.BlockSpec(
            (blk_rows * 2 * HEADS, LANES),
            lambda i, nblk=nblk: ((layer * batch + i // tiles_per_seq) * nblk
                                  + jnp.maximum(i % tiles_per_seq - (tiles_per_seq - nblk), 0), 0))
        outs.append((shape, spec))

    aliased = list(prev_states) if prev_states is not None else []
    n_in = 6
    w_qkv = pl.BlockSpec((d_model, 3 * QKV_COLS), lambda i: (0, 0),
                         pipeline_mode=pl.Buffered(1))
    kern = functools.partial(_inproj_kernel, tm=tm, d_model=d_model, dils=DILATIONS,
                             state_rows=tuple(state_rows), tiles_per_seq=tiles_per_seq,
                             deinterleave=deinterleave, n_aliased=len(aliased))
    return pl.pallas_call(
        kern,
        grid=(t // tm,),
        in_specs=[pl.BlockSpec((tm, d_model), lambda i: (i, 0)),
                  pl.BlockSpec((1,) + mod.shape[1:], lambda i: (i // tiles_per_seq, 0, 0)),
                  _resident((1, d_model)),
                  w_qkv,
                  _resident((1, QKV_COLS)), _resident((1, QKV_COLS))]
                 + [pl.BlockSpec(memory_space=pl.ANY)] * len(aliased),
        out_specs=[o[1] for o in outs],
        out_shape=[o[0] for o in outs],
        input_output_aliases={n_in + g: len(outs) - N_GROUPS + g for g in range(len(aliased))},
        scratch_shapes=[pltpu.VMEM((tm, d_model), BF16),
                        pltpu.VMEM((N_GROUPS - 1, HEADS, tm, LANES), F32),
                        pltpu.VMEM((3, HEADS, tm, LANES), F32),
                        pltpu.VMEM((N_GROUPS, 2 * HEADS, tm, LANES), F32)],
        compiler_params=_params(1),
        name="inproj_deint" if deinterleave else "inproj_tok",
    )(x, mod, n1w, w_in, qw, kw, *aliased)


def _alibi_slopes():
    h = np.arange(1, N_GROUPS * HEADS + 1, dtype=np.float32)
    s = np.power(np.float32(2.0), -8.0 * h / (N_GROUPS * HEADS)).astype(np.float32)
    return (s.astype(np.float64) * LOG2E).astype(np.float32).reshape(N_GROUPS, HEADS)


def _band_bias():
    qi = np.arange(N_STEPS)[:, None]
    ki = np.arange(2 * N_STEPS)[None, :]
    dist = N_STEPS + qi - ki
    valid = (dist >= 0) & (dist <= N_STEPS)
    slopes = _alibi_slopes()
    out = np.empty((N_GROUPS, HEADS, 2, N_STEPS, 2 * N_STEPS), np.float32)
    for g in range(N_GROUPS):
        for hh in range(HEADS):
            b = np.where(valid, -slopes[g, hh] * (dist * DILATIONS[g]).astype(np.float32),
                         np.float32(NEG_INF)).astype(np.float32)
            out[g, hh, 1] = b
            first = b.copy()
            first[:, :N_STEPS] = NEG_INF
            out[g, hh, 0] = first
    return out


def _attn_block(q, k2, v2, bias):
    s = _dot_nt(q, k2) + bias
    m = jnp.max(s, axis=-1, keepdims=True)
    p = jnp.exp2(s - m)
    l = jnp.sum(p, axis=-1, keepdims=True)
    return _dot(p.astype(v2.dtype), v2), m, l


ATTN_UNROLL = 32


def _attn_kernel(q0, k0, v0, q1, k1, v1, q2, k2, v2, bias_ref, *refs, seq, n_cast):
    cast_src, refs = refs[:n_cast], refs[n_cast:]
    o_ref, cast_dst = refs[0], refs[1:1 + n_cast]
    o_scr, m_scr, l_scr, stage_scr = refs[1 + n_cast:]
    blk = N_STEPS
    s1 = RELAYOUT_STRIDE

    for src, dst in zip(cast_src, cast_dst):
        dst[...] = src[...].astype(dst.dtype)

    def run_group(g, qr, kr, vr):
        d = DILATIONS[g]
        nblk = seq // d // blk

        def one_block(r, i):
            own = pl.multiple_of(i * blk, blk)
            prev = pl.multiple_of(jnp.maximum(i - 1, 0) * blk, blk)
            if d == 1:
                ld = lambda ref, start: ref[pl.ds(start, blk), :]
            else:
                ld = lambda ref, start: ref[0, r, pl.ds(start, blk), :]
            k2_ = jnp.concatenate([ld(kr, prev), ld(kr, own)], axis=0)
            v2_ = jnp.concatenate([ld(vr, prev), ld(vr, own)], axis=0)
            o, m, l = _attn_block(ld(qr, own), k2_, v2_, bias_ref[g, 0, jnp.minimum(i, 1)])
            m = jnp.broadcast_to(m, (blk, LANES))
            l = jnp.broadcast_to(l, (blk, LANES))
            if 1 < d <= s1:
                rows = pl.ds(i * (blk * d) + r, blk, stride=d)
                o_scr[g - 1, rows, :] = o
                m_scr[g - 1, rows, :] = m
                l_scr[g - 1, rows, :] = l
                return
            if d > s1:
                s2 = d // s1
                rows = pl.ds((r % s1) * (seq // s1) + i * (blk * s2) + r // s1, blk, stride=s2)
                stage_scr[0, rows, :] = o
                stage_scr[1, rows, :] = m
                stage_scr[2, rows, :] = l
                return
            rows = pl.ds(own, blk)
            ms = [m, m_scr[0, rows, :], m_scr[1, rows, :]]
            mx = jnp.maximum(jnp.maximum(ms[0], ms[1]), ms[2])
            ws = [jnp.exp2(x - mx) for x in ms]
            num = ws[0] * o + ws[1] * o_scr[0, rows, :] + ws[2] * o_scr[1, rows, :]
            den = ws[0] * l + ws[1] * l_scr[0, rows, :] + ws[2] * l_scr[1, rows, :]
            o_ref[rows, :] = (num / den).astype(o_ref.dtype)

        ur = min(ATTN_UNROLL, d)
        ui = ATTN_UNROLL // ur
        assert d % ur == 0 and nblk % ui == 0

        def body(it, carry):
            i0 = (it // (d // ur)) * ui
            r0 = (it % (d // ur)) * ur
            for a in range(ui):
                for b in range(ur):
                    one_block(r0 + b, i0 + a)
            return carry

        jax.lax.fori_loop(0, (d // ur) * (nblk // ui), body, 0)

    run_group(1, q1, k1, v1)
    run_group(2, q2, k2, v2)
    g = N_GROUPS - 1
    assert DILATIONS[g] > s1 and all(d <= s1 for d in DILATIONS[:g])
    per_stream = seq // s1
    for a, dst in enumerate((o_scr, m_scr, l_scr)):
        for r1 in range(s1):
            for c in range(per_stream // blk):
                dst[g - 1, pl.ds(s1 * c * blk + r1, blk, stride=s1), :] = (
                    stage_scr[a, r1 * per_stream + c * blk:r1 * per_stream + (c + 1) * blk, :])
    run_group(0, q0, k0, v0)


def _attn_call(qkv, bias, casts, *, batch, seq):
    n_steps = batch * HEADS
    cast_in, cast_out, cast_shape = [], [], []
    for w, layer in casts:
        _, k_dim, n_dim = w.shape
        band = k_dim // n_steps
        assert band * n_steps == k_dim and band % 16 == 0
        cast_in.append(pl.BlockSpec((None, band, n_dim),
                                    lambda b, h, layer=layer: (layer, b * HEADS + h, 0)))
        cast_out.append(pl.BlockSpec((band, n_dim), lambda b, h: (b * HEADS + h, 0)))
        cast_shape.append(jax.ShapeDtypeStruct((k_dim, n_dim), BF16))
    in_specs = []
    for g in range(N_GROUPS):
        d = DILATIONS[g]
        if d == 1:
            spec = pl.BlockSpec((seq, HEAD_DIM), lambda b, h: (b, h))
        else:
            spec = pl.BlockSpec((1, d, seq // d, HEAD_DIM), lambda b, h: (b, 0, 0, h))
        in_specs += [spec] * 3
    in_specs.append(pl.BlockSpec((N_GROUPS, 1, 2, N_STEPS, 2 * N_STEPS), lambda b, h: (0, h, 0, 0, 0)))
    return pl.pallas_call(
        functools.partial(_attn_kernel, seq=seq, n_cast=len(casts)),
        grid=(batch, HEADS),
        in_specs=in_specs + cast_in,
        out_specs=[pl.BlockSpec((seq, HEAD_DIM), lambda b, h: (b, h))] + cast_out,
        out_shape=[jax.ShapeDtypeStruct((batch * seq, GROUP_COLS), BF16)] + cast_shape,
        scratch_shapes=[pltpu.VMEM((N_GROUPS - 1, seq, LANES), F32)] * 3
                       + [pltpu.VMEM((3, seq, LANES), F32)],
        compiler_params=_params(2),
        name="attn_prompt",
    )(*qkv, bias, *[w for w, _ in casts])


def _attn_sample_kernel(q0, q1, q2, n0, n1, n2, c0, c1, c2, slope_ref, o_ref, *, bb):
    q_refs, n_refs, c_refs = (q0, q1, q2), (n0, n1, n2), (c0, c1, c2)
    half = N_STEPS // 2
    tile = (2 * HEADS, HEAD_DIM)
    top = jax.lax.broadcasted_iota(jnp.int32, tile, 0) < HEADS
    steps_a = (N_STEPS - jax.lax.broadcasted_iota(jnp.int32, (half, 1, 1), 0)).astype(F32)
    back = jnp.where(top[:, 0:1], steps_a, steps_a - float(half))

    def both_halves(t):
        return jnp.where(top, t, pltpu.roll(t, HEADS, axis=0))

    for b in range(bb):
        outs, lses = [], []
        for g in range(N_GROUPS):
            q8, kv8 = q_refs[g][b], n_refs[g][b]
            xa, xb = c_refs[g][b, 0:half], c_refs[g][b, half:N_STEPS]
            slope = both_halves(slope_ref[g])[:, 0:1]
            bias = (back * (-float(DILATIONS[g]))) * slope
            prod = jnp.where(top, xa * q8, pltpu.roll(xb * q8, HEADS, axis=1))
            s = jnp.sum(prod, axis=-1, keepdims=True) + bias
            s_self = both_halves(jnp.broadcast_to(jnp.sum(kv8 * q8, axis=-1, keepdims=True), tile))
            m = jnp.broadcast_to(jnp.max(s, axis=0), tile)
            m = jnp.maximum(jnp.maximum(m, pltpu.roll(m, HEADS, axis=0)), s_self)
            p = jnp.exp2(s - m[:, 0:1])
            p_self = jnp.exp2(s_self - m)
            l = jnp.broadcast_to(jnp.sum(p, axis=0), tile)
            l = l + pltpu.roll(l, HEADS, axis=0) + p_self
            p_b = jnp.broadcast_to(p, xb.shape)
            acc = jnp.sum(pltpu.roll(p_b, HEADS, axis=1) * xa + p_b * xb, axis=0) + p_self * kv8
            outs.append(acc / l)
            lses.append(m + jnp.log(l) * LOG2E)
        mx = jnp.maximum(jnp.maximum(lses[0], lses[1]), lses[2])
        ws = [jnp.exp2(v - mx) for v in lses]
        num = ws[0] * outs[0] + ws[1] * outs[1] + ws[2] * outs[2]
        o8 = num / (ws[0] + ws[1] + ws[2])
        for hh in range(HEADS):
            o_ref[b:b + 1, hh * HEAD_DIM:(hh + 1) * HEAD_DIM] = o8[HEADS + hh:HEADS + hh + 1, :]


def _attn_sample_call(q_tiles, new_tiles, caches, layer, slopes8):
    bd = q_tiles[0].shape[0]
    bb = 8
    nb = bd // bb
    tile = (2 * HEADS, HEAD_DIM)
    in_specs = [pl.BlockSpec((bb,) + tile, lambda i: (i, 0, 0))] * 3
    in_specs += [pl.BlockSpec((bb,) + tile, lambda i: (layer * nb + i, 0, 0))] * 3
    in_specs += [pl.BlockSpec((None, bb, N_STEPS) + tile, lambda i: (layer, i, 0, 0, 0))] * 3
    in_specs.append(pl.BlockSpec((N_GROUPS,) + tile, lambda i: (0, 0, 0)))
    return pl.pallas_call(
        functools.partial(_attn_sample_kernel, bb=bb),
        grid=(nb,),
        in_specs=in_specs,
        out_specs=pl.BlockSpec((bb, GROUP_COLS), lambda i: (i, 0)),
        out_shape=jax.ShapeDtypeStruct((bd, GROUP_COLS), F32),
        compiler_params=_params(1),
        name="attn_sample",
    )(*q_tiles, *new_tiles, *caches, slopes8)


def _post_math(x, mod, ya, w, ws_ref, bs_ref, mix_scr, *, chunk_rows, d_model, d_ff):
    g1 = mod[:, 2 * d_model:3 * d_model]
    g2 = mod[:, 5 * d_model:6 * d_model]

    h = _modulated_norm(x, w["n1w"][...], mod, 0, d_model).astype(BF16)
    sw = SGU_WIDTH
    wg1, wg2 = w["wg1"], w["wg2"]
    u = _gelu(_dot(h, wg1[:, 0:sw]))
    vs = _gelu(_dot(h, wg1[:, sw:2 * sw]))
    mu = jnp.mean(vs, axis=-1, keepdims=True)
    vc = vs - mu
    var = jnp.mean(vc * vc, axis=-1, keepdims=True)
    vs = vc * jax.lax.rsqrt(var + EPS) * w["lnw"][...] + w["lnb"][...]
    half = d_model // 2
    ga = jnp.concatenate([_sigmoid(_dot(h, wg1[:, 2 * sw:2 * sw + half])),
                          _sigmoid(_dot(h, wg2[:, 0:half]))], axis=-1)
    gb = _sigmoid(_dot(h, wg2[:, half:half + d_model]))

    if chunk_rows is not None:
        row = jax.lax.broadcasted_iota(jnp.int32, (CHUNK, CHUNK), 0)
        col = jax.lax.broadcasted_iota(jnp.int32, (CHUNK, CHUNK), 1)
        tril = col <= row
        lane = jax.lax.broadcasted_iota(jnp.int32, (CHUNK, LANES), 1)
        low = lane < SGU_GROUP_DIM
        wms = [jnp.where(tril, ws_ref[gi], 0.0).astype(BF16) for gi in range(SGU_GROUPS)]
        vs_b = vs.astype(BF16)
        for c in range(chunk_rows // CHUNK):
            rows = slice(c * CHUNK, (c + 1) * CHUNK)
            for j in range(SGU_WIDTH // LANES):
                cols = slice(j * LANES, (j + 1) * LANES)
                v = vs_b[rows, cols]
                mix_scr[rows, cols] = jnp.where(low, _dot(wms[2 * j], v), _dot(wms[2 * j + 1], v))
        mixed = mix_scr[...] + jnp.concatenate([bs_ref[...]] * (chunk_rows // CHUNK), axis=0)
    else:
        mixed = vs * ws_ref[...] + bs_ref[...]

    ys = (u * mixed).astype(BF16)
    xa = _dot(ya.astype(BF16), w["wpa"][...])
    xs = _dot(ys, w["wps"][...])
    merged = (ga * xa + gb * xs).astype(BF16)
    x1 = x + g1 * _dot(merged, w["wo"][...])
    h2 = _modulated_norm(x1, w["n2w"][...], mod, 1, d_model).astype(BF16)
    a = _dot(h2, w["wfi"][:, 0:d_ff])
    b = _dot(h2, w["wfi"][:, d_ff:2 * d_ff])
    act = (a * _sigmoid(a) * b).astype(BF16)
    return x1 + g2 * _dot(act, w["wfo"][...]), vs


_POST_WEIGHTS = ("n1w", "wg1", "wg2", "lnw", "lnb", "wpa", "wps", "wo", "n2w", "wfi", "wfo")


def _post_kernel(x_ref, mod_ref, ya_ref, ws_ref, bs_ref, xs_ref, mods_ref, yas_ref, wss_ref, bss_ref,
                 *refs, tm, d_model, d_ff):
    n_w = len(_POST_WEIGHTS)
    w = dict(zip(_POST_WEIGHTS, refs[:n_w]))
    o_ref, os_ref, vs_ref, mix_scr = refs[n_w:]
    o_ref[...], _ = _post_math(x_ref[...], mod_ref[0], ya_ref[...], w, ws_ref, bs_ref, mix_scr,
                               chunk_rows=tm, d_model=d_model, d_ff=d_ff)

    @pl.when(pl.program_id(0) == pl.num_programs(0) - 1)
    def _():
        os_ref[...], vs_ref[...] = _post_math(xs_ref[...], mods_ref[0], yas_ref[...], w, wss_ref,
                                              bss_ref, None, chunk_rows=None,
                                              d_model=d_model, d_ff=d_ff)


def _post_call(x, mod, ya, ws, bs, xs, mod_s, ya_s, ws_s, bs_s,
               n1w, w_in, lnw, lnb, wpa, wps, wo, n2w, wfi, wfo, *, seq, tm, layer):
    t, d_model = x.shape
    ts = xs.shape[0]
    d_ff = wfo.shape[0]
    tiles_per_seq = seq // tm
    gblk = QKV_COLS
    assert 2 * SGU_WIDTH + d_model // 2 == gblk and w_in.shape[1] == 5 * gblk

    def w_gate(j):
        return pl.BlockSpec((d_model, gblk), lambda i: (0, 3 + j), pipeline_mode=pl.Buffered(1))

    def tok(cols):
        return pl.BlockSpec((tm, cols), lambda i: (i, 0))

    weight_specs = [_resident(n1w.shape), w_gate(0), w_gate(1), _resident(lnw.shape),
                    _resident(lnb.shape), _resident_layer(wpa.shape, layer),
                    _resident_layer(wps.shape, layer), _resident_layer(wo.shape, layer),
                    _resident(n2w.shape), _resident(wfi.shape), _resident(wfo.shape)]
    assert len(weight_specs) == len(_POST_WEIGHTS)
    kern = functools.partial(_post_kernel, tm=tm, d_model=d_model, d_ff=d_ff)
    return pl.pallas_call(
        kern,
        grid=(t // tm,),
        in_specs=[tok(d_model),
                  pl.BlockSpec((1,) + mod.shape[1:], lambda i: (i // tiles_per_seq, 0, 0)),
                  tok(GROUP_COLS), _resident(ws.shape), _resident(bs.shape),
                  _resident(xs.shape), _resident(mod_s.shape), _resident(ya_s.shape),
                  _resident(ws_s.shape), _resident(bs_s.shape)] + weight_specs,
        out_specs=[tok(d_model),
                   pl.BlockSpec((ts, d_model), lambda i: (0, 0)),
                   pl.BlockSpec((ts, SGU_WIDTH), lambda i: (0, 0))],
        out_shape=[jax.ShapeDtypeStruct((t, d_model), F32),
                   jax.ShapeDtypeStruct((ts, d_model), F32),
                   jax.ShapeDtypeStruct((ts, SGU_WIDTH), F32)],
        scratch_shapes=[pltpu.VMEM((tm, SGU_WIDTH), F32)],
        compiler_params=_params(1),
        name="post",
    )(x, mod, ya, ws, bs, xs, mod_s, ya_s, ws_s, bs_s,
      n1w, w_in, w_in, lnw, lnb, wpa, wps, wo, n2w, wfi, wfo)


def kernel(x_prompt, x_sample, cache_kv_w128, cache_kv_w512, cache_kv_w2048, c_prompt, c_sample,
           w_ada, b_ada, norm1_w, w_in, q_norm_w, k_norm_w, sgu_ln_w, sgu_ln_b, w_spatial, b_spatial,
           w_proj_att, w_proj_sgu, w_out, norm2_w, w_ffn_in, w_ffn_out):
    batch, seq, d_model = x_prompt.shape
    bd, dec_seq, _ = x_sample.shape
    depth = w_in.shape[0]
    assert dec_seq == 1 and seq % (N_STEPS * DILATIONS[-1]) == 0
    caches_in = (cache_kv_w128, cache_kv_w512, cache_kv_w2048)
    for g in range(N_GROUPS):
        assert caches_in[g].shape[2] == WINDOWS[g]
    tm = 512
    tm_in = 512

    n_c = batch + bd
    c_all = jnp.concatenate([c_prompt, c_sample], axis=0)
    c_all = jnp.pad(c_all, ((0, -n_c % 16), (0, 0)))
    mod = _ada_call(c_all, w_ada, b_ada)

    band_bias = jnp.asarray(_band_bias())
    tile = (2 * HEADS, HEAD_DIM)
    slopes8 = np.zeros((N_GROUPS,) + tile, np.float32)
    slopes8[:, :HEADS, :] = _alibi_slopes()[:, :, None]
    slopes8 = jnp.asarray(slopes8)
    caches = [caches_in[g].reshape((depth, bd, N_STEPS, DILATIONS[g] * tile[0], tile[1]))
              for g in range(N_GROUPS)]

    xp = x_prompt.reshape(batch * seq, d_model)
    xs = x_sample.reshape(bd, d_model)
    p_states = s_states = None
    s_v = []
    wpa, wps, wo = w_proj_att.astype(BF16), w_proj_sgu.astype(BF16), w_out.astype(BF16)
    w_in_l = w_in[0].astype(BF16)
    for l in range(depth):
        qw = jnp.tile(q_norm_w[l][:, None, :], (1, HEADS, 1)).reshape(1, QKV_COLS)
        kw = jnp.tile(k_norm_w[l][:, None, :], (1, HEADS, 1)).reshape(1, QKV_COLS)
        lnw, lnb = sgu_ln_w[l].reshape(1, -1), sgu_ln_b[l].reshape(1, -1)
        n1w, n2w = norm1_w[l].reshape(1, -1), norm2_w[l].reshape(1, -1)
        mod_p = mod[l, :batch].reshape(batch, 1, -1)
        mod_s = mod[l, batch:n_c].reshape(1, bd, -1)
        bs_p = jnp.repeat(b_spatial[l].T, SGU_GROUP_DIM, axis=1)
        ws_s = jnp.repeat(w_spatial[l][:, 0, 0], SGU_GROUP_DIM).reshape(1, SGU_WIDTH)
        bs_s = jnp.repeat(b_spatial[l][:, 0], SGU_GROUP_DIM).reshape(1, SGU_WIDTH)

        outs = _inproj_call(xp, mod_p, n1w, w_in_l, qw, kw,
                            seq=seq, tm=tm_in, deinterleave=True, act_dtype=BF16,
                            layer=l, depth=depth, prev_states=p_states)
        qkv, p_states = outs[:9], outs[9:]
        casts = [(w_ffn_in, l), (w_ffn_out, l)] + ([(w_in, l + 1)] if l + 1 < depth else [])
        ya, wfi, wfo, *w_in_next = _attn_call(qkv, band_bias, casts, batch=batch, seq=seq)

        outs = _inproj_call(xs, mod_s, n1w, w_in_l, qw, kw,
                            seq=bd, tm=bd, deinterleave=False, act_dtype=F32,
                            layer=l, depth=depth, prev_states=s_states)
        q_tiles, s_states = [o.reshape((bd,) + tile) for o in outs[:3]], outs[3:]
        new_tiles = [s.reshape((depth * bd,) + tile) for s in s_states]
        ya_s = _attn_sample_call(q_tiles, new_tiles, caches, l, slopes8)

        xp, xs, vs = _post_call(xp, mod_p, ya, w_spatial[l], bs_p, xs, mod_s, ya_s, ws_s, bs_s,
                                n1w, w_in_l, lnw, lnb, wpa, wps, wo, n2w, wfi, wfo,
                                seq=seq, tm=tm, layer=l)
        s_v.append(vs.reshape(bd, 1, SGU_WIDTH))
        if w_in_next:
            (w_in_l,) = w_in_next

    p_states =[s.reshape(depth, batch, -1, 2, HEADS, HEAD_DIM) for s in p_states]
    s_states = [s.reshape(depth, bd, 1, 2, HEADS, HEAD_DIM) for s in s_states]
    return (xp.reshape(batch, seq, d_model), xs.reshape(bd, 1, d_model),
            p_states[0], p_states[1], p_states[2], s_states[0], s_states[1], s_states[2],
            jnp.stack(s_v))
```

```python
import functools
import math

import jax
import jax.numpy as jnp
import numpy as np
from jax.experimental import pallas as pl
from jax.experimental.pallas import tpu as pltpu

HEAD_DIM = 128
N_GROUPS = 3
HEADS = 4
GROUP_COLS = HEADS * HEAD_DIM
QKV_COLS = N_GROUPS * GROUP_COLS
WINDOWS = (128, 512, 2048)
DILATIONS = (1, 4, 16)
N_STEPS = 128
CHUNK = 128
SGU_WIDTH = 512
SGU_GROUPS = 8
SGU_GROUP_DIM = SGU_WIDTH // SGU_GROUPS
N_ADA = 6
EPS = 1e-6
NEG_INF = -1e30
LOG2E = math.log2(math.e)
QK_SCALE = HEAD_DIM ** -0.5 * LOG2E

LANES = 128
RELAYOUT_STRIDE = 4
VMEM_LIMIT_BYTES = 56 * 1024 * 1024

INPROJ_TILE_ROWS = 512
POST_TILE_ROWS = 512
ADALN_TILE_COLS = 2048
SAMPLE_ATTN_SEQS = 8
ATTN_UNROLL = 32

BF16 = jnp.bfloat16
F32 = jnp.float32


def _params(n_axes):
    return pltpu.CompilerParams(dimension_semantics=("arbitrary",) * n_axes,
                                vmem_limit_bytes=VMEM_LIMIT_BYTES)


def _resident(shape):
    nd = len(shape)
    return pl.BlockSpec(shape, lambda *_: (0,) * nd, pipeline_mode=pl.Buffered(1))


def _resident_layer(shape, layer):
    nd = len(shape)
    return pl.BlockSpec((None,) + tuple(shape[1:]), lambda *_: (layer,) + (0,) * (nd - 1),
                        pipeline_mode=pl.Buffered(1))


def _dot(a, b):
    return jnp.dot(a, b, preferred_element_type=F32)


def _dot_nt(a, b):
    return jax.lax.dot_general(a, b, (((1,), (1,)), ((), ())), preferred_element_type=F32)


def _rms(x):
    return x * jax.lax.rsqrt(jnp.mean(x * x, axis=-1, keepdims=True) + EPS)


def _gelu(x):
    return 0.5 * x * (1.0 + jax.lax.erf(x * (2.0 ** -0.5)))


def _sigmoid(x):
    return 1.0 / (1.0 + jnp.exp(-x))


def _ada_kernel(c_ref, w_ref, b_ref, o_ref):
    c = c_ref[...]
    a = c * _sigmoid(c)
    w = w_ref[...]
    a_hi = a.astype(BF16)
    a_lo = (a - a_hi.astype(F32)).astype(BF16)
    w_hi = w.astype(BF16)
    w_lo = (w - w_hi.astype(F32)).astype(BF16)
    rows = a.shape[0]
    both = _dot(jnp.concatenate([a_hi, a_lo], axis=0), w_hi)
    o_ref[...] = both[:rows] + both[rows:] + _dot(a_hi, w_lo) + b_ref[...]


def _ada_call(c_all, w_ada, b_ada):
    depth, d, n = w_ada.shape
    rows = c_all.shape[0]
    tn = ADALN_TILE_COLS
    return pl.pallas_call(
        _ada_kernel,
        grid=(depth, n // tn),
        in_specs=[pl.BlockSpec((rows, d), lambda l, j: (0, 0)),
                  pl.BlockSpec((None, d, tn), lambda l, j: (l, 0, j)),
                  pl.BlockSpec((None, 1, tn), lambda l, j: (l, 0, j))],
        out_specs=pl.BlockSpec((None, rows, tn), lambda l, j: (l, 0, j)),
        out_shape=jax.ShapeDtypeStruct((depth, rows, n), F32),
        compiler_params=_params(2),
        name="adaln_mod",
    )(c_all, w_ada, b_ada.reshape(depth, 1, n))


def _modulated_norm(x, w, mod, which, d_model):
    shift = mod[:, 3 * which * d_model:(3 * which + 1) * d_model]
    scale = mod[:, (3 * which + 1) * d_model:(3 * which + 2) * d_model]
    return _rms(x) * w * (1.0 + scale) + shift


def _inproj_kernel(x_ref, mod_ref, n1w_ref, w_ref, qw_ref, kw_ref,
                   *refs, tm, d_model, dils, state_rows, tiles_per_seq, deinterleave, n_aliased):
    refs = refs[n_aliased:]
    if deinterleave:
        q_refs, k_refs, v_refs, st_refs = refs[0:9:3], refs[1:9:3], refs[2:9:3], refs[9:12]
        h_scr, slab, slab2, kv_stash = refs[12:]
    else:
        q_refs, st_refs = refs[0:3], refs[3:6]
        h_scr, slab, slab2, kv_stash = refs[6:]
    ti = pl.program_id(0) % tiles_per_seq
    tile_rows = 2 * HEADS

    h = _modulated_norm(x_ref[...], n1w_ref[...], mod_ref[0], 0, d_model)
    h_scr[...] = h.astype(BF16)

    def proj(c0, width):
        return _dot(h_scr[...], w_ref[:, c0:c0 + width])

    def put_tiles(ref, first_row, val):
        for hh in range(HEADS):
            ref[pl.ds(first_row + hh, tm, stride=tile_rows), :] = val[:, hh * HEAD_DIM:(hh + 1) * HEAD_DIM]

    def put(ref, g, val, slot, slabs=None):
        d = dils[g]
        if d == 1:
            ref[...] = val.astype(ref.dtype)
            return
        if slabs is None:
            slabs = [slab.at[g - 1, hh] for hh in range(HEADS)]
            for hh in range(HEADS):
                slabs[hh][...] = val[:, hh * HEAD_DIM:(hh + 1) * HEAD_DIM]
        if d <= RELAYOUT_STRIDE:
            for r in range(d):
                for hh in range(HEADS):
                    ref[0, r, :, hh * HEAD_DIM:(hh + 1) * HEAD_DIM] = (
                        slabs[hh][pl.ds(r, tm // d, stride=d), :].astype(ref.dtype))
            return
        s1, s2 = RELAYOUT_STRIDE, d // RELAYOUT_STRIDE
        part = tm // s1
        for hh in range(HEADS):
            for r1 in range(s1):
                slab2[slot, hh,r1 * part:(r1 + 1) * part, :] = slabs[hh][pl.ds(r1, part, stride=s1), :]
            for r1 in range(s1):
                for r2 in range(s2):
                    ref[0, r1 + s1 * r2, :, hh * HEAD_DIM:(hh + 1) * HEAD_DIM] = (
                        slab2[slot, hh,pl.ds(r1 * part + r2, tm // d, stride=s2), :].astype(ref.dtype))

    def head_rms(z, w):
        parts = [_rms(z[:, hh * HEAD_DIM:(hh + 1) * HEAD_DIM]) for hh in range(HEADS)]
        return jnp.concatenate(parts, axis=-1) * w

    def put_state(g, kn, v):
        if state_rows[g] is None:
            put_tiles(st_refs[g], 0, kn)
            put_tiles(st_refs[g], HEADS, v)
        else:
            for hh in range(HEADS):
                cols = slice(hh * HEAD_DIM, (hh + 1) * HEAD_DIM)
                kv_stash[g, hh] = kn[:, cols]
                kv_stash[g, HEADS + hh] = v[:, cols]

    def flush_state(g):
        st, rows = st_refs[g], min(state_rows[g], tm)
        first = tiles_per_seq - max(state_rows[g] // tm, 1)

        @pl.when(ti >= first)
        def _():
            for j in range(2 * HEADS):
                st[pl.ds(j, rows, stride=2 * HEADS), :] = kv_stash[g, j, tm - rows:, :]

    for g in reversed(range(N_GROUPS)):
        c0 = g * GROUP_COLS
        q = head_rms(proj(c0, GROUP_COLS), qw_ref[:, c0:c0 + GROUP_COLS] * QK_SCALE)
        kn = head_rms(proj(QKV_COLS + c0, GROUP_COLS), kw_ref[:, c0:c0 + GROUP_COLS])
        v = proj(2 * QKV_COLS + c0, GROUP_COLS)
        put_state(g, kn, v)
        if not deinterleave:
            put_tiles(q_refs[g], 0, q)
            put_tiles(q_refs[g], HEADS, jnp.zeros_like(q))
            continue
        put(q_refs[g], g, q, 0)
        put(k_refs[g], g, kn, 1, [kv_stash.at[g, hh] for hh in range(HEADS)])
        put(v_refs[g], g, v, 2, [kv_stash.at[g, HEADS + hh] for hh in range(HEADS)])

    for g in range(N_GROUPS):
        if state_rows[g] is not None:
            flush_state(g)


def _inproj_call(x, mod, n1w, w_in, qw, kw, *, seq, tm, deinterleave, act_dtype,
                 layer=0, depth=1, prev_states=None):
    t, d_model = x.shape
    batch = t // seq
    tiles_per_seq = seq // tm

    def tok(cols, dtype):
        return (jax.ShapeDtypeStruct((t, cols), dtype), pl.BlockSpec((tm, cols), lambda i: (i, 0)))

    def grp(g, dtype):
        d = DILATIONS[g]
        if d == 1:
            return tok(GROUP_COLS, dtype)
        assert tm % d == 0 and (tm // d) % 16 == 0
        return (jax.ShapeDtypeStruct((batch, d, seq // d, GROUP_COLS), dtype),
                pl.BlockSpec((1, d, tm // d, GROUP_COLS),
                             lambda i: (i // tiles_per_seq, 0, i % tiles_per_seq, 0)))

    tile_rows = 2 * HEADS
    outs = []
    for g in range(N_GROUPS):
        if deinterleave:
            outs += [grp(g, act_dtype)] * 3
        else:
            assert t == tm
            outs.append((jax.ShapeDtypeStruct((t * tile_rows, LANES), F32),
                         pl.BlockSpec((t * tile_rows, LANES), lambda i: (0, 0))))

    state_rows = []
    for g in range(N_GROUPS):
        if not deinterleave:
            state_rows.append(None)
            outs.append((jax.ShapeDtypeStruct((depth * t * tile_rows, LANES), F32),
                         pl.BlockSpec((t * tile_rows, LANES), lambda i: (layer, 0))))
            continue
        rows = min(WINDOWS[g], seq)
        state_rows.append(rows)
        blk_rows = min(rows, tm)
        assert rows % blk_rows == 0
        nblk = rows // blk_rows
        shape = jax.ShapeDtypeStruct((depth * batch * rows * 2 * HEADS, LANES), F32)
        spec = pl.BlockSpec(
            (blk_rows * 2 * HEADS, LANES),
            lambda i, nblk=nblk: ((layer * batch + i // tiles_per_seq) * nblk
                                  + jnp.maximum(i % tiles_per_seq - (tiles_per_seq - nblk), 0), 0))
        outs.append((shape, spec))

    aliased = list(prev_states) if prev_states is not None else []
    n_in = 6
    w_qkv = pl.BlockSpec((d_model, 3 * QKV_COLS), lambda i: (0, 0),
                         pipeline_mode=pl.Buffered(1))
    kern = functools.partial(_inproj_kernel, tm=tm, d_model=d_model, dils=DILATIONS,
                             state_rows=tuple(state_rows), tiles_per_seq=tiles_per_seq,
                             deinterleave=deinterleave, n_aliased=len(aliased))
    return pl.pallas_call(
        kern,
        grid=(t // tm,),
        in_specs=[pl.BlockSpec((tm, d_model), lambda i: (i, 0)),
                  pl.BlockSpec((1,) + mod.shape[1:], lambda i: (i // tiles_per_seq, 0, 0)),
                  _resident((1, d_model)),
                  w_qkv,
                  _resident((1, QKV_COLS)), _resident((1, QKV_COLS))]
                 + [pl.BlockSpec(memory_space=pl.ANY)] * len(aliased),
        out_specs=[o[1] for o in outs],
        out_shape=[o[0] for o in outs],
        input_output_aliases={n_in + g: len(outs) - N_GROUPS + g for g in range(len(aliased))},
        scratch_shapes=[pltpu.VMEM((tm, d_model), BF16),
                        pltpu.VMEM((N_GROUPS - 1, HEADS, tm, LANES), F32),
                        pltpu.VMEM((3, HEADS, tm, LANES), F32),
                        pltpu.VMEM((N_GROUPS, 2 * HEADS, tm, LANES), F32)],
        compiler_params=_params(1),
        name="inproj_deint" if deinterleave else "inproj_tok",
    )(x, mod, n1w, w_in, qw, kw, *aliased)


def _alibi_slopes():
    h = np.arange(1, N_GROUPS * HEADS + 1, dtype=np.float32)
    s = np.power(np.float32(2.0), -8.0 * h / (N_GROUPS * HEADS)).astype(np.float32)
    return (s.astype(np.float64) * LOG2E).astype(np.float32).reshape(N_GROUPS, HEADS)


def _band_bias():
    qi = np.arange(N_STEPS)[:, None]
    ki = np.arange(2 * N_STEPS)[None, :]
    dist = N_STEPS + qi - ki
    valid = (dist >= 0) & (dist <= N_STEPS)
    slopes = _alibi_slopes()
    out = np.empty((N_GROUPS, HEADS, 2, N_STEPS, 2 * N_STEPS), np.float32)
    for g in range(N_GROUPS):
        for hh in range(HEADS):
            b = np.where(valid, -slopes[g, hh] * (dist * DILATIONS[g]).astype(np.float32),
                         np.float32(NEG_INF)).astype(np.float32)
            out[g, hh, 1] = b
            first = b.copy()
            first[:, :N_STEPS] = NEG_INF
            out[g, hh, 0] = first
    return out


def _attn_block(q, k2, v2, bias):
    s = _dot_nt(q, k2) + bias
    m = jnp.max(s, axis=-1, keepdims=True)
    p = jnp.exp2(s - m)
    l = jnp.sum(p, axis=-1, keepdims=True)
    return _dot(p.astype(v2.dtype), v2), m, l


def _attn_kernel(q0, k0, v0, q1, k1, v1, q2, k2, v2, bias_ref, *refs, seq, n_cast):
    cast_src, refs = refs[:n_cast], refs[n_cast:]
    o_ref, cast_dst = refs[0], refs[1:1 + n_cast]
    o_scr, m_scr, l_scr, stage_scr = refs[1 + n_cast:]
    blk = N_STEPS
    s1 = RELAYOUT_STRIDE

    for src, dst in zip(cast_src, cast_dst):
        dst[...] = src[...].astype(dst.dtype)

    def run_group(g, qr, kr, vr):
        d = DILATIONS[g]
        nblk = seq // d // blk

        def one_block(r, i):
            own = pl.multiple_of(i * blk, blk)
            prev = pl.multiple_of(jnp.maximum(i - 1, 0) * blk, blk)
            if d == 1:
                ld = lambda ref, start: ref[pl.ds(start, blk), :]
            else:
                ld = lambda ref, start: ref[0, r, pl.ds(start, blk), :]
            k2_ = jnp.concatenate([ld(kr, prev), ld(kr, own)], axis=0)
            v2_ = jnp.concatenate([ld(vr, prev), ld(vr, own)], axis=0)
            o, m, l = _attn_block(ld(qr, own), k2_, v2_, bias_ref[g, 0, jnp.minimum(i, 1)])
            m = jnp.broadcast_to(m, (blk, LANES))
            l = jnp.broadcast_to(l, (blk, LANES))
            if 1 < d <= s1:
                rows = pl.ds(i * (blk * d) + r, blk, stride=d)
                o_scr[g - 1, rows, :] = o
                m_scr[g - 1, rows, :] = m
                l_scr[g - 1, rows, :] = l
                return
            if d > s1:
                s2 = d // s1
                rows = pl.ds((r % s1) * (seq // s1) + i * (blk * s2) + r // s1, blk, stride=s2)
                stage_scr[0, rows, :] = o
                stage_scr[1, rows, :] = m
                stage_scr[2, rows, :] = l
                return
            rows = pl.ds(own, blk)
            ms = [m, m_scr[0, rows, :], m_scr[1, rows, :]]
            mx = jnp.maximum(jnp.maximum(ms[0], ms[1]), ms[2])
            ws = [jnp.exp2(x - mx) for x in ms]
            num = ws[0] * o + ws[1] * o_scr[0, rows, :] + ws[2] * o_scr[1, rows, :]
            den = ws[0] * l + ws[1] * l_scr[0, rows, :] + ws[2] * l_scr[1, rows, :]
            o_ref[rows, :] = (num / den).astype(o_ref.dtype)

        ur = min(ATTN_UNROLL, d)
        ui = ATTN_UNROLL // ur
        assert d % ur == 0 and nblk % ui == 0

        def body(it, carry):
            i0 = (it // (d // ur)) * ui
            r0 = (it % (d // ur)) * ur
            for a in range(ui):
                for b in range(ur):
                    one_block(r0 + b, i0 + a)
            return carry

        jax.lax.fori_loop(0, (d // ur) * (nblk // ui), body, 0)

    run_group(1, q1, k1, v1)
    run_group(2, q2, k2, v2)
    g = N_GROUPS - 1
    assert DILATIONS[g] > s1 and all(d <= s1 for d in DILATIONS[:g])
    per_stream = seq // s1
    for a, dst in enumerate((o_scr, m_scr, l_scr)):
        for r1 in range(s1):
            for c in range(per_stream // blk):
                dst[g - 1, pl.ds(s1 * c * blk + r1, blk, stride=s1), :] = (
                    stage_scr[a, r1 * per_stream + c * blk:r1 * per_stream + (c + 1) * blk, :])
    run_group(0, q0, k0, v0)


def _attn_call(qkv, bias, casts, *, batch, seq):
    n_steps = batch * HEADS
    cast_in, cast_out, cast_shape = [], [], []
    for w, layer in casts:
        _, k_dim, n_dim = w.shape
        band = k_dim // n_steps
        assert band * n_steps == k_dim and band % 16 == 0
        cast_in.append(pl.BlockSpec((None, band, n_dim),
                                    lambda b, h, layer=layer: (layer, b * HEADS + h, 0)))
        cast_out.append(pl.BlockSpec((band, n_dim), lambda b, h: (b * HEADS + h, 0)))
        cast_shape.append(jax.ShapeDtypeStruct((k_dim, n_dim), BF16))
    in_specs = []
    for g in range(N_GROUPS):
        d = DILATIONS[g]
        if d == 1:
            spec = pl.BlockSpec((seq, HEAD_DIM), lambda b, h: (b, h))
        else:
            spec = pl.BlockSpec((1, d, seq // d, HEAD_DIM), lambda b, h: (b, 0, 0, h))
        in_specs += [spec] * 3
    in_specs.append(pl.BlockSpec((N_GROUPS, 1, 2, N_STEPS, 2 * N_STEPS), lambda b, h: (0, h, 0, 0, 0)))
    return pl.pallas_call(
        functools.partial(_attn_kernel, seq=seq, n_cast=len(casts)),
        grid=(batch, HEADS),
        in_specs=in_specs + cast_in,
        out_specs=[pl.BlockSpec((seq, HEAD_DIM), lambda b, h: (b, h))] + cast_out,
        out_shape=[jax.ShapeDtypeStruct((batch * seq, GROUP_COLS), BF16)] + cast_shape,
        scratch_shapes=[pltpu.VMEM((N_GROUPS - 1, seq, LANES), F32)] * 3
                       + [pltpu.VMEM((3, seq, LANES), F32)],
        compiler_params=_params(2),
        name="attn_prompt",
    )(*qkv, bias, *[w for w, _ in casts])


def _attn_sample_kernel(q0, q1, q2, n0, n1, n2, c0, c1, c2, slope_ref, o_ref, *, bb):
    q_refs, n_refs, c_refs = (q0, q1, q2), (n0, n1, n2), (c0, c1, c2)
    half = N_STEPS // 2
    tile = (2 * HEADS, HEAD_DIM)
    top = jax.lax.broadcasted_iota(jnp.int32, tile, 0) < HEADS
    steps_a = (N_STEPS - jax.lax.broadcasted_iota(jnp.int32, (half, 1, 1), 0)).astype(F32)
    back = jnp.where(top[:, 0:1], steps_a, steps_a - float(half))

    def both_halves(t):
        return jnp.where(top, t, pltpu.roll(t, HEADS, axis=0))

    for b in range(bb):
        outs, lses = [], []
        for g in range(N_GROUPS):
            q8, kv8 = q_refs[g][b], n_refs[g][b]
            xa, xb = c_refs[g][b, 0:half], c_refs[g][b, half:N_STEPS]
            slope = both_halves(slope_ref[g])[:, 0:1]
            bias = (back * (-float(DILATIONS[g]))) * slope
            prod = jnp.where(top, xa * q8, pltpu.roll(xb * q8, HEADS, axis=1))
            s = jnp.sum(prod, axis=-1, keepdims=True) + bias
            s_self = both_halves(jnp.broadcast_to(jnp.sum(kv8 * q8, axis=-1, keepdims=True), tile))
            m = jnp.broadcast_to(jnp.max(s, axis=0), tile)
            m = jnp.maximum(jnp.maximum(m, pltpu.roll(m, HEADS, axis=0)), s_self)
            p = jnp.exp2(s - m[:, 0:1])
            p_self = jnp.exp2(s_self - m)
            l = jnp.broadcast_to(jnp.sum(p, axis=0), tile)
            l = l + pltpu.roll(l, HEADS, axis=0) + p_self
            p_b = jnp.broadcast_to(p, xb.shape)
            acc = jnp.sum(pltpu.roll(p_b, HEADS, axis=1) * xa + p_b * xb, axis=0) + p_self * kv8
            outs.append(acc / l)
            lses.append(m + jnp.log(l) * LOG2E)
        mx = jnp.maximum(jnp.maximum(lses[0], lses[1]), lses[2])
        ws = [jnp.exp2(v - mx) for v in lses]
        num = ws[0] * outs[0] + ws[1] * outs[1] + ws[2] * outs[2]
        o8 = num / (ws[0] + ws[1] + ws[2])
        for hh in range(HEADS):
            o_ref[b:b + 1, hh * HEAD_DIM:(hh + 1) * HEAD_DIM] = o8[HEADS + hh:HEADS + hh + 1, :]


def _attn_sample_call(q_tiles, new_tiles, caches, layer, slopes8):
    bd = q_tiles[0].shape[0]
    bb = SAMPLE_ATTN_SEQS
    nb = bd // bb
    tile = (2 * HEADS, HEAD_DIM)
    in_specs = [pl.BlockSpec((bb,) + tile, lambda i: (i, 0, 0))] * 3
    in_specs += [pl.BlockSpec((bb,) + tile, lambda i: (layer * nb + i, 0, 0))] * 3
    in_specs += [pl.BlockSpec((None, bb, N_STEPS) + tile, lambda i: (layer, i, 0, 0, 0))] * 3
    in_specs.append(pl.BlockSpec((N_GROUPS,) + tile, lambda i: (0, 0, 0)))
    return pl.pallas_call(
        functools.partial(_attn_sample_kernel, bb=bb),
        grid=(nb,),
        in_specs=in_specs,
        out_specs=pl.BlockSpec((bb, GROUP_COLS), lambda i: (i, 0)),
        out_shape=jax.ShapeDtypeStruct((bd, GROUP_COLS), F32),
        compiler_params=_params(1),
        name="attn_sample",
    )(*q_tiles, *new_tiles, *caches, slopes8)


def _post_math(x, mod, ya, w, ws_ref, bs_ref, mix_scr, *, chunk_rows, d_model, d_ff):
    g1 = mod[:, 2 * d_model:3 * d_model]
    g2 = mod[:, 5 * d_model:6 * d_model]

    h = _modulated_norm(x, w["n1w"][...], mod, 0, d_model).astype(BF16)
    sw = SGU_WIDTH
    wg1, wg2 = w["wg1"], w["wg2"]
    u = _gelu(_dot(h, wg1[:, 0:sw]))
    vs = _gelu(_dot(h, wg1[:, sw:2 * sw]))
    mu = jnp.mean(vs, axis=-1, keepdims=True)
    vc = vs - mu
    var = jnp.mean(vc * vc, axis=-1, keepdims=True)
    vs = vc * jax.lax.rsqrt(var + EPS) * w["lnw"][...] + w["lnb"][...]
    half = d_model // 2
    ga = jnp.concatenate([_sigmoid(_dot(h, wg1[:, 2 * sw:2 * sw + half])),
                          _sigmoid(_dot(h, wg2[:, 0:half]))], axis=-1)
    gb = _sigmoid(_dot(h, wg2[:, half:half + d_model]))

    if chunk_rows is not None:
        row = jax.lax.broadcasted_iota(jnp.int32, (CHUNK, CHUNK), 0)
        col = jax.lax.broadcasted_iota(jnp.int32, (CHUNK, CHUNK), 1)
        tril = col <= row
        lane = jax.lax.broadcasted_iota(jnp.int32, (CHUNK, LANES), 1)
        low = lane < SGU_GROUP_DIM
        wms = [jnp.where(tril, ws_ref[gi], 0.0).astype(BF16) for gi in range(SGU_GROUPS)]
        vs_b = vs.astype(BF16)
        for c in range(chunk_rows // CHUNK):
            rows = slice(c * CHUNK, (c + 1) * CHUNK)
            for j in range(SGU_WIDTH // LANES):
                cols = slice(j * LANES, (j + 1) * LANES)
                v = vs_b[rows, cols]
                mix_scr[rows, cols] = jnp.where(low, _dot(wms[2 * j], v), _dot(wms[2 * j + 1], v))
        mixed = mix_scr[...] + jnp.concatenate([bs_ref[...]] * (chunk_rows // CHUNK), axis=0)
    else:
        mixed = vs * ws_ref[...] + bs_ref[...]

    ys = (u * mixed).astype(BF16)
    xa = _dot(ya.astype(BF16), w["wpa"][...])
    xs = _dot(ys, w["wps"][...])
    merged = (ga * xa + gb * xs).astype(BF16)
    x1 = x + g1 * _dot(merged, w["wo"][...])
    h2 = _modulated_norm(x1, w["n2w"][...], mod, 1, d_model).astype(BF16)
    a = _dot(h2, w["wfi"][:, 0:d_ff])
    b = _dot(h2, w["wfi"][:, d_ff:2 * d_ff])
    act = (a * _sigmoid(a) * b).astype(BF16)
    return x1 + g2 * _dot(act, w["wfo"][...]), vs


_POST_WEIGHTS = ("n1w", "wg1", "wg2", "lnw", "lnb", "wpa", "wps", "wo", "n2w", "wfi", "wfo")


def _post_kernel(x_ref, mod_ref, ya_ref, ws_ref, bs_ref, xs_ref, mods_ref, yas_ref, wss_ref, bss_ref,
                 *refs, tm, d_model, d_ff):
    n_w = len(_POST_WEIGHTS)
    w = dict(zip(_POST_WEIGHTS, refs[:n_w]))
    o_ref, os_ref, vs_ref, mix_scr = refs[n_w:]
    o_ref[...], _ = _post_math(x_ref[...], mod_ref[0], ya_ref[...], w, ws_ref, bs_ref, mix_scr,
                               chunk_rows=tm, d_model=d_model, d_ff=d_ff)

    @pl.when(pl.program_id(0) == pl.num_programs(0) - 1)
    def _():
        os_ref[...], vs_ref[...] = _post_math(xs_ref[...], mods_ref[0], yas_ref[...], w, wss_ref,
                                              bss_ref, None, chunk_rows=None,
                                              d_model=d_model, d_ff=d_ff)


def _post_call(x, mod, ya, ws, bs, xs, mod_s, ya_s, ws_s, bs_s,
               n1w, w_in, lnw, lnb, wpa, wps, wo, n2w, wfi, wfo, *, seq, tm, layer):
    t, d_model = x.shape
    ts = xs.shape[0]
    d_ff = wfo.shape[0]
    tiles_per_seq = seq // tm
    gblk = QKV_COLS
    assert 2 * SGU_WIDTH + d_model // 2 == gblk and w_in.shape[1] == 5 * gblk

    def w_gate(j):
        return pl.BlockSpec((d_model, gblk), lambda i: (0, 3 + j), pipeline_mode=pl.Buffered(1))

    def tok(cols):
        return pl.BlockSpec((tm, cols), lambda i: (i, 0))

    weight_specs = [_resident(n1w.shape), w_gate(0), w_gate(1), _resident(lnw.shape),
                    _resident(lnb.shape), _resident_layer(wpa.shape, layer),
                    _resident_layer(wps.shape, layer), _resident_layer(wo.shape, layer),
                    _resident(n2w.shape), _resident(wfi.shape), _resident(wfo.shape)]
    assert len(weight_specs) == len(_POST_WEIGHTS)
    kern = functools.partial(_post_kernel, tm=tm, d_model=d_model, d_ff=d_ff)
    return pl.pallas_call(
        kern,
        grid=(t // tm,),
        in_specs=[tok(d_model),
                  pl.BlockSpec((1,) + mod.shape[1:], lambda i: (i // tiles_per_seq, 0, 0)),
                  tok(GROUP_COLS), _resident(ws.shape), _resident(bs.shape),
                  _resident(xs.shape), _resident(mod_s.shape), _resident(ya_s.shape),
                  _resident(ws_s.shape), _resident(bs_s.shape)] + weight_specs,
        out_specs=[tok(d_model),
                   pl.BlockSpec((ts, d_model), lambda i: (0, 0)),
                   pl.BlockSpec((ts, SGU_WIDTH), lambda i: (0, 0))],
        out_shape=[jax.ShapeDtypeStruct((t, d_model), F32),
                   jax.ShapeDtypeStruct((ts, d_model), F32),
                   jax.ShapeDtypeStruct((ts, SGU_WIDTH), F32)],
        scratch_shapes=[pltpu.VMEM((tm, SGU_WIDTH), F32)],
        compiler_params=_params(1),
        name="post",
    )(x, mod, ya, ws, bs, xs, mod_s, ya_s, ws_s, bs_s,
      n1w, w_in, w_in, lnw, lnb, wpa, wps, wo, n2w, wfi, wfo)


def kernel(x_prompt, x_sample, cache_kv_w128, cache_kv_w512, cache_kv_w2048, c_prompt, c_sample,
           w_ada, b_ada, norm1_w, w_in, q_norm_w, k_norm_w, sgu_ln_w, sgu_ln_b, w_spatial, b_spatial,
           w_proj_att, w_proj_sgu, w_out, norm2_w, w_ffn_in, w_ffn_out):
    batch, seq, d_model = x_prompt.shape
    bd, dec_seq, _ = x_sample.shape
    depth = w_in.shape[0]
    assert dec_seq == 1 and seq % (N_STEPS * DILATIONS[-1]) == 0
    caches_in = (cache_kv_w128, cache_kv_w512, cache_kv_w2048)
    for g in range(N_GROUPS):
        assert caches_in[g].shape[2] == WINDOWS[g]

    n_c = batch + bd
    c_all = jnp.concatenate([c_prompt, c_sample], axis=0)
    c_all = jnp.pad(c_all, ((0, -n_c % 16), (0, 0)))
    mod = _ada_call(c_all, w_ada, b_ada)

    band_bias = jnp.asarray(_band_bias())
    tile = (2 * HEADS, HEAD_DIM)
    slopes8 = np.zeros((N_GROUPS,) + tile, np.float32)
    slopes8[:, :HEADS, :] = _alibi_slopes()[:, :, None]
    slopes8 = jnp.asarray(slopes8)
    caches = [caches_in[g].reshape((depth, bd, N_STEPS, DILATIONS[g] * tile[0], tile[1]))
              for g in range(N_GROUPS)]

    xp = x_prompt.reshape(batch * seq, d_model)
    xs = x_sample.reshape(bd, d_model)
    p_states = s_states = None
    s_v = []
    wpa, wps, wo = w_proj_att.astype(BF16), w_proj_sgu.astype(BF16), w_out.astype(BF16)
    w_in_l = w_in[0].astype(BF16)
    for l in range(depth):
        qw = jnp.tile(q_norm_w[l][:, None, :], (1, HEADS, 1)).reshape(1, QKV_COLS)
        kw = jnp.tile(k_norm_w[l][:, None, :], (1, HEADS, 1)).reshape(1, QKV_COLS)
        lnw, lnb = sgu_ln_w[l].reshape(1, -1), sgu_ln_b[l].reshape(1, -1)
        n1w, n2w = norm1_w[l].reshape(1, -1), norm2_w[l].reshape(1, -1)
        mod_p = mod[l, :batch].reshape(batch, 1, -1)
        mod_s = mod[l, batch:n_c].reshape(1, bd, -1)
        bs_p = jnp.repeat(b_spatial[l].T, SGU_GROUP_DIM, axis=1)
        ws_s = jnp.repeat(w_spatial[l][:, 0, 0], SGU_GROUP_DIM).reshape(1, SGU_WIDTH)
        bs_s = jnp.repeat(b_spatial[l][:, 0], SGU_GROUP_DIM).reshape(1, SGU_WIDTH)

        outs = _inproj_call(xp, mod_p, n1w, w_in_l, qw, kw,
                            seq=seq, tm=INPROJ_TILE_ROWS, deinterleave=True, act_dtype=BF16,
                            layer=l, depth=depth, prev_states=p_states)
        qkv, p_states = outs[:9], outs[9:]
        casts = [(w_ffn_in, l), (w_ffn_out, l)] + ([(w_in, l + 1)] if l + 1 < depth else [])
        ya, wfi, wfo, *w_in_next = _attn_call(qkv, band_bias, casts, batch=batch, seq=seq)

        outs = _inproj_call(xs, mod_s, n1w, w_in_l, qw, kw,
                            seq=bd, tm=bd, deinterleave=False, act_dtype=F32,
                            layer=l, depth=depth, prev_states=s_states)
        q_tiles, s_states = [o.reshape((bd,) + tile) for o in outs[:3]], outs[3:]
        new_tiles = [s.reshape((depth * bd,) + tile) for s in s_states]
        ya_s = _attn_sample_call(q_tiles, new_tiles, caches, l, slopes8)

        xp, xs, vs = _post_call(xp, mod_p, ya, w_spatial[l], bs_p, xs, mod_s, ya_s, ws_s, bs_s,
                                n1w, w_in_l, lnw, lnb, wpa, wps, wo, n2w, wfi, wfo,
                                seq=seq, tm=POST_TILE_ROWS, layer=l)
        s_v.append(vs.reshape(bd, 1, SGU_WIDTH))
        if w_in_next:
            (w_in_l,) = w_in_next

    p_states =[s.reshape(depth, batch, -1, 2, HEADS, HEAD_DIM) for s in p_states]
    s_states = [s.reshape(depth, bd, 1, 2, HEADS, HEAD_DIM) for s in s_states]
    return (xp.reshape(batch, seq, d_model), xs.reshape(bd, 1, d_model),
            p_states[0], p_states[1], p_states[2], s_states[0], s_states[1], s_states[2],
            jnp.stack(s_v))
```

```python
import functools
import math

import jax
import jax.numpy as jnp
import numpy as np
from jax.experimental import pallas as pl
from jax.experimental.pallas import tpu as pltpu

HEAD_DIM = 128
N_GROUPS = 3
HEADS = 4
GROUP_COLS = HEADS * HEAD_DIM
QKV_COLS = N_GROUPS * GROUP_COLS
WINDOWS = (128, 512, 2048)
DILATIONS = (1, 4, 16)
N_STEPS = 128
CHUNK = 128
SGU_WIDTH = 512
SGU_GROUPS = 8
SGU_GROUP_DIM = SGU_WIDTH // SGU_GROUPS
N_ADA = 6
EPS = 1e-6
NEG_INF = -1e30
LOG2E = math.log2(math.e)
QK_SCALE = HEAD_DIM ** -0.5 * LOG2E

LANES = 128
RELAYOUT_STRIDE = 4
VMEM_LIMIT_BYTES = 56 * 1024 * 1024

INPROJ_TILE_ROWS = 512
POST_TILE_ROWS = 512
ADALN_TILE_COLS = 1536
SAMPLE_ATTN_SEQS = 8
ATTN_UNROLL = 32

BF16 = jnp.bfloat16
F32 = jnp.float32


def _params(n_axes):
    return pltpu.CompilerParams(dimension_semantics=("arbitrary",) * n_axes,
                                vmem_limit_bytes=VMEM_LIMIT_BYTES)


def _resident(shape):
    nd = len(shape)
    return pl.BlockSpec(shape, lambda *_: (0,) * nd, pipeline_mode=pl.Buffered(1))


def _resident_layer(shape, layer):
    nd = len(shape)
    return pl.BlockSpec((None,) + tuple(shape[1:]), lambda *_: (layer,) + (0,) * (nd - 1),
                        pipeline_mode=pl.Buffered(1))


def _dot(a, b):
    return jnp.dot(a, b, preferred_element_type=F32)


def _dot_nt(a, b):
    return jax.lax.dot_general(a, b, (((1,), (1,)), ((), ())), preferred_element_type=F32)


def _rms(x):
    return x * jax.lax.rsqrt(jnp.mean(x * x, axis=-1, keepdims=True) + EPS)


def _gelu(x):
    return 0.5 * x * (1.0 + jax.lax.erf(x * (2.0 ** -0.5)))


def _sigmoid(x):
    return 1.0 / (1.0 + jnp.exp(-x))


def _cast_specs(casts, n_steps, step):
    ins, outs, shapes = [], [], []
    for w, layer in casts:
        _, k_dim, n_dim = w.shape
        band = k_dim // n_steps
        assert band * n_steps == k_dim and band % 16 == 0
        ins.append(pl.BlockSpec((None, band, n_dim), lambda *g, layer=layer: (layer, step(*g), 0)))
        outs.append(pl.BlockSpec((band, n_dim), lambda *g: (step(*g), 0)))
        shapes.append(jax.ShapeDtypeStruct((k_dim, n_dim), BF16))
    return ins, outs, shapes


def _ada_kernel(c_ref, w_ref, b_ref, *refs):
    n_cast = (len(refs) - 1) // 2
    o_ref = refs[n_cast]
    for src, dst in zip(refs[:n_cast], refs[n_cast + 1:]):
        dst[...] = src[...].astype(dst.dtype)
    c = c_ref[...]
    a = c * _sigmoid(c)
    w = w_ref[...]
    a_hi = a.astype(BF16)
    a_lo = (a - a_hi.astype(F32)).astype(BF16)
    w_hi = w.astype(BF16)
    w_lo = (w - w_hi.astype(F32)).astype(BF16)
    rows = a.shape[0]
    both = _dot(jnp.concatenate([a_hi, a_lo], axis=0), w_hi)
    o_ref[...] = both[:rows] + both[rows:] + _dot(a_hi, w_lo) + b_ref[...]


def _ada_call(c_all, w_ada, b_ada, casts):
    depth, d, n = w_ada.shape
    rows = c_all.shape[0]
    tn = ADALN_TILE_COLS
    per_layer = n // tn
    cast_in, cast_out, cast_shape = _cast_specs(casts, depth * per_layer, lambda l, j: l * per_layer + j)
    return pl.pallas_call(
        _ada_kernel,
        grid=(depth, per_layer),
        in_specs=[pl.BlockSpec((rows, d), lambda l, j: (0, 0)),
                  pl.BlockSpec((None, d, tn), lambda l, j: (l, 0, j)),
                  pl.BlockSpec((None, 1, tn), lambda l, j: (l, 0, j))] + cast_in,
        out_specs=[pl.BlockSpec((None, rows, tn), lambda l, j: (l, 0, j))] + cast_out,
        out_shape=[jax.ShapeDtypeStruct((depth, rows, n), F32)] + cast_shape,
        compiler_params=_params(2),
        name="adaln_mod",
    )(c_all, w_ada, b_ada.reshape(depth, 1, n), *[w for w, _ in casts])


def _modulated_norm(x, w, mod, which, d_model):
    shift = mod[:, 3 * which * d_model:(3 * which + 1) * d_model]
    scale = mod[:, (3 * which + 1) * d_model:(3 * which + 2) * d_model]
    return _rms(x) * w * (1.0 + scale) + shift


def _inproj_kernel(x_ref, mod_ref, n1w_ref, w_ref, qw_ref, kw_ref,
                   *refs, tm, d_model, dils, state_rows, tiles_per_seq, deinterleave, n_aliased):
    refs = refs[n_aliased:]
    if deinterleave:
        q_refs, k_refs, v_refs, st_refs = refs[0:9:3], refs[1:9:3], refs[2:9:3], refs[9:12]
        h_scr, slab, slab2, kv_stash = refs[12:]
    else:
        q_refs, st_refs = refs[0:3], refs[3:6]
        h_scr, slab, slab2, kv_stash = refs[6:]
    ti = pl.program_id(0) % tiles_per_seq
    tile_rows = 2 * HEADS

    h = _modulated_norm(x_ref[...], n1w_ref[...], mod_ref[0], 0, d_model)
    h_scr[...] = h.astype(BF16)

    def proj(c0, width):
        return _dot(h_scr[...], w_ref[:, c0:c0 + width])

    def put_tiles(ref, first_row, val):
        for hh in range(HEADS):
            ref[pl.ds(first_row + hh, tm, stride=tile_rows), :] = val[:, hh * HEAD_DIM:(hh + 1) * HEAD_DIM]

    def put(ref, g, val, slot, slabs=None):
        d = dils[g]
        if d == 1:
            ref[...] = val.astype(ref.dtype)
            return
        if slabs is None:
            slabs = [slab.at[g - 1, hh] for hh in range(HEADS)]
            for hh in range(HEADS):
                slabs[hh][...] = val[:, hh * HEAD_DIM:(hh + 1) * HEAD_DIM]
        if d <= RELAYOUT_STRIDE:
            for r in range(d):
                for hh in range(HEADS):
                    ref[0, r, :, hh * HEAD_DIM:(hh + 1) * HEAD_DIM] = (
                        slabs[hh][pl.ds(r, tm // d, stride=d), :].astype(ref.dtype))
            return
        s1, s2 = RELAYOUT_STRIDE, d // RELAYOUT_STRIDE
        part = tm // s1
        for hh in range(HEADS):
            for r1 in range(s1):
                slab2[slot, hh,r1 * part:(r1 + 1) * part, :] = slabs[hh][pl.ds(r1, part, stride=s1), :]
            for r1 in range(s1):
                for r2 in range(s2):
                    ref[0, r1 + s1 * r2, :, hh * HEAD_DIM:(hh + 1) * HEAD_DIM] = (
                        slab2[slot, hh,pl.ds(r1 * part + r2, tm // d, stride=s2), :].astype(ref.dtype))

    def head_rms(z, w):
        parts = [_rms(z[:, hh * HEAD_DIM:(hh + 1) * HEAD_DIM]) for hh in range(HEADS)]
        return jnp.concatenate(parts, axis=-1) * w

    def put_state(g, kn, v):
        if state_rows[g] is None:
            put_tiles(st_refs[g], 0, kn)
            put_tiles(st_refs[g], HEADS, v)
        else:
            for hh in range(HEADS):
                cols = slice(hh * HEAD_DIM, (hh + 1) * HEAD_DIM)
                kv_stash[g, hh] = kn[:, cols]
                kv_stash[g, HEADS + hh] = v[:, cols]

    def flush_state(g):
        st, rows = st_refs[g], min(state_rows[g], tm)
        first = tiles_per_seq - max(state_rows[g] // tm, 1)

        @pl.when(ti >= first)
        def _():
            for j in range(2 * HEADS):
                st[pl.ds(j, rows, stride=2 * HEADS), :] = kv_stash[g, j, tm - rows:, :]

    for g in reversed(range(N_GROUPS)):
        c0 = g * GROUP_COLS
        q = head_rms(proj(c0, GROUP_COLS), qw_ref[:, c0:c0 + GROUP_COLS] * QK_SCALE)
        kn = head_rms(proj(QKV_COLS + c0, GROUP_COLS), kw_ref[:, c0:c0 + GROUP_COLS])
        v = proj(2 * QKV_COLS + c0, GROUP_COLS)
        put_state(g, kn, v)
        if not deinterleave:
            put_tiles(q_refs[g], 0, q)
            put_tiles(q_refs[g], HEADS, jnp.zeros_like(q))
            continue
        put(q_refs[g], g, q, 0)
        put(k_refs[g], g, kn, 1, [kv_stash.at[g, hh] for hh in range(HEADS)])
        put(v_refs[g], g, v, 2, [kv_stash.at[g, HEADS + hh] for hh in range(HEADS)])

    for g in range(N_GROUPS):
        if state_rows[g] is not None:
            flush_state(g)


def _inproj_call(x, mod, n1w, w_in, qw, kw, *, seq, tm, deinterleave, act_dtype,
                 layer=0, depth=1, prev_states=None):
    t, d_model = x.shape
    batch = t // seq
    tiles_per_seq = seq // tm

    def tok(cols, dtype):
        return (jax.ShapeDtypeStruct((t, cols), dtype), pl.BlockSpec((tm, cols), lambda i: (i, 0)))

    def grp(g, dtype):
        d = DILATIONS[g]
        if d == 1:
            return tok(GROUP_COLS, dtype)
        assert tm % d == 0 and (tm // d) % 16 == 0
        return (jax.ShapeDtypeStruct((batch, d, seq // d, GROUP_COLS), dtype),
                pl.BlockSpec((1, d, tm // d, GROUP_COLS),
                             lambda i: (i // tiles_per_seq, 0, i % tiles_per_seq, 0)))

    tile_rows = 2 * HEADS
    outs = []
    for g in range(N_GROUPS):
        if deinterleave:
            outs += [grp(g, act_dtype)] * 3
        else:
            assert t == tm
            outs.append((jax.ShapeDtypeStruct((t * tile_rows, LANES), F32),
                         pl.BlockSpec((t * tile_rows, LANES), lambda i: (0, 0))))

    state_rows = []
    for g in range(N_GROUPS):
        if not deinterleave:
            state_rows.append(None)
            outs.append((jax.ShapeDtypeStruct((depth * t * tile_rows, LANES), F32),
                         pl.BlockSpec((t * tile_rows, LANES), lambda i: (layer, 0))))
            continue
        rows = min(WINDOWS[g], seq)
        state_rows.append(rows)
        blk_rows = min(rows, tm)
        assert rows % blk_rows == 0
        nblk = rows // blk_rows
        shape = jax.ShapeDtypeStruct((depth * batch * rows * 2 * HEADS, LANES), F32)
        spec = pl.BlockSpec(
            (blk_rows * 2 * HEADS, LANES),
            lambda i, nblk=nblk: ((layer * batch + i // tiles_per_seq) * nblk
                                  + jnp.maximum(i % tiles_per_seq - (tiles_per_seq - nblk), 0), 0))
        outs.append((shape, spec))

    aliased = list(prev_states) if prev_states is not None else []
    n_in = 6
    w_qkv = pl.BlockSpec((d_model, 3 * QKV_COLS), lambda i: (0, 0),
                         pipeline_mode=pl.Buffered(1))
    kern = functools.partial(_inproj_kernel, tm=tm, d_model=d_model, dils=DILATIONS,
                             state_rows=tuple(state_rows), tiles_per_seq=tiles_per_seq,
                             deinterleave=deinterleave, n_aliased=len(aliased))
    return pl.pallas_call(
        kern,
        grid=(t // tm,),
        in_specs=[pl.BlockSpec((tm, d_model), lambda i: (i, 0)),
                  pl.BlockSpec((1,) + mod.shape[1:], lambda i: (i // tiles_per_seq, 0, 0)),
                  _resident((1, d_model)),
                  w_qkv,
                  _resident((1, QKV_COLS)), _resident((1, QKV_COLS))]
                 + [pl.BlockSpec(memory_space=pl.ANY)] * len(aliased),
        out_specs=[o[1] for o in outs],
        out_shape=[o[0] for o in outs],
        input_output_aliases={n_in + g: len(outs) - N_GROUPS + g for g in range(len(aliased))},
        scratch_shapes=[pltpu.VMEM((tm, d_model), BF16),
                        pltpu.VMEM((N_GROUPS - 1, HEADS, tm, LANES), F32),
                        pltpu.VMEM((3, HEADS, tm, LANES), F32),
                        pltpu.VMEM((N_GROUPS, 2 * HEADS, tm, LANES), F32)],
        compiler_params=_params(1),
        name="inproj_deint" if deinterleave else "inproj_tok",
    )(x, mod, n1w, w_in, qw, kw, *aliased)


def _alibi_slopes():
    h = np.arange(1, N_GROUPS * HEADS + 1, dtype=np.float32)
    s = np.power(np.float32(2.0), -8.0 * h / (N_GROUPS * HEADS)).astype(np.float32)
    return (s.astype(np.float64) * LOG2E).astype(np.float32).reshape(N_GROUPS, HEADS)


def _band_bias():
    qi = np.arange(N_STEPS)[:, None]
    ki = np.arange(2 * N_STEPS)[None, :]
    dist = N_STEPS + qi - ki
    valid = (dist >= 0) & (dist <= N_STEPS)
    slopes = _alibi_slopes()
    out = np.empty((N_GROUPS, HEADS, 2, N_STEPS, 2 * N_STEPS), np.float32)
    for g in range(N_GROUPS):
        for hh in range(HEADS):
            b = np.where(valid, -slopes[g, hh] * (dist * DILATIONS[g]).astype(np.float32),
                         np.float32(NEG_INF)).astype(np.float32)
            out[g, hh, 1] = b
            first = b.copy()
            first[:, :N_STEPS] = NEG_INF
            out[g, hh, 0] = first
    return out


def _attn_block(q, k2, v2, bias):
    s = _dot_nt(q, k2) + bias
    m = jnp.max(s, axis=-1, keepdims=True)
    p = jnp.exp2(s - m)
    l = jnp.sum(p, axis=-1, keepdims=True)
    return _dot(p.astype(v2.dtype), v2), m, l


def _attn_kernel(q0, k0, v0, q1, k1, v1, q2, k2, v2, bias_ref, *refs, seq, n_cast):
    cast_src, refs = refs[:n_cast], refs[n_cast:]
    o_ref, cast_dst = refs[0], refs[1:1 + n_cast]
    o_scr, m_scr, l_scr, stage_scr = refs[1 + n_cast:]
    blk = N_STEPS
    s1 = RELAYOUT_STRIDE

    for src, dst in zip(cast_src, cast_dst):
        dst[...] = src[...].astype(dst.dtype)

    def run_group(g, qr, kr, vr):
        d = DILATIONS[g]
        nblk = seq // d // blk

        def one_block(r, i):
            own = pl.multiple_of(i * blk, blk)
            prev = pl.multiple_of(jnp.maximum(i - 1, 0) * blk, blk)
            if d == 1:
                ld = lambda ref, start: ref[pl.ds(start, blk), :]
            else:
                ld = lambda ref, start: ref[0, r, pl.ds(start, blk), :]
            k2_ = jnp.concatenate([ld(kr, prev), ld(kr, own)], axis=0)
            v2_ = jnp.concatenate([ld(vr, prev), ld(vr, own)], axis=0)
            o, m, l = _attn_block(ld(qr, own), k2_, v2_, bias_ref[g, 0, jnp.minimum(i, 1)])
            m = jnp.broadcast_to(m, (blk, LANES))
            l = jnp.broadcast_to(l, (blk, LANES))
            if 1 < d <= s1:
                rows = pl.ds(i * (blk * d) + r, blk, stride=d)
                o_scr[g - 1, rows, :] = o
                m_scr[g - 1, rows, :] = m
                l_scr[g - 1, rows, :] = l
                return
            if d > s1:
                s2 = d // s1
                rows = pl.ds((r % s1) * (seq // s1) + i * (blk * s2) + r // s1, blk, stride=s2)
                stage_scr[0, rows, :] = o
                stage_scr[1, rows, :] = m
                stage_scr[2, rows, :] = l
                return
            rows = pl.ds(own, blk)
            ms = [m, m_scr[0, rows, :], m_scr[1, rows, :]]
            mx = jnp.maximum(jnp.maximum(ms[0], ms[1]), ms[2])
            ws = [jnp.exp2(x - mx) for x in ms]
            num = ws[0] * o + ws[1] * o_scr[0, rows, :] + ws[2] * o_scr[1, rows, :]
            den = ws[0] * l + ws[1] * l_scr[0, rows, :] + ws[2] * l_scr[1, rows, :]
            o_ref[rows, :] = (num / den).astype(o_ref.dtype)

        ur = min(ATTN_UNROLL, d)
        ui = ATTN_UNROLL // ur
        assert d % ur == 0 and nblk % ui == 0

        def body(it, carry):
            i0 = (it // (d // ur)) * ui
            r0 = (it % (d // ur)) * ur
            for a in range(ui):
                for b in range(ur):
                    one_block(r0 + b, i0 + a)
            return carry

        jax.lax.fori_loop(0, (d // ur) * (nblk // ui), body, 0)

    run_group(2, q2, k2, v2)
    run_group(1, q1, k1, v1)
    g = N_GROUPS - 1
    assert DILATIONS[g] > s1 and all(d <= s1 for d in DILATIONS[:g])
    per_stream = seq // s1
    for a, dst in enumerate((o_scr, m_scr, l_scr)):
        for r1 in range(s1):
            for c in range(per_stream // blk):
                dst[g - 1, pl.ds(s1 * c * blk + r1, blk, stride=s1), :] = (
                    stage_scr[a, r1 * per_stream + c * blk:r1 * per_stream + (c + 1) * blk, :])
    run_group(0, q0, k0, v0)


def _attn_call(qkv, bias, casts, *, batch, seq):
    cast_in, cast_out, cast_shape = _cast_specs(casts, batch * HEADS, lambda b, h: b * HEADS + h)
    in_specs = []
    for g in range(N_GROUPS):
        d = DILATIONS[g]
        if d == 1:
            spec = pl.BlockSpec((seq, HEAD_DIM), lambda b, h: (b, h))
        else:
            spec = pl.BlockSpec((1, d, seq // d, HEAD_DIM), lambda b, h: (b, 0, 0, h))
        in_specs += [spec] * 3
    in_specs.append(pl.BlockSpec((N_GROUPS, 1, 2, N_STEPS, 2 * N_STEPS), lambda b, h: (0, h, 0, 0, 0)))
    return pl.pallas_call(
        functools.partial(_attn_kernel, seq=seq, n_cast=len(casts)),
        grid=(batch, HEADS),
        in_specs=in_specs + cast_in,
        out_specs=[pl.BlockSpec((seq, HEAD_DIM), lambda b, h: (b, h))] + cast_out,
        out_shape=[jax.ShapeDtypeStruct((batch * seq, GROUP_COLS), BF16)] + cast_shape,
        scratch_shapes=[pltpu.VMEM((N_GROUPS - 1, seq, LANES), F32)] * 3
                       + [pltpu.VMEM((3, seq, LANES), F32)],
        compiler_params=_params(2),
        name="attn_prompt",
    )(*qkv, bias, *[w for w, _ in casts])


def _attn_sample_kernel(q0, q1, q2, n0, n1, n2, c0, c1, c2, slope_ref, o_ref, *, bb):
    q_refs, n_refs, c_refs = (q0, q1, q2), (n0, n1, n2), (c0, c1, c2)
    half = N_STEPS // 2
    tile = (2 * HEADS, HEAD_DIM)
    top = jax.lax.broadcasted_iota(jnp.int32, tile, 0) < HEADS
    steps_a = (N_STEPS - jax.lax.broadcasted_iota(jnp.int32, (half, 1, 1), 0)).astype(F32)
    back = jnp.where(top[:, 0:1], steps_a, steps_a - float(half))

    def both_halves(t):
        return jnp.where(top, t, pltpu.roll(t, HEADS, axis=0))

    for b in range(bb):
        outs, lses = [], []
        for g in range(N_GROUPS):
            q8, kv8 = q_refs[g][b], n_refs[g][b]
            xa, xb = c_refs[g][b, 0:half], c_refs[g][b, half:N_STEPS]
            slope = both_halves(slope_ref[g])[:, 0:1]
            bias = (back * (-float(DILATIONS[g]))) * slope
            prod = jnp.where(top, xa * q8, pltpu.roll(xb * q8, HEADS, axis=1))
            s = jnp.sum(prod, axis=-1, keepdims=True) + bias
            s_self = both_halves(jnp.broadcast_to(jnp.sum(kv8 * q8, axis=-1, keepdims=True), tile))
            m = jnp.broadcast_to(jnp.max(s, axis=0), tile)
            m = jnp.maximum(jnp.maximum(m, pltpu.roll(m, HEADS, axis=0)), s_self)
            p = jnp.exp2(s - m[:, 0:1])
            p_self = jnp.exp2(s_self - m)
            l = jnp.broadcast_to(jnp.sum(p, axis=0), tile)
            l = l + pltpu.roll(l, HEADS, axis=0) + p_self
            p_b = jnp.broadcast_to(p, xb.shape)
            acc = jnp.sum(pltpu.roll(p_b, HEADS, axis=1) * xa + p_b * xb, axis=0) + p_self * kv8
            outs.append(acc / l)
            lses.append(m + jnp.log(l) * LOG2E)
        mx = jnp.maximum(jnp.maximum(lses[0], lses[1]), lses[2])
        ws = [jnp.exp2(v - mx) for v in lses]
        num = ws[0] * outs[0] + ws[1] * outs[1] + ws[2] * outs[2]
        o8 = num / (ws[0] + ws[1] + ws[2])
        for hh in range(HEADS):
            o_ref[b:b + 1, hh * HEAD_DIM:(hh + 1) * HEAD_DIM] = o8[HEADS + hh:HEADS + hh + 1, :]


def _attn_sample_call(q_tiles, new_tiles, caches, layer, slopes8):
    bd = q_tiles[0].shape[0]
    bb = SAMPLE_ATTN_SEQS
    nb = bd // bb
    tile = (2 * HEADS, HEAD_DIM)
    in_specs = [pl.BlockSpec((bb,) + tile, lambda i: (i, 0, 0))] * 3
    in_specs += [pl.BlockSpec((bb,) + tile, lambda i: (layer * nb + i, 0, 0))] * 3
    in_specs += [pl.BlockSpec((None, bb, N_STEPS) + tile, lambda i: (layer, i, 0, 0, 0))] * 3
    in_specs.append(pl.BlockSpec((N_GROUPS,) + tile, lambda i: (0, 0, 0)))
    return pl.pallas_call(
        functools.partial(_attn_sample_kernel, bb=bb),
        grid=(nb,),
        in_specs=in_specs,
        out_specs=pl.BlockSpec((bb, GROUP_COLS), lambda i: (i, 0)),
        out_shape=jax.ShapeDtypeStruct((bd, GROUP_COLS), F32),
        compiler_params=_params(1),
        name="attn_sample",
    )(*q_tiles, *new_tiles, *caches, slopes8)


def _post_math(x, mod, ya, w, ws_ref, bs_ref, mix_scr, *, chunk_rows, d_model, d_ff):
    g1 = mod[:, 2 * d_model:3 * d_model]
    g2 = mod[:, 5 * d_model:6 * d_model]

    h = _modulated_norm(x, w["n1w"][...], mod, 0, d_model).astype(BF16)
    sw = SGU_WIDTH
    wg1, wg2 = w["wg1"], w["wg2"]
    u = _gelu(_dot(h, wg1[:, 0:sw]))
    vs = _gelu(_dot(h, wg1[:, sw:2 * sw]))
    mu = jnp.mean(vs, axis=-1, keepdims=True)
    vc = vs - mu
    var = jnp.mean(vc * vc, axis=-1, keepdims=True)
    vs = vc * jax.lax.rsqrt(var + EPS) * w["lnw"][...] + w["lnb"][...]
    half = d_model // 2
    ga = jnp.concatenate([_sigmoid(_dot(h, wg1[:, 2 * sw:2 * sw + half])),
                          _sigmoid(_dot(h, wg2[:, 0:half]))], axis=-1)
    gb = _sigmoid(_dot(h, wg2[:, half:half + d_model]))

    if chunk_rows is not None:
        row = jax.lax.broadcasted_iota(jnp.int32, (CHUNK, CHUNK), 0)
        col = jax.lax.broadcasted_iota(jnp.int32, (CHUNK, CHUNK), 1)
        tril = col <= row
        lane = jax.lax.broadcasted_iota(jnp.int32, (CHUNK, LANES), 1)
        low = lane < SGU_GROUP_DIM
        wms = [jnp.where(tril, ws_ref[gi], 0.0).astype(BF16) for gi in range(SGU_GROUPS)]
        vs_b = vs.astype(BF16)
        for c in range(chunk_rows // CHUNK):
            rows = slice(c * CHUNK, (c + 1) * CHUNK)
            for j in range(SGU_WIDTH // LANES):
                cols = slice(j * LANES, (j + 1) * LANES)
                v = vs_b[rows, cols]
                mix_scr[rows, cols] = jnp.where(low, _dot(wms[2 * j], v), _dot(wms[2 * j + 1], v))
        mixed = mix_scr[...] + jnp.concatenate([bs_ref[...]] * (chunk_rows // CHUNK), axis=0)
    else:
        mixed = vs * ws_ref[...] + bs_ref[...]

    ys = (u * mixed).astype(BF16)
    xa = _dot(ya.astype(BF16), w["wpa"][...])
    xs = _dot(ys, w["wps"][...])
    merged = (ga * xa + gb * xs).astype(BF16)
    x1 = x + g1 * _dot(merged, w["wo"][...])
    h2 = _modulated_norm(x1, w["n2w"][...], mod, 1, d_model).astype(BF16)
    a = _dot(h2, w["wfi"][:, 0:d_ff])
    b = _dot(h2, w["wfi"][:, d_ff:2 * d_ff])
    act = (a * _sigmoid(a) * b).astype(BF16)
    return x1 + g2 * _dot(act, w["wfo"][...]), vs


_POST_WEIGHTS = ("n1w", "wg1", "wg2", "lnw", "lnb", "wpa", "wps", "wo", "n2w", "wfi", "wfo")


def _post_kernel(x_ref, mod_ref, ya_ref, ws_ref, bs_ref, xs_ref, mods_ref, yas_ref, wss_ref, bss_ref,
                 *refs, tm, d_model, d_ff):
    n_w = len(_POST_WEIGHTS)
    w = dict(zip(_POST_WEIGHTS, refs[:n_w]))
    o_ref, os_ref, vs_ref, mix_scr = refs[n_w:]
    o_ref[...], _ = _post_math(x_ref[...], mod_ref[0], ya_ref[...], w, ws_ref, bs_ref, mix_scr,
                               chunk_rows=tm, d_model=d_model, d_ff=d_ff)

    @pl.when(pl.program_id(0) == pl.num_programs(0) - 1)
    def _():
        os_ref[...], vs_ref[...] = _post_math(xs_ref[...], mods_ref[0], yas_ref[...], w, wss_ref,
                                              bss_ref, None, chunk_rows=None,
                                              d_model=d_model, d_ff=d_ff)


def _post_call(x, mod, ya, ws, bs, xs, mod_s, ya_s, ws_s, bs_s,
               n1w, w_in, lnw, lnb, wpa, wps, wo, n2w, wfi, wfo, *, seq, tm, layer):
    t, d_model = x.shape
    ts = xs.shape[0]
    d_ff = wfo.shape[0]
    tiles_per_seq = seq // tm
    gblk = QKV_COLS
    assert 2 * SGU_WIDTH + d_model // 2 == gblk and w_in.shape[1] == 5 * gblk

    def w_gate(j):
        return pl.BlockSpec((d_model, gblk), lambda i: (0, 3 + j), pipeline_mode=pl.Buffered(1))

    def tok(cols):
        return pl.BlockSpec((tm, cols), lambda i: (i, 0))

    weight_specs = [_resident(n1w.shape), w_gate(0), w_gate(1), _resident(lnw.shape),
                    _resident(lnb.shape), _resident_layer(wpa.shape, layer),
                    _resident_layer(wps.shape, layer), _resident_layer(wo.shape, layer),
                    _resident(n2w.shape), _resident(wfi.shape), _resident(wfo.shape)]
    assert len(weight_specs) == len(_POST_WEIGHTS)
    kern = functools.partial(_post_kernel, tm=tm, d_model=d_model, d_ff=d_ff)
    return pl.pallas_call(
        kern,
        grid=(t // tm,),
        in_specs=[tok(d_model),
                  pl.BlockSpec((1,) + mod.shape[1:], lambda i: (i // tiles_per_seq, 0, 0)),
                  tok(GROUP_COLS), _resident(ws.shape), _resident(bs.shape),
                  _resident(xs.shape), _resident(mod_s.shape), _resident(ya_s.shape),
                  _resident(ws_s.shape), _resident(bs_s.shape)] + weight_specs,
        out_specs=[tok(d_model),
                   pl.BlockSpec((ts, d_model), lambda i: (0, 0)),
                   pl.BlockSpec((ts, SGU_WIDTH), lambda i: (0, 0))],
        out_shape=[jax.ShapeDtypeStruct((t, d_model), F32),
                   jax.ShapeDtypeStruct((ts, d_model), F32),
                   jax.ShapeDtypeStruct((ts, SGU_WIDTH), F32)],
        scratch_shapes=[pltpu.VMEM((tm, SGU_WIDTH), F32)],
        compiler_params=_params(1),
        name="post",
    )(x, mod, ya, ws, bs, xs, mod_s, ya_s, ws_s, bs_s,
      n1w, w_in, w_in, lnw, lnb, wpa, wps, wo, n2w, wfi, wfo)


def kernel(x_prompt, x_sample, cache_kv_w128, cache_kv_w512, cache_kv_w2048, c_prompt, c_sample,
           w_ada, b_ada, norm1_w, w_in, q_norm_w, k_norm_w, sgu_ln_w, sgu_ln_b, w_spatial, b_spatial,
           w_proj_att, w_proj_sgu, w_out, norm2_w, w_ffn_in, w_ffn_out):
    batch, seq, d_model = x_prompt.shape
    bd, dec_seq, _ = x_sample.shape
    depth = w_in.shape[0]
    assert dec_seq == 1 and seq % (N_STEPS * DILATIONS[-1]) == 0
    caches_in = (cache_kv_w128, cache_kv_w512, cache_kv_w2048)
    for g in range(N_GROUPS):
        assert caches_in[g].shape[2] == WINDOWS[g]

    n_c = batch + bd
    c_all = jnp.concatenate([c_prompt, c_sample], axis=0)
    c_all = jnp.pad(c_all, ((0, -n_c % 16), (0, 0)))
    mod, w_in_l = _ada_call(c_all, w_ada, b_ada, [(w_in, 0)])

    band_bias = jnp.asarray(_band_bias())
    tile = (2 * HEADS, HEAD_DIM)
    slopes8 = np.zeros((N_GROUPS,) + tile, np.float32)
    slopes8[:, :HEADS, :] = _alibi_slopes()[:, :, None]
    slopes8 = jnp.asarray(slopes8)
    caches = [caches_in[g].reshape((depth, bd, N_STEPS, DILATIONS[g] * tile[0], tile[1]))
              for g in range(N_GROUPS)]

    xp = x_prompt.reshape(batch * seq, d_model)
    xs = x_sample.reshape(bd, d_model)
    p_states = s_states = None
    s_v = []
    wpa, wps, wo = w_proj_att.astype(BF16), w_proj_sgu.astype(BF16), w_out.astype(BF16)
    for l in range(depth):
        qw = jnp.tile(q_norm_w[l][:, None, :], (1, HEADS, 1)).reshape(1, QKV_COLS)
        kw = jnp.tile(k_norm_w[l][:, None, :], (1, HEADS, 1)).reshape(1, QKV_COLS)
        lnw, lnb = sgu_ln_w[l].reshape(1, -1), sgu_ln_b[l].reshape(1, -1)
        n1w, n2w = norm1_w[l].reshape(1, -1), norm2_w[l].reshape(1, -1)
        mod_p = mod[l, :batch].reshape(batch, 1, -1)
        mod_s = mod[l, batch:n_c].reshape(1, bd, -1)
        bs_p = jnp.repeat(b_spatial[l].T, SGU_GROUP_DIM, axis=1)
        ws_s = jnp.repeat(w_spatial[l][:, 0, 0], SGU_GROUP_DIM).reshape(1, SGU_WIDTH)
        bs_s = jnp.repeat(b_spatial[l][:, 0], SGU_GROUP_DIM).reshape(1, SGU_WIDTH)

        outs = _inproj_call(xp, mod_p, n1w, w_in_l, qw, kw,
                            seq=seq, tm=INPROJ_TILE_ROWS, deinterleave=True, act_dtype=BF16,
                            layer=l, depth=depth, prev_states=p_states)
        qkv, p_states = outs[:9], outs[9:]
        casts = [(w_ffn_in, l), (w_ffn_out, l)] + ([(w_in, l + 1)] if l + 1 < depth else [])
        ya, wfi, wfo, *w_in_next = _attn_call(qkv, band_bias, casts, batch=batch, seq=seq)

        outs = _inproj_call(xs, mod_s, n1w, w_in_l, qw, kw,
                            seq=bd, tm=bd, deinterleave=False, act_dtype=F32,
                            layer=l, depth=depth, prev_states=s_states)
        q_tiles, s_states = [o.reshape((bd,) + tile) for o in outs[:3]], outs[3:]
        new_tiles = [s.reshape((depth * bd,) + tile) for s in s_states]
        ya_s = _attn_sample_call(q_tiles, new_tiles, caches, l, slopes8)

        xp, xs, vs = _post_call(xp, mod_p, ya, w_spatial[l], bs_p, xs, mod_s, ya_s, ws_s, bs_s,
                                n1w, w_in_l, lnw, lnb, wpa, wps, wo, n2w, wfi, wfo,
                                seq=seq, tm=POST_TILE_ROWS, layer=l)
        s_v.append(vs.reshape(bd, 1, SGU_WIDTH))
        if w_in_next:
            (w_in_l,) = w_in_next

    p_states =[s.reshape(depth, batch, -1, 2, HEADS, HEAD_DIM) for s in p_states]
    s_states = [s.reshape(depth, bd, 1, 2, HEADS, HEAD_DIM) for s in s_states]
    return (xp.reshape(batch, seq, d_model), xs.reshape(bd, 1, d_model),
            p_states[0], p_states[1], p_states[2], s_states[0], s_states[1], s_states[2],
            jnp.stack(s_v))
```

```python
import functools
import math

import jax
import jax.numpy as jnp
import numpy as np
from jax.experimental import pallas as pl
from jax.experimental.pallas import tpu as pltpu

HEAD_DIM = 128
N_GROUPS = 3
HEADS = 4
GROUP_COLS = HEADS * HEAD_DIM
QKV_COLS = N_GROUPS * GROUP_COLS
WINDOWS = (128, 512, 2048)
DILATIONS = (1, 4, 16)
N_STEPS = 128
CHUNK = 128
SGU_WIDTH = 512
SGU_GROUPS = 8
SGU_GROUP_DIM = SGU_WIDTH // SGU_GROUPS
N_ADA = 6
EPS = 1e-6
NEG_INF = -1e30
LOG2E = math.log2(math.e)
QK_SCALE = HEAD_DIM ** -0.5 * LOG2E

LANES = 128
RELAYOUT_STRIDE = 4
VMEM_LIMIT_BYTES = 56 * 1024 * 1024

INPROJ_TILE_ROWS = 512
POST_TILE_ROWS = 512
ADALN_TILE_COLS = 1536
SAMPLE_ATTN_SEQS = 8
ATTN_UNROLL = 32

BF16 = jnp.bfloat16
F32 = jnp.float32


def _params(n_axes):
    return pltpu.CompilerParams(dimension_semantics=("arbitrary",) * n_axes,
                                vmem_limit_bytes=VMEM_LIMIT_BYTES)


def _resident(shape):
    nd = len(shape)
    return pl.BlockSpec(shape, lambda *_: (0,) * nd, pipeline_mode=pl.Buffered(1))


def _resident_layer(shape, layer):
    nd = len(shape)
    return pl.BlockSpec((None,) + tuple(shape[1:]), lambda *_: (layer,) + (0,) * (nd - 1),
                        pipeline_mode=pl.Buffered(1))


def _dot(a, b):
    return jnp.dot(a, b, preferred_element_type=F32)


def _dot_nt(a, b):
    return jax.lax.dot_general(a, b, (((1,), (1,)), ((), ())), preferred_element_type=F32)


def _rms(x):
    return x * jax.lax.rsqrt(jnp.mean(x * x, axis=-1, keepdims=True) + EPS)


def _gelu(x):
    return 0.5 * x * (1.0 + jax.lax.erf(x * (2.0 ** -0.5)))


def _sigmoid(x):
    return 1.0 / (1.0 + jnp.exp(-x))


def _cast_specs(casts, n_steps, step):
    ins, outs, shapes = [], [], []
    for w, layer in casts:
        _, k_dim, n_dim = w.shape
        band = k_dim // n_steps
        assert band * n_steps == k_dim and band % 16 == 0
        ins.append(pl.BlockSpec((None, band, n_dim), lambda *g, layer=layer: (layer, step(*g), 0)))
        outs.append(pl.BlockSpec((band, n_dim), lambda *g: (step(*g), 0)))
        shapes.append(jax.ShapeDtypeStruct((k_dim, n_dim), BF16))
    return ins, outs, shapes


def _ada_kernel(c_ref, w_ref, b_ref, *refs):
    n_cast = (len(refs) - 1) // 2
    o_ref = refs[n_cast]
    for src, dst in zip(refs[:n_cast], refs[n_cast + 1:]):
        dst[...] = src[...].astype(dst.dtype)
    c = c_ref[...]
    a = c * _sigmoid(c)
    w = w_ref[...]
    a_hi = a.astype(BF16)
    a_lo = (a - a_hi.astype(F32)).astype(BF16)
    w_hi = w.astype(BF16)
    w_lo = (w - w_hi.astype(F32)).astype(BF16)
    rows = a.shape[0]
    both = _dot(jnp.concatenate([a_hi, a_lo], axis=0), w_hi)
    o_ref[...] = both[:rows] + both[rows:] + _dot(a_hi, w_lo) + b_ref[...]


def _ada_call(c_all, w_ada, b_ada, casts):
    depth, d, n = w_ada.shape
    rows = c_all.shape[0]
    tn = ADALN_TILE_COLS
    per_layer = n // tn
    cast_in, cast_out, cast_shape = _cast_specs(casts, depth * per_layer, lambda l, j: l * per_layer + j)
    return pl.pallas_call(
        _ada_kernel,
        grid=(depth, per_layer),
        in_specs=[pl.BlockSpec((rows, d), lambda l, j: (0, 0)),
                  pl.BlockSpec((None, d, tn), lambda l, j: (l, 0, j)),
                  pl.BlockSpec((None, 1, tn), lambda l, j: (l, 0, j))] + cast_in,
        out_specs=[pl.BlockSpec((None, rows, tn), lambda l, j: (l, 0, j))] + cast_out,
        out_shape=[jax.ShapeDtypeStruct((depth, rows, n), F32)] + cast_shape,
        compiler_params=_params(2),
        name="adaln_mod",
    )(c_all, w_ada, b_ada.reshape(depth, 1, n), *[w for w, _ in casts])


def _modulated_norm(x, w, mod, which, d_model):
    shift = mod[:, 3 * which * d_model:(3 * which + 1) * d_model]
    scale = mod[:, (3 * which + 1) * d_model:(3 * which + 2) * d_model]
    return _rms(x) * w * (1.0 + scale) + shift


def _inproj_kernel(x_ref, mod_ref, n1w_ref, w_ref, qw_ref, kw_ref,
                   *refs, tm, d_model, dils, state_rows, tiles_per_seq, deinterleave, n_aliased):
    refs = refs[n_aliased:]
    if deinterleave:
        q_refs, k_refs, v_refs, st_refs = refs[0:9:3], refs[1:9:3], refs[2:9:3], refs[9:12]
        h_scr, slab, slab2, kv_stash = refs[12:]
    else:
        q_refs, st_refs = refs[0:3], refs[3:6]
        h_scr, slab, slab2, kv_stash = refs[6:]
    ti = pl.program_id(0) % tiles_per_seq
    tile_rows = 2 * HEADS

    h = _modulated_norm(x_ref[...], n1w_ref[...], mod_ref[0], 0, d_model)
    h_scr[...] = h.astype(BF16)

    def proj(c0, width):
        return _dot(h_scr[...], w_ref[:, c0:c0 + width])

    def put_tiles(ref, first_row, val):
        for hh in range(HEADS):
            ref[pl.ds(first_row + hh, tm, stride=tile_rows), :] = val[:, hh * HEAD_DIM:(hh + 1) * HEAD_DIM]

    def put(ref, g, val, slot, slabs=None):
        d = dils[g]
        if d == 1:
            ref[...] = val.astype(ref.dtype)
            return
        if slabs is None:
            slabs = [slab.at[g - 1, hh] for hh in range(HEADS)]
            for hh in range(HEADS):
                slabs[hh][...] = val[:, hh * HEAD_DIM:(hh + 1) * HEAD_DIM]
        if d <= RELAYOUT_STRIDE:
            for r in range(d):
                for hh in range(HEADS):
                    ref[0, r, :, hh * HEAD_DIM:(hh + 1) * HEAD_DIM] = (
                        slabs[hh][pl.ds(r, tm // d, stride=d), :].astype(ref.dtype))
            return
        s1, s2 = RELAYOUT_STRIDE, d // RELAYOUT_STRIDE
        part = tm // s1
        for hh in range(HEADS):
            for r1 in range(s1):
                slab2[slot, hh,r1 * part:(r1 + 1) * part, :] = slabs[hh][pl.ds(r1, part, stride=s1), :]
            for r1 in range(s1):
                for r2 in range(s2):
                    ref[0, r1 + s1 * r2, :, hh * HEAD_DIM:(hh + 1) * HEAD_DIM] = (
                        slab2[slot, hh,pl.ds(r1 * part + r2, tm // d, stride=s2), :].astype(ref.dtype))

    def head_rms(z, w):
        parts = [_rms(z[:, hh * HEAD_DIM:(hh + 1) * HEAD_DIM]) for hh in range(HEADS)]
        return jnp.concatenate(parts, axis=-1) * w

    def put_state(g, kn, v):
        if state_rows[g] is None:
            put_tiles(st_refs[g], 0, kn)
            put_tiles(st_refs[g], HEADS, v)
        else:
            for hh in range(HEADS):
                cols = slice(hh * HEAD_DIM, (hh + 1) * HEAD_DIM)
                kv_stash[g, hh] = kn[:, cols]
                kv_stash[g, HEADS + hh] = v[:, cols]

    def flush_state(g):
        st, rows = st_refs[g], min(state_rows[g], tm)
        first = tiles_per_seq - max(state_rows[g] // tm, 1)

        @pl.when(ti >= first)
        def _():
            for j in range(2 * HEADS):
                st[pl.ds(j, rows, stride=2 * HEADS), :] = kv_stash[g, j, tm - rows:, :]

    for g in reversed(range(N_GROUPS)):
        c0 = g * GROUP_COLS
        q = head_rms(proj(c0, GROUP_COLS), qw_ref[:, c0:c0 + GROUP_COLS] * QK_SCALE)
        kn = head_rms(proj(QKV_COLS + c0, GROUP_COLS), kw_ref[:, c0:c0 + GROUP_COLS])
        v = proj(2 * QKV_COLS + c0, GROUP_COLS)
        put_state(g, kn, v)
        if not deinterleave:
            put_tiles(q_refs[g], 0, q)
            put_tiles(q_refs[g], HEADS, jnp.zeros_like(q))
            continue
        put(q_refs[g], g, q, 0)
        put(k_refs[g], g, kn, 1, [kv_stash.at[g, hh] for hh in range(HEADS)])
        put(v_refs[g], g, v, 2, [kv_stash.at[g, HEADS + hh] for hh in range(HEADS)])

    for g in range(N_GROUPS):
        if state_rows[g] is not None:
            flush_state(g)


def _inproj_call(x, mod, n1w, w_in, qw, kw, *, seq, tm, deinterleave, act_dtype,
                 layer=0, depth=1, prev_states=None):
    t, d_model = x.shape
    batch = t // seq
    tiles_per_seq = seq // tm

    def tok(cols, dtype):
        return (jax.ShapeDtypeStruct((t, cols), dtype), pl.BlockSpec((tm, cols), lambda i: (i, 0)))

    def grp(g, dtype):
        d = DILATIONS[g]
        if d == 1:
            return tok(GROUP_COLS, dtype)
        assert tm % d == 0 and (tm // d) % 16 == 0
        return (jax.ShapeDtypeStruct((batch, d, seq // d, GROUP_COLS), dtype),
                pl.BlockSpec((1, d, tm // d, GROUP_COLS),
                             lambda i: (i // tiles_per_seq, 0, i % tiles_per_seq, 0)))

    tile_rows = 2 * HEADS
    outs = []
    for g in range(N_GROUPS):
        if deinterleave:
            outs += [grp(g, act_dtype)] * 3
        else:
            assert t == tm
            outs.append((jax.ShapeDtypeStruct((t * tile_rows, LANES), F32),
                         pl.BlockSpec((t * tile_rows, LANES), lambda i: (0, 0))))

    state_rows = []
    for g in range(N_GROUPS):
        if not deinterleave:
            state_rows.append(None)
            outs.append((jax.ShapeDtypeStruct((depth * t * tile_rows, LANES), F32),
                         pl.BlockSpec((t * tile_rows, LANES), lambda i: (layer, 0))))
            continue
        rows = min(WINDOWS[g], seq)
        state_rows.append(rows)
        blk_rows = min(rows, tm)
        assert rows % blk_rows == 0
        nblk = rows // blk_rows
        shape = jax.ShapeDtypeStruct((depth * batch * rows * 2 * HEADS, LANES), F32)
        spec = pl.BlockSpec(
            (blk_rows * 2 * HEADS, LANES),
            lambda i, nblk=nblk: ((layer * batch + i // tiles_per_seq) * nblk
                                  + jnp.maximum(i % tiles_per_seq - (tiles_per_seq - nblk), 0), 0))
        outs.append((shape, spec))

    aliased = list(prev_states) if prev_states is not None else []
    n_in = 6
    w_qkv = pl.BlockSpec((d_model, 3 * QKV_COLS), lambda i: (0, 0),
                         pipeline_mode=pl.Buffered(1))
    kern = functools.partial(_inproj_kernel, tm=tm, d_model=d_model, dils=DILATIONS,
                             state_rows=tuple(state_rows), tiles_per_seq=tiles_per_seq,
                             deinterleave=deinterleave, n_aliased=len(aliased))
    return pl.pallas_call(
        kern,
        grid=(t // tm,),
        in_specs=[pl.BlockSpec((tm, d_model), lambda i: (i, 0)),
                  pl.BlockSpec((1,) + mod.shape[1:], lambda i: (i // tiles_per_seq, 0, 0)),
                  _resident_layer(n1w.shape, layer),
                  w_qkv,
                  _resident_layer(qw.shape, layer), _resident_layer(kw.shape, layer)]
                 + [pl.BlockSpec(memory_space=pl.ANY)] * len(aliased),
        out_specs=[o[1] for o in outs],
        out_shape=[o[0] for o in outs],
        input_output_aliases={n_in + g: len(outs) - N_GROUPS + g for g in range(len(aliased))},
        scratch_shapes=[pltpu.VMEM((tm, d_model), BF16),
                        pltpu.VMEM((N_GROUPS - 1, HEADS, tm, LANES), F32),
                        pltpu.VMEM((3, HEADS, tm, LANES), F32),
                        pltpu.VMEM((N_GROUPS, 2 * HEADS, tm, LANES), F32)],
        compiler_params=_params(1),
        name="inproj_deint" if deinterleave else "inproj_tok",
    )(x, mod, n1w, w_in, qw, kw, *aliased)


def _alibi_slopes():
    h = np.arange(1, N_GROUPS * HEADS + 1, dtype=np.float32)
    s = np.power(np.float32(2.0), -8.0 * h / (N_GROUPS * HEADS)).astype(np.float32)
    return (s.astype(np.float64) * LOG2E).astype(np.float32).reshape(N_GROUPS, HEADS)


def _band_bias():
    qi = np.arange(N_STEPS)[:, None]
    ki = np.arange(2 * N_STEPS)[None, :]
    dist = N_STEPS + qi - ki
    valid = (dist >= 0) & (dist <= N_STEPS)
    slopes = _alibi_slopes()
    out = np.empty((N_GROUPS, HEADS, 2, N_STEPS, 2 * N_STEPS), np.float32)
    for g in range(N_GROUPS):
        for hh in range(HEADS):
            b = np.where(valid, -slopes[g, hh] * (dist * DILATIONS[g]).astype(np.float32),
                         np.float32(NEG_INF)).astype(np.float32)
            out[g, hh, 1] = b
            first = b.copy()
            first[:, :N_STEPS] = NEG_INF
            out[g, hh, 0] = first
    return out


def _attn_block(q, k2, v2, bias):
    s = _dot_nt(q, k2) + bias
    m = jnp.max(s, axis=-1, keepdims=True)
    p = jnp.exp2(s - m)
    l = jnp.sum(p, axis=-1, keepdims=True)
    return _dot(p.astype(v2.dtype), v2), m, l


def _attn_kernel(q0, k0, v0, q1, k1, v1, q2, k2, v2, bias_ref, *refs, seq, n_cast):
    cast_src, refs = refs[:n_cast], refs[n_cast:]
    o_ref, cast_dst = refs[0], refs[1:1 + n_cast]
    o_scr, m_scr, l_scr, stage_scr = refs[1 + n_cast:]
    blk = N_STEPS
    s1 = RELAYOUT_STRIDE

    for src, dst in zip(cast_src, cast_dst):
        dst[...] = src[...].astype(dst.dtype)

    def run_group(g, qr, kr, vr):
        d = DILATIONS[g]
        nblk = seq // d // blk

        def one_block(r, i):
            own = pl.multiple_of(i * blk, blk)
            prev = pl.multiple_of(jnp.maximum(i - 1, 0) * blk, blk)
            if d == 1:
                ld = lambda ref, start: ref[pl.ds(start, blk), :]
            else:
                ld = lambda ref, start: ref[0, r, pl.ds(start, blk), :]
            k2_ = jnp.concatenate([ld(kr, prev), ld(kr, own)], axis=0)
            v2_ = jnp.concatenate([ld(vr, prev), ld(vr, own)], axis=0)
            o, m, l = _attn_block(ld(qr, own), k2_, v2_, bias_ref[g, 0, jnp.minimum(i, 1)])
            m = jnp.broadcast_to(m, (blk, LANES))
            l = jnp.broadcast_to(l, (blk, LANES))
            if 1 < d <= s1:
                rows = pl.ds(i * (blk * d) + r, blk, stride=d)
                o_scr[g - 1, rows, :] = o
                m_scr[g - 1, rows, :] = m
                l_scr[g - 1, rows, :] = l
                return
            if d > s1:
                s2 = d // s1
                rows = pl.ds((r % s1) * (seq // s1) + i * (blk * s2) + r // s1, blk, stride=s2)
                stage_scr[0, rows, :] = o
                stage_scr[1, rows, :] = m
                stage_scr[2, rows, :] = l
                return
            rows = pl.ds(own, blk)
            ms = [m, m_scr[0, rows, :], m_scr[1, rows, :]]
            mx = jnp.maximum(jnp.maximum(ms[0], ms[1]), ms[2])
            ws = [jnp.exp2(x - mx) for x in ms]
            num = ws[0] * o + ws[1] * o_scr[0, rows, :] + ws[2] * o_scr[1, rows, :]
            den = ws[0] * l + ws[1] * l_scr[0, rows, :] + ws[2] * l_scr[1, rows, :]
            o_ref[rows, :] = (num / den).astype(o_ref.dtype)

        ur = min(ATTN_UNROLL, d)
        ui = ATTN_UNROLL // ur
        assert d % ur == 0 and nblk % ui == 0

        def body(it, carry):
            i0 = (it // (d // ur)) * ui
            r0 = (it % (d // ur)) * ur
            for a in range(ui):
                for b in range(ur):
                    one_block(r0 + b, i0 + a)
            return carry

        jax.lax.fori_loop(0, (d // ur) * (nblk // ui), body, 0)

    run_group(2, q2, k2, v2)
    run_group(1, q1, k1, v1)
    g = N_GROUPS - 1
    assert DILATIONS[g] > s1 and all(d <= s1 for d in DILATIONS[:g])
    per_stream = seq // s1
    for a, dst in enumerate((o_scr, m_scr, l_scr)):
        for r1 in range(s1):
            for c in range(per_stream // blk):
                dst[g - 1, pl.ds(s1 * c * blk + r1, blk, stride=s1), :] = (
                    stage_scr[a, r1 * per_stream + c * blk:r1 * per_stream + (c + 1) * blk, :])
    run_group(0, q0, k0, v0)


def _attn_call(qkv, bias, casts, *, batch, seq):
    cast_in, cast_out, cast_shape = _cast_specs(casts, batch * HEADS, lambda b, h: b * HEADS + h)
    in_specs = []
    for g in range(N_GROUPS):
        d = DILATIONS[g]
        if d == 1:
            spec = pl.BlockSpec((seq, HEAD_DIM), lambda b, h: (b, h))
        else:
            spec = pl.BlockSpec((1, d, seq // d, HEAD_DIM), lambda b, h: (b, 0, 0, h))
        in_specs += [spec] * 3
    in_specs.append(pl.BlockSpec((N_GROUPS, 1, 2, N_STEPS, 2 * N_STEPS), lambda b, h: (0, h, 0, 0, 0)))
    return pl.pallas_call(
        functools.partial(_attn_kernel, seq=seq, n_cast=len(casts)),
        grid=(batch, HEADS),
        in_specs=in_specs + cast_in,
        out_specs=[pl.BlockSpec((seq, HEAD_DIM), lambda b, h: (b, h))] + cast_out,
        out_shape=[jax.ShapeDtypeStruct((batch * seq, GROUP_COLS), BF16)] + cast_shape,
        scratch_shapes=[pltpu.VMEM((N_GROUPS - 1, seq, LANES), F32)] * 3
                       + [pltpu.VMEM((3, seq, LANES), F32)],
        compiler_params=_params(2),
        name="attn_prompt",
    )(*qkv, bias, *[w for w, _ in casts])


def _attn_sample_kernel(q0, q1, q2, n0, n1, n2, c0, c1, c2, slope_ref, o_ref, *, bb):
    q_refs, n_refs, c_refs = (q0, q1, q2), (n0, n1, n2), (c0, c1, c2)
    half = N_STEPS // 2
    tile = (2 * HEADS, HEAD_DIM)
    top = jax.lax.broadcasted_iota(jnp.int32, tile, 0) < HEADS
    steps_a = (N_STEPS - jax.lax.broadcasted_iota(jnp.int32, (half, 1, 1), 0)).astype(F32)
    back = jnp.where(top[:, 0:1], steps_a, steps_a - float(half))

    def both_halves(t):
        return jnp.where(top, t, pltpu.roll(t, HEADS, axis=0))

    for b in range(bb):
        outs, lses = [], []
        for g in range(N_GROUPS):
            q8, kv8 = q_refs[g][b], n_refs[g][b]
            xa, xb = c_refs[g][b, 0:half], c_refs[g][b, half:N_STEPS]
            slope = both_halves(slope_ref[g])[:, 0:1]
            bias = (back * (-float(DILATIONS[g]))) * slope
            prod = jnp.where(top, xa * q8, pltpu.roll(xb * q8, HEADS, axis=1))
            s = jnp.sum(prod, axis=-1, keepdims=True) + bias
            s_self = both_halves(jnp.broadcast_to(jnp.sum(kv8 * q8, axis=-1, keepdims=True), tile))
            m = jnp.broadcast_to(jnp.max(s, axis=0), tile)
            m = jnp.maximum(jnp.maximum(m, pltpu.roll(m, HEADS, axis=0)), s_self)
            p = jnp.exp2(s - m[:, 0:1])
            p_self = jnp.exp2(s_self - m)
            l = jnp.broadcast_to(jnp.sum(p, axis=0), tile)
            l = l + pltpu.roll(l, HEADS, axis=0) + p_self
            p_b = jnp.broadcast_to(p, xb.shape)
            acc = jnp.sum(pltpu.roll(p_b, HEADS, axis=1) * xa + p_b * xb, axis=0) + p_self * kv8
            outs.append(acc / l)
            lses.append(m + jnp.log(l) * LOG2E)
        mx = jnp.maximum(jnp.maximum(lses[0], lses[1]), lses[2])
        ws = [jnp.exp2(v - mx) for v in lses]
        num = ws[0] * outs[0] + ws[1] * outs[1] + ws[2] * outs[2]
        o8 = num / (ws[0] + ws[1] + ws[2])
        for hh in range(HEADS):
            o_ref[b:b + 1, hh * HEAD_DIM:(hh + 1) * HEAD_DIM] = o8[HEADS + hh:HEADS + hh + 1, :]


def _attn_sample_call(q_tiles, new_tiles, caches, layer, slopes8):
    bd = q_tiles[0].shape[0]
    bb = SAMPLE_ATTN_SEQS
    nb = bd // bb
    tile = (2 * HEADS, HEAD_DIM)
    in_specs = [pl.BlockSpec((bb,) + tile, lambda i: (i, 0, 0))] * 3
    in_specs += [pl.BlockSpec((bb,) + tile, lambda i: (layer * nb + i, 0, 0))] * 3
    in_specs += [pl.BlockSpec((None, bb, N_STEPS) + tile, lambda i: (layer, i, 0, 0, 0))] * 3
    in_specs.append(pl.BlockSpec((N_GROUPS,) + tile, lambda i: (0, 0, 0)))
    return pl.pallas_call(
        functools.partial(_attn_sample_kernel, bb=bb),
        grid=(nb,),
        in_specs=in_specs,
        out_specs=pl.BlockSpec((bb, GROUP_COLS), lambda i: (i, 0)),
        out_shape=jax.ShapeDtypeStruct((bd, GROUP_COLS), F32),
        compiler_params=_params(1),
        name="attn_sample",
    )(*q_tiles, *new_tiles, *caches, slopes8)


def _post_math(x, mod, ya, w, ws_ref, bs_ref, mix_scr, *, chunk_rows, d_model, d_ff):
    g1 = mod[:, 2 * d_model:3 * d_model]
    g2 = mod[:, 5 * d_model:6 * d_model]

    h = _modulated_norm(x, w["n1w"][...], mod, 0, d_model).astype(BF16)
    sw = SGU_WIDTH
    wg1, wg2 = w["wg1"], w["wg2"]
    u = _gelu(_dot(h, wg1[:, 0:sw]))
    vs = _gelu(_dot(h, wg1[:, sw:2 * sw]))
    mu = jnp.mean(vs, axis=-1, keepdims=True)
    vc = vs - mu
    var = jnp.mean(vc * vc, axis=-1, keepdims=True)
    vs = vc * jax.lax.rsqrt(var + EPS) * w["lnw"][...] + w["lnb"][...]
    half = d_model // 2
    ga = jnp.concatenate([_sigmoid(_dot(h, wg1[:, 2 * sw:2 * sw + half])),
                          _sigmoid(_dot(h, wg2[:, 0:half]))], axis=-1)
    gb = _sigmoid(_dot(h, wg2[:, half:half + d_model]))

    if chunk_rows is not None:
        row = jax.lax.broadcasted_iota(jnp.int32, (CHUNK, CHUNK), 0)
        col = jax.lax.broadcasted_iota(jnp.int32, (CHUNK, CHUNK), 1)
        tril = col <= row
        lane = jax.lax.broadcasted_iota(jnp.int32, (CHUNK, LANES), 1)
        low = lane < SGU_GROUP_DIM
        wms = [jnp.where(tril, ws_ref[gi], 0.0).astype(BF16) for gi in range(SGU_GROUPS)]
        vs_b = vs.astype(BF16)
        for c in range(chunk_rows // CHUNK):
            rows = slice(c * CHUNK, (c + 1) * CHUNK)
            for j in range(SGU_WIDTH // LANES):
                cols = slice(j * LANES, (j + 1) * LANES)
                v = vs_b[rows, cols]
                mix_scr[rows, cols] = jnp.where(low, _dot(wms[2 * j], v), _dot(wms[2 * j + 1], v))
        mixed = mix_scr[...] + jnp.concatenate([bs_ref[...]] * (chunk_rows // CHUNK), axis=0)
    else:
        mixed = vs * ws_ref[...] + bs_ref[...]

    ys = (u * mixed).astype(BF16)
    xa = _dot(ya.astype(BF16), w["wpa"][...])
    xs = _dot(ys, w["wps"][...])
    merged = (ga * xa + gb * xs).astype(BF16)
    x1 = x + g1 * _dot(merged, w["wo"][...])
    h2 = _modulated_norm(x1, w["n2w"][...], mod, 1, d_model).astype(BF16)
    a = _dot(h2, w["wfi"][:, 0:d_ff])
    b = _dot(h2, w["wfi"][:, d_ff:2 * d_ff])
    act = (a * _sigmoid(a) * b).astype(BF16)
    return x1 + g2 * _dot(act, w["wfo"][...]), vs


_POST_WEIGHTS = ("n1w", "wg1", "wg2", "lnw", "lnb", "wpa", "wps", "wo", "n2w", "wfi", "wfo")


def _post_kernel(x_ref, mod_ref, ya_ref, ws_ref, bs_ref, xs_ref, mods_ref, yas_ref, wss_ref, bss_ref,
                 *refs, tm, d_model, d_ff):
    n_w = len(_POST_WEIGHTS)
    w = dict(zip(_POST_WEIGHTS, refs[:n_w]))
    o_ref, os_ref, vs_ref, mix_scr = refs[n_w:]
    o_ref[...], _ = _post_math(x_ref[...], mod_ref[0], ya_ref[...], w, ws_ref, bs_ref, mix_scr,
                               chunk_rows=tm, d_model=d_model, d_ff=d_ff)

    @pl.when(pl.program_id(0) == pl.num_programs(0) - 1)
    def _():
        os_ref[...], vs_ref[...] = _post_math(xs_ref[...], mods_ref[0], yas_ref[...], w, wss_ref,
                                              bss_ref, None, chunk_rows=None,
                                              d_model=d_model, d_ff=d_ff)


def _post_call(x, mod, ya, ws, bs, xs, mod_s, ya_s, ws_s, bs_s,
               n1w, w_in, lnw, lnb, wpa, wps, wo, n2w, wfi, wfo, *, seq, tm, layer):
    t, d_model = x.shape
    ts = xs.shape[0]
    d_ff = wfo.shape[0]
    tiles_per_seq = seq // tm
    gblk = QKV_COLS
    assert 2 * SGU_WIDTH + d_model // 2 == gblk and w_in.shape[1] == 5 * gblk

    def w_gate(j):
        return pl.BlockSpec((d_model, gblk), lambda i: (0, 3 + j), pipeline_mode=pl.Buffered(1))

    def tok(cols):
        return pl.BlockSpec((tm, cols), lambda i: (i, 0))

    weight_specs = [_resident_layer(n1w.shape, layer), w_gate(0), w_gate(1),
                    _resident_layer(lnw.shape, layer), _resident_layer(lnb.shape, layer),
                    _resident_layer(wpa.shape, layer), _resident_layer(wps.shape, layer),
                    _resident_layer(wo.shape, layer), _resident_layer(n2w.shape, layer),
                    _resident(wfi.shape), _resident(wfo.shape)]
    assert len(weight_specs) == len(_POST_WEIGHTS)
    kern = functools.partial(_post_kernel, tm=tm, d_model=d_model, d_ff=d_ff)
    return pl.pallas_call(
        kern,
        grid=(t // tm,),
        in_specs=[tok(d_model),
                  pl.BlockSpec((1,) + mod.shape[1:], lambda i: (i // tiles_per_seq, 0, 0)),
                  tok(GROUP_COLS), _resident(ws.shape), _resident(bs.shape),
                  _resident(xs.shape), _resident(mod_s.shape), _resident(ya_s.shape),
                  _resident(ws_s.shape), _resident(bs_s.shape)] + weight_specs,
        out_specs=[tok(d_model),
                   pl.BlockSpec((ts, d_model), lambda i: (0, 0)),
                   pl.BlockSpec((ts, SGU_WIDTH), lambda i: (0, 0))],
        out_shape=[jax.ShapeDtypeStruct((t, d_model), F32),
                   jax.ShapeDtypeStruct((ts, d_model), F32),
                   jax.ShapeDtypeStruct((ts, SGU_WIDTH), F32)],
        scratch_shapes=[pltpu.VMEM((tm, SGU_WIDTH), F32)],
        compiler_params=_params(1),
        name="post",
    )(x, mod, ya, ws, bs, xs, mod_s, ya_s, ws_s, bs_s,
      n1w, w_in, w_in, lnw, lnb, wpa, wps, wo, n2w, wfi, wfo)


def kernel(x_prompt, x_sample, cache_kv_w128, cache_kv_w512, cache_kv_w2048, c_prompt, c_sample,
           w_ada, b_ada, norm1_w, w_in, q_norm_w, k_norm_w, sgu_ln_w, sgu_ln_b, w_spatial, b_spatial,
           w_proj_att, w_proj_sgu, w_out, norm2_w, w_ffn_in, w_ffn_out):
    batch, seq, d_model = x_prompt.shape
    bd, dec_seq, _ = x_sample.shape
    depth = w_in.shape[0]
    assert dec_seq == 1 and seq % (N_STEPS * DILATIONS[-1]) == 0
    caches_in = (cache_kv_w128, cache_kv_w512, cache_kv_w2048)
    for g in range(N_GROUPS):
        assert caches_in[g].shape[2] == WINDOWS[g]

    n_c = batch + bd
    c_all = jnp.concatenate([c_prompt, c_sample], axis=0)
    c_all = jnp.pad(c_all, ((0, -n_c % 16), (0, 0)))
    mod, w_in_l = _ada_call(c_all, w_ada, b_ada, [(w_in, 0)])

    band_bias = jnp.asarray(_band_bias())
    tile = (2 * HEADS, HEAD_DIM)
    slopes8 = np.zeros((N_GROUPS,) + tile, np.float32)
    slopes8[:, :HEADS, :] = _alibi_slopes()[:, :, None]
    slopes8 = jnp.asarray(slopes8)
    caches = [caches_in[g].reshape((depth, bd, N_STEPS, DILATIONS[g] * tile[0], tile[1]))
              for g in range(N_GROUPS)]

    xp = x_prompt.reshape(batch * seq, d_model)
    xs = x_sample.reshape(bd, d_model)
    p_states = s_states = None
    s_v = []
    wpa, wps, wo = w_proj_att.astype(BF16), w_proj_sgu.astype(BF16), w_out.astype(BF16)
    qw = jnp.tile(q_norm_w[:, :, None, :], (1, 1, HEADS, 1)).reshape(depth, 1, QKV_COLS)
    kw = jnp.tile(k_norm_w[:, :, None, :], (1, 1, HEADS, 1)).reshape(depth, 1, QKV_COLS)
    lnw, lnb = sgu_ln_w.reshape(depth, 1, -1), sgu_ln_b.reshape(depth, 1, -1)
    n1w, n2w = norm1_w.reshape(depth, 1, -1), norm2_w.reshape(depth, 1, -1)
    for l in range(depth):
        mod_p = mod[l, :batch].reshape(batch, 1, -1)
        mod_s = mod[l, batch:n_c].reshape(1, bd, -1)
        bs_p = jnp.repeat(b_spatial[l].T, SGU_GROUP_DIM, axis=1)
        ws_s = jnp.repeat(w_spatial[l][:, 0, 0], SGU_GROUP_DIM).reshape(1, SGU_WIDTH)
        bs_s = jnp.repeat(b_spatial[l][:, 0], SGU_GROUP_DIM).reshape(1, SGU_WIDTH)

        outs = _inproj_call(xp, mod_p, n1w, w_in_l, qw, kw,
                            seq=seq, tm=INPROJ_TILE_ROWS, deinterleave=True, act_dtype=BF16,
                            layer=l, depth=depth, prev_states=p_states)
        qkv, p_states = outs[:9], outs[9:]
        casts = [(w_ffn_in, l), (w_ffn_out, l)] + ([(w_in, l + 1)] if l + 1 < depth else [])
        ya, wfi, wfo, *w_in_next = _attn_call(qkv, band_bias, casts, batch=batch, seq=seq)

        outs = _inproj_call(xs, mod_s, n1w, w_in_l, qw, kw,
                            seq=bd, tm=bd, deinterleave=False, act_dtype=F32,
                            layer=l, depth=depth, prev_states=s_states)
        q_tiles, s_states = [o.reshape((bd,) + tile) for o in outs[:3]], outs[3:]
        new_tiles = [s.reshape((depth * bd,) + tile) for s in s_states]
        ya_s = _attn_sample_call(q_tiles, new_tiles, caches, l, slopes8)

        xp, xs, vs = _post_call(xp, mod_p, ya, w_spatial[l], bs_p, xs, mod_s, ya_s, ws_s, bs_s,
                                n1w, w_in_l, lnw, lnb, wpa, wps, wo, n2w, wfi, wfo,
                                seq=seq, tm=POST_TILE_ROWS, layer=l)
        s_v.append(vs.reshape(bd, 1, SGU_WIDTH))
        if w_in_next:
            (w_in_l,) = w_in_next

    p_states =[s.reshape(depth, batch, -1, 2, HEADS, HEAD_DIM) for s in p_states]
    s_states = [s.reshape(depth, bd, 1, 2, HEADS, HEAD_DIM) for s in s_states]
    return (xp.reshape(batch, seq, d_model), xs.reshape(bd, 1, d_model),
            p_states[0], p_states[1], p_states[2], s_states[0], s_states[1], s_states[2],
            jnp.stack(s_v))
```

```python
import functools
import math

import jax
import jax.numpy as jnp
import numpy as np
from jax.experimental import pallas as pl
from jax.experimental.pallas import tpu as pltpu

HEAD_DIM = 128
N_GROUPS = 3
HEADS = 4
GROUP_COLS = HEADS * HEAD_DIM
QKV_COLS = N_GROUPS * GROUP_COLS
WINDOWS = (128, 512, 2048)
DILATIONS = (1, 4, 16)
N_STEPS = 128
CHUNK = 128
SGU_WIDTH = 512
SGU_GROUPS = 8
SGU_GROUP_DIM = SGU_WIDTH // SGU_GROUPS
N_ADA = 6
EPS = 1e-6
NEG_INF = -1e30
LOG2E = math.log2(math.e)
QK_SCALE = HEAD_DIM ** -0.5 * LOG2E

LANES = 128
RELAYOUT_STRIDE = 4
VMEM_LIMIT_BYTES = 56 * 1024 * 1024

INPROJ_TILE_ROWS = 512
POST_TILE_ROWS = 512
ADALN_TILE_COLS = 1536
SAMPLE_ATTN_SEQS = 8
ATTN_UNROLL = 32

BF16 = jnp.bfloat16
F32 = jnp.float32


def _params(n_axes):
    return pltpu.CompilerParams(dimension_semantics=("arbitrary",) * n_axes,
                                vmem_limit_bytes=VMEM_LIMIT_BYTES)


def _resident(shape):
    nd = len(shape)
    return pl.BlockSpec(shape, lambda *_: (0,) * nd, pipeline_mode=pl.Buffered(1))


def _resident_layer(shape, layer):
    nd = len(shape)
    return pl.BlockSpec((None,) + tuple(shape[1:]), lambda *_: (layer,) + (0,) * (nd - 1),
                        pipeline_mode=pl.Buffered(1))


def _dot(a, b):
    return jnp.dot(a, b, preferred_element_type=F32)


def _dot_nt(a, b):
    return jax.lax.dot_general(a, b, (((1,), (1,)), ((), ())), preferred_element_type=F32)


def _rms(x):
    return x * jax.lax.rsqrt(jnp.mean(x * x, axis=-1, keepdims=True) + EPS)


def _gelu(x):
    return 0.5 * x * (1.0 + jax.lax.erf(x * (2.0 ** -0.5)))


def _sigmoid(x):
    return 1.0 / (1.0 + jnp.exp(-x))


def _cast_specs(casts, n_steps, step):
    ins, outs, shapes = [], [], []
    for w, layer in casts:
        _, k_dim, n_dim = w.shape
        band = k_dim // n_steps
        assert band * n_steps == k_dim and band % 16 == 0
        ins.append(pl.BlockSpec((None, band, n_dim), lambda *g, layer=layer: (layer, step(*g), 0)))
        outs.append(pl.BlockSpec((band, n_dim), lambda *g: (step(*g), 0)))
        shapes.append(jax.ShapeDtypeStruct((k_dim, n_dim), BF16))
    return ins, outs, shapes


def _ada_kernel(c_ref, w_ref, b_ref, *refs):
    n_cast = (len(refs) - 1) // 2
    o_ref = refs[n_cast]
    for src, dst in zip(refs[:n_cast], refs[n_cast + 1:]):
        dst[...] = src[...].astype(dst.dtype)
    c = c_ref[...]
    a = c * _sigmoid(c)
    w = w_ref[...]
    a_hi = a.astype(BF16)
    a_lo = (a - a_hi.astype(F32)).astype(BF16)
    w_hi = w.astype(BF16)
    w_lo = (w - w_hi.astype(F32)).astype(BF16)
    rows = a.shape[0]
    both = _dot(jnp.concatenate([a_hi, a_lo], axis=0), w_hi)
    o_ref[...] = both[:rows] + both[rows:] + _dot(a_hi, w_lo) + b_ref[...]


def _ada_call(c_all, w_ada, b_ada, casts):
    depth, d, n = w_ada.shape
    rows = c_all.shape[0]
    tn = ADALN_TILE_COLS
    per_layer = n // tn
    cast_in, cast_out, cast_shape = _cast_specs(casts, depth * per_layer, lambda l, j: l * per_layer + j)
    return pl.pallas_call(
        _ada_kernel,
        grid=(depth, per_layer),
        in_specs=[pl.BlockSpec((rows, d), lambda l, j: (0, 0)),
                  pl.BlockSpec((None, d, tn), lambda l, j: (l, 0, j)),
                  pl.BlockSpec((None, 1, tn), lambda l, j: (l, 0, j))] + cast_in,
        out_specs=[pl.BlockSpec((None, rows, tn), lambda l, j: (l, 0, j))] + cast_out,
        out_shape=[jax.ShapeDtypeStruct((depth, rows, n), F32)] + cast_shape,
        compiler_params=_params(2),
        name="adaln_mod",
    )(c_all, w_ada, b_ada.reshape(depth, 1, n), *[w for w, _ in casts])


def _modulated_norm(x, w, mod, which, d_model):
    shift = mod[:, 3 * which * d_model:(3 * which + 1) * d_model]
    scale = mod[:, (3 * which + 1) * d_model:(3 * which + 2) * d_model]
    return _rms(x) * w * (1.0 + scale) + shift


def _inproj_kernel(x_ref, mod_ref, n1w_ref, w_ref, qw_ref, kw_ref,
                   *refs, tm, d_model, dils, state_rows, tiles_per_seq, deinterleave, n_aliased):
    refs = refs[n_aliased:]
    if deinterleave:
        q_refs, k_refs, v_refs, st_refs = refs[0:9:3], refs[1:9:3], refs[2:9:3], refs[9:12]
        h_scr, slab, slab2, kv_stash = refs[12:]
    else:
        q_refs, st_refs = refs[0:3], refs[3:6]
        h_scr, slab, slab2, kv_stash = refs[6:]
    ti = pl.program_id(0) % tiles_per_seq
    tile_rows = 2 * HEADS

    h = _modulated_norm(x_ref[...], n1w_ref[...], mod_ref[0], 0, d_model)
    h_scr[...] = h.astype(BF16)

    def proj(c0, width):
        return _dot(h_scr[...], w_ref[:, c0:c0 + width])

    def put_tiles(ref, first_row, val):
        for hh in range(HEADS):
            ref[pl.ds(first_row + hh, tm, stride=tile_rows), :] = val[:, hh * HEAD_DIM:(hh + 1) * HEAD_DIM]

    def put(ref, g, val, slot, slabs=None):
        d = dils[g]
        if d == 1:
            ref[...] = val.astype(ref.dtype)
            return
        if slabs is None:
            slabs = [slab.at[g - 1, hh] for hh in range(HEADS)]
            for hh in range(HEADS):
                slabs[hh][...] = val[:, hh * HEAD_DIM:(hh + 1) * HEAD_DIM]
        if d <= RELAYOUT_STRIDE:
            for r in range(d):
                for hh in range(HEADS):
                    ref[0, r, :, hh * HEAD_DIM:(hh + 1) * HEAD_DIM] = (
                        slabs[hh][pl.ds(r, tm // d, stride=d), :].astype(ref.dtype))
            return
        s1, s2 = RELAYOUT_STRIDE, d // RELAYOUT_STRIDE
        part = tm // s1
        for hh in range(HEADS):
            for r1 in range(s1):
                slab2[slot, hh,r1 * part:(r1 + 1) * part, :] = slabs[hh][pl.ds(r1, part, stride=s1), :]
            for r1 in range(s1):
                for r2 in range(s2):
                    ref[0, r1 + s1 * r2, :, hh * HEAD_DIM:(hh + 1) * HEAD_DIM] = (
                        slab2[slot, hh,pl.ds(r1 * part + r2, tm // d, stride=s2), :].astype(ref.dtype))

    def head_rms(z, w):
        parts = [_rms(z[:, hh * HEAD_DIM:(hh + 1) * HEAD_DIM]) for hh in range(HEADS)]
        return jnp.concatenate(parts, axis=-1) * w

    def put_state(g, kn, v):
        if state_rows[g] is None:
            put_tiles(st_refs[g], 0, kn)
            put_tiles(st_refs[g], HEADS, v)
        else:
            for hh in range(HEADS):
                cols = slice(hh * HEAD_DIM, (hh + 1) * HEAD_DIM)
                kv_stash[g, hh] = kn[:, cols]
                kv_stash[g, HEADS + hh] = v[:, cols]

    def flush_state(g):
        st, rows = st_refs[g], min(state_rows[g], tm)
        first = tiles_per_seq - max(state_rows[g] // tm, 1)

        @pl.when(ti >= first)
        def _():
            for j in range(2 * HEADS):
                st[pl.ds(j, rows, stride=2 * HEADS), :] = kv_stash[g, j, tm - rows:, :]

    for g in reversed(range(N_GROUPS)):
        c0 = g * GROUP_COLS
        q = head_rms(proj(c0, GROUP_COLS), qw_ref[:, c0:c0 + GROUP_COLS] * QK_SCALE)
        kn = head_rms(proj(QKV_COLS + c0, GROUP_COLS), kw_ref[:, c0:c0 + GROUP_COLS])
        v = proj(2 * QKV_COLS + c0, GROUP_COLS)
        put_state(g, kn, v)
        if not deinterleave:
            put_tiles(q_refs[g], 0, q)
            put_tiles(q_refs[g], HEADS, jnp.zeros_like(q))
            continue
        put(q_refs[g], g, q, 0)
        put(k_refs[g], g, kn, 1, [kv_stash.at[g, hh] for hh in range(HEADS)])
        put(v_refs[g], g, v, 2, [kv_stash.at[g, HEADS + hh] for hh in range(HEADS)])

    for g in range(N_GROUPS):
        if state_rows[g] is not None:
            flush_state(g)


def _inproj_call(x, mod, n1w, w_in, qw, kw, *, seq, tm, deinterleave, act_dtype,
                 layer=0, depth=1, prev_states=None):
    t, d_model = x.shape
    batch = t // seq
    tiles_per_seq = seq // tm

    def tok(cols, dtype):
        return (jax.ShapeDtypeStruct((t, cols), dtype), pl.BlockSpec((tm, cols), lambda i: (i, 0)))

    def grp(g, dtype):
        d = DILATIONS[g]
        if d == 1:
            return tok(GROUP_COLS, dtype)
        assert tm % d == 0 and (tm // d) % 16 == 0
        return (jax.ShapeDtypeStruct((batch, d, seq // d, GROUP_COLS), dtype),
                pl.BlockSpec((1, d, tm // d, GROUP_COLS),
                             lambda i: (i // tiles_per_seq, 0, i % tiles_per_seq, 0)))

    tile_rows = 2 * HEADS
    outs = []
    for g in range(N_GROUPS):
        if deinterleave:
            outs += [grp(g, act_dtype)] * 3
        else:
            assert t == tm
            outs.append((jax.ShapeDtypeStruct((t * tile_rows, LANES), F32),
                         pl.BlockSpec((t * tile_rows, LANES), lambda i: (0, 0))))

    state_rows = []
    for g in range(N_GROUPS):
        if not deinterleave:
            state_rows.append(None)
            outs.append((jax.ShapeDtypeStruct((depth * t * tile_rows, LANES), F32),
                         pl.BlockSpec((t * tile_rows, LANES), lambda i: (layer, 0))))
            continue
        rows = min(WINDOWS[g], seq)
        state_rows.append(rows)
        blk_rows = min(rows, tm)
        assert rows % blk_rows == 0
        nblk = rows // blk_rows
        shape = jax.ShapeDtypeStruct((depth * batch * rows * 2 * HEADS, LANES), F32)
        spec = pl.BlockSpec(
            (blk_rows * 2 * HEADS, LANES),
            lambda i, nblk=nblk: ((layer * batch + i // tiles_per_seq) * nblk
                                  + jnp.maximum(i % tiles_per_seq - (tiles_per_seq - nblk), 0), 0))
        outs.append((shape, spec))

    aliased = list(prev_states) if prev_states is not None else []
    n_in = 6
    w_qkv = pl.BlockSpec((d_model, 3 * QKV_COLS), lambda i: (0, 0),
                         pipeline_mode=pl.Buffered(1))
    kern = functools.partial(_inproj_kernel, tm=tm, d_model=d_model, dils=DILATIONS,
                             state_rows=tuple(state_rows), tiles_per_seq=tiles_per_seq,
                             deinterleave=deinterleave, n_aliased=len(aliased))
    return pl.pallas_call(
        kern,
        grid=(t // tm,),
        in_specs=[pl.BlockSpec((tm, d_model), lambda i: (i, 0)),
                  pl.BlockSpec((1,) + mod.shape[1:], lambda i: (i // tiles_per_seq, 0, 0)),
                  _resident_layer(n1w.shape, layer),
                  w_qkv,
                  _resident_layer(qw.shape, layer), _resident_layer(kw.shape, layer)]
                 + [pl.BlockSpec(memory_space=pl.ANY)] * len(aliased),
        out_specs=[o[1] for o in outs],
        out_shape=[o[0] for o in outs],
        input_output_aliases={n_in + g: len(outs) - N_GROUPS + g for g in range(len(aliased))},
        scratch_shapes=[pltpu.VMEM((tm, d_model), BF16),
                        pltpu.VMEM((N_GROUPS - 1, HEADS, tm, LANES), F32),
                        pltpu.VMEM((3, HEADS, tm, LANES), F32),
                        pltpu.VMEM((N_GROUPS, 2 * HEADS, tm, LANES), F32)],
        compiler_params=_params(1),
        name="inproj_deint" if deinterleave else "inproj_tok",
    )(x, mod, n1w, w_in, qw, kw, *aliased)


def _alibi_slopes():
    h = np.arange(1, N_GROUPS * HEADS + 1, dtype=np.float32)
    s = np.power(np.float32(2.0), -8.0 * h / (N_GROUPS * HEADS)).astype(np.float32)
    return (s.astype(np.float64) * LOG2E).astype(np.float32).reshape(N_GROUPS, HEADS)


def _band_bias():
    qi = np.arange(N_STEPS)[:, None]
    ki = np.arange(2 * N_STEPS)[None, :]
    dist = N_STEPS + qi - ki
    valid = (dist >= 0) & (dist <= N_STEPS)
    slopes = _alibi_slopes()
    out = np.empty((N_GROUPS, HEADS, 2, N_STEPS, 2 * N_STEPS), np.float32)
    for g in range(N_GROUPS):
        for hh in range(HEADS):
            b = np.where(valid, -slopes[g, hh] * (dist * DILATIONS[g]).astype(np.float32),
                         np.float32(NEG_INF)).astype(np.float32)
            out[g, hh, 1] = b
            first = b.copy()
            first[:, :N_STEPS] = NEG_INF
            out[g, hh, 0] = first
    return out


def _attn_block(q, k2, v2, bias):
    s = _dot_nt(q, k2) + bias
    m = jnp.max(s, axis=-1, keepdims=True)
    p = jnp.exp2(s - m)
    l = jnp.sum(p, axis=-1, keepdims=True)
    return _dot(p.astype(v2.dtype), v2), m, l


def _attn_kernel(q0, k0, v0, q1, k1, v1, q2, k2, v2, bias_ref, *refs, seq, n_cast):
    cast_src, refs = refs[:n_cast], refs[n_cast:]
    o_ref, cast_dst = refs[0], refs[1:1 + n_cast]
    o_scr, m_scr, l_scr, stage_scr = refs[1 + n_cast:]
    blk = N_STEPS
    s1 = RELAYOUT_STRIDE

    for src, dst in zip(cast_src, cast_dst):
        dst[...] = src[...].astype(dst.dtype)

    def run_group(g, qr, kr, vr):
        d = DILATIONS[g]
        nblk = seq // d // blk

        def one_block(r, i):
            own = pl.multiple_of(i * blk, blk)
            prev = pl.multiple_of(jnp.maximum(i - 1, 0) * blk, blk)
            if d == 1:
                ld = lambda ref, start: ref[pl.ds(start, blk), :]
            else:
                ld = lambda ref, start: ref[0, r, pl.ds(start, blk), :]
            k2_ = jnp.concatenate([ld(kr, prev), ld(kr, own)], axis=0)
            v2_ = jnp.concatenate([ld(vr, prev), ld(vr, own)], axis=0)
            o, m, l = _attn_block(ld(qr, own), k2_, v2_, bias_ref[g, 0, jnp.minimum(i, 1)])
            m = jnp.broadcast_to(m, (blk, LANES))
            l = jnp.broadcast_to(l, (blk, LANES))
            if 1 < d <= s1:
                rows = pl.ds(i * (blk * d) + r, blk, stride=d)
                o_scr[g - 1, rows, :] = o
                m_scr[g - 1, rows, :] = m
                l_scr[g - 1, rows, :] = l
                return
            if d > s1:
                s2 = d // s1
                rows = pl.ds((r % s1) * (seq // s1) + i * (blk * s2) + r // s1, blk, stride=s2)
                stage_scr[0, rows, :] = o
                stage_scr[1, rows, :] = m
                stage_scr[2, rows, :] = l
                return
            rows = pl.ds(own, blk)
            ms = [m, m_scr[0, rows, :], m_scr[1, rows, :]]
            mx = jnp.maximum(jnp.maximum(ms[0], ms[1]), ms[2])
            ws = [jnp.exp2(x - mx) for x in ms]
            num = ws[0] * o + ws[1] * o_scr[0, rows, :] + ws[2] * o_scr[1, rows, :]
            den = ws[0] * l + ws[1] * l_scr[0, rows, :] + ws[2] * l_scr[1, rows, :]
            o_ref[rows, :] = (num / den).astype(o_ref.dtype)

        ur = min(ATTN_UNROLL, d)
        ui = ATTN_UNROLL // ur
        assert d % ur == 0 and nblk % ui == 0

        def body(it, carry):
            i0 = (it // (d // ur)) * ui
            r0 = (it % (d // ur)) * ur
            for a in range(ui):
                for b in range(ur):
                    one_block(r0 + b, i0 + a)
            return carry

        jax.lax.fori_loop(0, (d // ur) * (nblk // ui), body, 0)

    run_group(2, q2, k2, v2)
    run_group(1, q1, k1, v1)
    g = N_GROUPS - 1
    assert DILATIONS[g] > s1 and all(d <= s1 for d in DILATIONS[:g])
    per_stream = seq // s1
    for a, dst in enumerate((o_scr, m_scr, l_scr)):
        for r1 in range(s1):
            for c in range(per_stream // blk):
                dst[g - 1, pl.ds(s1 * c * blk + r1, blk, stride=s1), :] = (
                    stage_scr[a, r1 * per_stream + c * blk:r1 * per_stream + (c + 1) * blk, :])
    run_group(0, q0, k0, v0)


def _attn_call(qkv, bias, casts, *, batch, seq):
    cast_in, cast_out, cast_shape = _cast_specs(casts, batch * HEADS, lambda b, h: b * HEADS + h)
    in_specs = []
    for g in range(N_GROUPS):
        d = DILATIONS[g]
        if d == 1:
            spec = pl.BlockSpec((seq, HEAD_DIM), lambda b, h: (b, h))
        else:
            spec = pl.BlockSpec((1, d, seq // d, HEAD_DIM), lambda b, h: (b, 0, 0, h))
        in_specs += [spec] * 3
    in_specs.append(pl.BlockSpec((N_GROUPS, 1, 2, N_STEPS, 2 * N_STEPS), lambda b, h: (0, h, 0, 0, 0)))
    return pl.pallas_call(
        functools.partial(_attn_kernel, seq=seq, n_cast=len(casts)),
        grid=(batch, HEADS),
        in_specs=in_specs + cast_in,
        out_specs=[pl.BlockSpec((seq, HEAD_DIM), lambda b, h: (b, h))] + cast_out,
        out_shape=[jax.ShapeDtypeStruct((batch * seq, GROUP_COLS), BF16)] + cast_shape,
        scratch_shapes=[pltpu.VMEM((N_GROUPS - 1, seq, LANES), F32)] * 3
                       + [pltpu.VMEM((3, seq, LANES), F32)],
        compiler_params=_params(2),
        name="attn_prompt",
    )(*qkv, bias, *[w for w, _ in casts])


def _attn_sample_kernel(q0, q1, q2, n0, n1, n2, c0, c1, c2, slope_ref, o_ref, *, bb):
    q_refs, n_refs, c_refs = (q0, q1, q2), (n0, n1, n2), (c0, c1, c2)
    half = N_STEPS // 2
    tile = (2 * HEADS, HEAD_DIM)
    top = jax.lax.broadcasted_iota(jnp.int32, tile, 0) < HEADS
    steps_a = (N_STEPS - jax.lax.broadcasted_iota(jnp.int32, (half, 1, 1), 0)).astype(F32)
    back = jnp.where(top[:, 0:1], steps_a, steps_a - float(half))

    def both_halves(t):
        return jnp.where(top, t, pltpu.roll(t, HEADS, axis=0))

    for b in range(bb):
        outs, lses = [], []
        for g in range(N_GROUPS):
            q8, kv8 = q_refs[g][b], n_refs[g][b]
            xa, xb = c_refs[g][b, 0:half], c_refs[g][b, half:N_STEPS]
            slope = both_halves(slope_ref[g])[:, 0:1]
            bias = (back * (-float(DILATIONS[g]))) * slope
            prod = jnp.where(top, xa * q8, pltpu.roll(xb * q8, HEADS, axis=1))
            s = jnp.sum(prod, axis=-1, keepdims=True) + bias
            s_self = both_halves(jnp.broadcast_to(jnp.sum(kv8 * q8, axis=-1, keepdims=True), tile))
            m = jnp.broadcast_to(jnp.max(s, axis=0), tile)
            m = jnp.maximum(jnp.maximum(m, pltpu.roll(m, HEADS, axis=0)), s_self)
            p = jnp.exp2(s - m[:, 0:1])
            p_self = jnp.exp2(s_self - m)
            l = jnp.broadcast_to(jnp.sum(p, axis=0), tile)
            l = l + pltpu.roll(l, HEADS, axis=0) + p_self
            p_b = jnp.broadcast_to(p, xb.shape)
            acc = jnp.sum(pltpu.roll(p_b, HEADS, axis=1) * xa + p_b * xb, axis=0) + p_self * kv8
            outs.append(acc / l)
            lses.append(m + jnp.log(l) * LOG2E)
        mx = jnp.maximum(jnp.maximum(lses[0], lses[1]), lses[2])
        ws = [jnp.exp2(v - mx) for v in lses]
        num = ws[0] * outs[0] + ws[1] * outs[1] + ws[2] * outs[2]
        o8 = num / (ws[0] + ws[1] + ws[2])
        for hh in range(HEADS):
            o_ref[b:b + 1, hh * HEAD_DIM:(hh + 1) * HEAD_DIM] = o8[HEADS + hh:HEADS + hh + 1, :]


def _attn_sample_call(q_tiles, new_tiles, caches, layer, slopes8):
    bd = q_tiles[0].shape[0]
    bb = SAMPLE_ATTN_SEQS
    nb = bd // bb
    tile = (2 * HEADS, HEAD_DIM)
    in_specs = [pl.BlockSpec((bb,) + tile, lambda i: (i, 0, 0))] * 3
    in_specs += [pl.BlockSpec((bb,) + tile, lambda i: (layer * nb + i, 0, 0))] * 3
    in_specs += [pl.BlockSpec((None, bb, N_STEPS) + tile, lambda i: (layer, i, 0, 0, 0))] * 3
    in_specs.append(pl.BlockSpec((N_GROUPS,) + tile, lambda i: (0, 0, 0)))
    return pl.pallas_call(
        functools.partial(_attn_sample_kernel, bb=bb),
        grid=(nb,),
        in_specs=in_specs,
        out_specs=pl.BlockSpec((bb, GROUP_COLS), lambda i: (i, 0)),
        out_shape=jax.ShapeDtypeStruct((bd, GROUP_COLS), F32),
        compiler_params=_params(1),
        name="attn_sample",
    )(*q_tiles, *new_tiles, *caches, slopes8)


def _post_math(x, mod, ya, w, ws_ref, bs_ref, mix_scr, *, chunk_rows, d_model, d_ff):
    g1 = mod[:, 2 * d_model:3 * d_model]
    g2 = mod[:, 5 * d_model:6 * d_model]
    xa = _dot(ya.astype(BF16), w["wpa"][...])

    h = _modulated_norm(x, w["n1w"][...], mod, 0, d_model).astype(BF16)
    sw = SGU_WIDTH
    wg1, wg2 = w["wg1"], w["wg2"]
    u = _gelu(_dot(h, wg1[:, 0:sw]))
    vs = _gelu(_dot(h, wg1[:, sw:2 * sw]))
    mu = jnp.mean(vs, axis=-1, keepdims=True)
    vc = vs - mu
    var = jnp.mean(vc * vc, axis=-1, keepdims=True)
    vs = vc * jax.lax.rsqrt(var + EPS) * w["lnw"][...] + w["lnb"][...]
    half = d_model // 2
    ga = jnp.concatenate([_sigmoid(_dot(h, wg1[:, 2 * sw:2 * sw + half])),
                          _sigmoid(_dot(h, wg2[:, 0:half]))], axis=-1)
    gb = _sigmoid(_dot(h, wg2[:, half:half + d_model]))

    if chunk_rows is not None:
        row = jax.lax.broadcasted_iota(jnp.int32, (CHUNK, CHUNK), 0)
        col = jax.lax.broadcasted_iota(jnp.int32, (CHUNK, CHUNK), 1)
        tril = col <= row
        lane = jax.lax.broadcasted_iota(jnp.int32, (CHUNK, LANES), 1)
        low = lane < SGU_GROUP_DIM
        wms = [jnp.where(tril, ws_ref[gi], 0.0).astype(BF16) for gi in range(SGU_GROUPS)]
        vs_b = vs.astype(BF16)
        for c in range(chunk_rows // CHUNK):
            rows = slice(c * CHUNK, (c + 1) * CHUNK)
            for j in range(SGU_WIDTH // LANES):
                cols = slice(j * LANES, (j + 1) * LANES)
                v = vs_b[rows, cols]
                mix_scr[rows, cols] = jnp.where(low, _dot(wms[2 * j], v), _dot(wms[2 * j + 1], v))
        mixed = mix_scr[...] + jnp.concatenate([bs_ref[...]] * (chunk_rows // CHUNK), axis=0)
    else:
        mixed = vs * ws_ref[...] + bs_ref[...]

    ys = (u * mixed).astype(BF16)
    xs = _dot(ys, w["wps"][...])
    merged = (ga * xa + gb * xs).astype(BF16)
    x1 = x + g1 * _dot(merged, w["wo"][...])
    h2 = _modulated_norm(x1, w["n2w"][...], mod, 1, d_model).astype(BF16)
    a = _dot(h2, w["wfi"][:, 0:d_ff])
    b = _dot(h2, w["wfi"][:, d_ff:2 * d_ff])
    act = (a * _sigmoid(a) * b).astype(BF16)
    return x1 + g2 * _dot(act, w["wfo"][...]), vs


_POST_WEIGHTS = ("n1w", "wg1", "wg2", "lnw", "lnb", "wpa", "wps", "wo", "n2w", "wfi", "wfo")


def _post_kernel(x_ref, mod_ref, ya_ref, ws_ref, bs_ref, xs_ref, mods_ref, yas_ref, wss_ref, bss_ref,
                 *refs, tm, d_model, d_ff):
    n_w = len(_POST_WEIGHTS)
    w = dict(zip(_POST_WEIGHTS, refs[:n_w]))
    o_ref, os_ref, vs_ref, mix_scr = refs[n_w:]
    o_ref[...], _ = _post_math(x_ref[...], mod_ref[0], ya_ref[...], w, ws_ref, bs_ref, mix_scr,
                               chunk_rows=tm, d_model=d_model, d_ff=d_ff)

    @pl.when(pl.program_id(0) == pl.num_programs(0) - 1)
    def _():
        os_ref[...], vs_ref[...] = _post_math(xs_ref[...], mods_ref[0], yas_ref[...], w, wss_ref,
                                              bss_ref, None, chunk_rows=None,
                                              d_model=d_model, d_ff=d_ff)


def _post_call(x, mod, ya, ws, bs, xs, mod_s, ya_s, ws_s, bs_s,
               n1w, w_in, lnw, lnb, wpa, wps, wo, n2w, wfi, wfo, *, seq, tm, layer):
    t, d_model = x.shape
    ts = xs.shape[0]
    d_ff = wfo.shape[0]
    tiles_per_seq = seq // tm
    gblk = QKV_COLS
    assert 2 * SGU_WIDTH + d_model // 2 == gblk and w_in.shape[1] == 5 * gblk

    def w_gate(j):
        return pl.BlockSpec((d_model, gblk), lambda i: (0, 3 + j), pipeline_mode=pl.Buffered(1))

    def tok(cols):
        return pl.BlockSpec((tm, cols), lambda i: (i, 0))

    weight_specs = [_resident_layer(n1w.shape, layer), w_gate(0), w_gate(1),
                    _resident_layer(lnw.shape, layer), _resident_layer(lnb.shape, layer),
                    _resident_layer(wpa.shape, layer), _resident_layer(wps.shape, layer),
                    _resident_layer(wo.shape, layer), _resident_layer(n2w.shape, layer),
                    _resident(wfi.shape), _resident(wfo.shape)]
    assert len(weight_specs) == len(_POST_WEIGHTS)
    kern = functools.partial(_post_kernel, tm=tm, d_model=d_model, d_ff=d_ff)
    return pl.pallas_call(
        kern,
        grid=(t // tm,),
        in_specs=[tok(d_model),
                  pl.BlockSpec((1,) + mod.shape[1:], lambda i: (i // tiles_per_seq, 0, 0)),
                  tok(GROUP_COLS), _resident(ws.shape), _resident(bs.shape),
                  _resident(xs.shape), _resident(mod_s.shape), _resident(ya_s.shape),
                  _resident(ws_s.shape), _resident(bs_s.shape)] + weight_specs,
        out_specs=[tok(d_model),
                   pl.BlockSpec((ts, d_model), lambda i: (0, 0)),
                   pl.BlockSpec((ts, SGU_WIDTH), lambda i: (0, 0))],
        out_shape=[jax.ShapeDtypeStruct((t, d_model), F32),
                   jax.ShapeDtypeStruct((ts, d_model), F32),
                   jax.ShapeDtypeStruct((ts, SGU_WIDTH), F32)],
        scratch_shapes=[pltpu.VMEM((tm, SGU_WIDTH), F32)],
        compiler_params=_params(1),
        name="post",
    )(x, mod, ya, ws, bs, xs, mod_s, ya_s, ws_s, bs_s,
      n1w, w_in, w_in, lnw, lnb, wpa, wps, wo, n2w, wfi, wfo)


def kernel(x_prompt, x_sample, cache_kv_w128, cache_kv_w512, cache_kv_w2048, c_prompt, c_sample,
           w_ada, b_ada, norm1_w, w_in, q_norm_w, k_norm_w, sgu_ln_w, sgu_ln_b, w_spatial, b_spatial,
           w_proj_att, w_proj_sgu, w_out, norm2_w, w_ffn_in, w_ffn_out):
    batch, seq, d_model = x_prompt.shape
    bd, dec_seq, _ = x_sample.shape
    depth = w_in.shape[0]
    assert dec_seq == 1 and seq % (N_STEPS * DILATIONS[-1]) == 0
    caches_in = (cache_kv_w128, cache_kv_w512, cache_kv_w2048)
    for g in range(N_GROUPS):
        assert caches_in[g].shape[2] == WINDOWS[g]

    n_c = batch + bd
    c_all = jnp.concatenate([c_prompt, c_sample], axis=0)
    c_all = jnp.pad(c_all, ((0, -n_c % 16), (0, 0)))
    mod, w_in_l = _ada_call(c_all, w_ada, b_ada, [(w_in, 0)])

    band_bias = jnp.asarray(_band_bias())
    tile = (2 * HEADS, HEAD_DIM)
    slopes8 = np.zeros((N_GROUPS,) + tile, np.float32)
    slopes8[:, :HEADS, :] = _alibi_slopes()[:, :, None]
    slopes8 = jnp.asarray(slopes8)
    caches = [caches_in[g].reshape((depth, bd, N_STEPS, DILATIONS[g] * tile[0], tile[1]))
              for g in range(N_GROUPS)]

    xp = x_prompt.reshape(batch * seq, d_model)
    xs = x_sample.reshape(bd, d_model)
    p_states = s_states = None
    s_v = []
    wpa, wps, wo = w_proj_att.astype(BF16), w_proj_sgu.astype(BF16), w_out.astype(BF16)
    qw = jnp.tile(q_norm_w[:, :, None, :], (1, 1, HEADS, 1)).reshape(depth, 1, QKV_COLS)
    kw = jnp.tile(k_norm_w[:, :, None, :], (1, 1, HEADS, 1)).reshape(depth, 1, QKV_COLS)
    lnw, lnb = sgu_ln_w.reshape(depth, 1, -1), sgu_ln_b.reshape(depth, 1, -1)
    n1w, n2w = norm1_w.reshape(depth, 1, -1), norm2_w.reshape(depth, 1, -1)
    for l in range(depth):
        mod_p = mod[l, :batch].reshape(batch, 1, -1)
        mod_s = mod[l, batch:n_c].reshape(1, bd, -1)
        bs_p = jnp.repeat(b_spatial[l].T, SGU_GROUP_DIM, axis=1)
        ws_s = jnp.repeat(w_spatial[l][:, 0, 0], SGU_GROUP_DIM).reshape(1, SGU_WIDTH)
        bs_s = jnp.repeat(b_spatial[l][:, 0], SGU_GROUP_DIM).reshape(1, SGU_WIDTH)

        outs = _inproj_call(xp, mod_p, n1w, w_in_l, qw, kw,
                            seq=seq, tm=INPROJ_TILE_ROWS, deinterleave=True, act_dtype=BF16,
                            layer=l, depth=depth, prev_states=p_states)
        qkv, p_states = outs[:9], outs[9:]
        casts = [(w_ffn_in, l), (w_ffn_out, l)] + ([(w_in, l + 1)] if l + 1 < depth else [])
        ya, wfi, wfo, *w_in_next = _attn_call(qkv, band_bias, casts, batch=batch, seq=seq)

        outs = _inproj_call(xs, mod_s, n1w, w_in_l, qw, kw,
                            seq=bd, tm=bd, deinterleave=False, act_dtype=F32,
                            layer=l, depth=depth, prev_states=s_states)
        q_tiles, s_states = [o.reshape((bd,) + tile) for o in outs[:3]], outs[3:]
        new_tiles = [s.reshape((depth * bd,) + tile) for s in s_states]
        ya_s = _attn_sample_call(q_tiles, new_tiles, caches, l, slopes8)

        xp, xs, vs = _post_call(xp, mod_p, ya, w_spatial[l], bs_p, xs, mod_s, ya_s, ws_s, bs_s,
                                n1w, w_in_l, lnw, lnb, wpa, wps, wo, n2w, wfi, wfo,
                                seq=seq, tm=POST_TILE_ROWS, layer=l)
        s_v.append(vs.reshape(bd, 1, SGU_WIDTH))
        if w_in_next:
            (w_in_l,) = w_in_next

    p_states =[s.reshape(depth, batch, -1, 2, HEADS, HEAD_DIM) for s in p_states]
    s_states = [s.reshape(depth, bd, 1, 2, HEADS, HEAD_DIM) for s in s_states]
    return (xp.reshape(batch, seq, d_model), xs.reshape(bd, 1, d_model),
            p_states[0], p_states[1], p_states[2], s_states[0], s_states[1], s_states[2],
            jnp.stack(s_v))
```

```python
import functools
import math

import jax
import jax.numpy as jnp
import numpy as np
from jax.experimental import pallas as pl
from jax.experimental.pallas import tpu as pltpu

HEAD_DIM = 128
N_GROUPS = 3
HEADS = 4
GROUP_COLS = HEADS * HEAD_DIM
QKV_COLS = N_GROUPS * GROUP_COLS
WINDOWS = (128, 512, 2048)
DILATIONS = (1, 4, 16)
N_STEPS = 128
CHUNK = 128
SGU_WIDTH = 512
SGU_GROUPS = 8
SGU_GROUP_DIM = SGU_WIDTH // SGU_GROUPS
N_ADA = 6
EPS = 1e-6
NEG_INF = -1e30
LOG2E = math.log2(math.e)
QK_SCALE = HEAD_DIM ** -0.5 * LOG2E

LANES = 128
RELAYOUT_STRIDE = 4
VMEM_LIMIT_BYTES = 56 * 1024 * 1024

INPROJ_TILE_ROWS = 512
POST_TILE_ROWS = 512
ADALN_TILE_COLS = 1536
SAMPLE_ATTN_SEQS = 8
ATTN_UNROLL = 32

BF16 = jnp.bfloat16
F32 = jnp.float32


def _params(n_axes):
    return pltpu.CompilerParams(dimension_semantics=("arbitrary",) * n_axes,
                                vmem_limit_bytes=VMEM_LIMIT_BYTES)


def _resident(shape):
    nd = len(shape)
    return pl.BlockSpec(shape, lambda *_: (0,) * nd, pipeline_mode=pl.Buffered(1))


def _resident_layer(shape, layer):
    nd = len(shape)
    return pl.BlockSpec((None,) + tuple(shape[1:]), lambda *_: (layer,) + (0,) * (nd - 1),
                        pipeline_mode=pl.Buffered(1))


def _dot(a, b):
    return jnp.dot(a, b, preferred_element_type=F32)


def _dot_nt(a, b):
    return jax.lax.dot_general(a, b, (((1,), (1,)), ((), ())), preferred_element_type=F32)


def _rms(x):
    return x * jax.lax.rsqrt(jnp.mean(x * x, axis=-1, keepdims=True) + EPS)


def _gelu(x):
    return 0.5 * x * (1.0 + jax.lax.erf(x * (2.0 ** -0.5)))


def _sigmoid(x):
    return 1.0 / (1.0 + jnp.exp(-x))


def _cast_specs(casts, n_steps, step):
    ins, outs, shapes = [], [], []
    for w, layer in casts:
        _, k_dim, n_dim = w.shape
        band = k_dim // n_steps
        assert band * n_steps == k_dim and band % 16 == 0
        ins.append(pl.BlockSpec((None, band, n_dim), lambda *g, layer=layer: (layer, step(*g), 0)))
        outs.append(pl.BlockSpec((band, n_dim), lambda *g: (step(*g), 0)))
        shapes.append(jax.ShapeDtypeStruct((k_dim, n_dim), BF16))
    return ins, outs, shapes


def _ada_kernel(c_ref, w_ref, b_ref, *refs):
    n_cast = (len(refs) - 1) // 2
    o_ref = refs[n_cast]
    for src, dst in zip(refs[:n_cast], refs[n_cast + 1:]):
        dst[...] = src[...].astype(dst.dtype)
    c = c_ref[...]
    a = c * _sigmoid(c)
    w = w_ref[...]
    a_hi = a.astype(BF16)
    a_lo = (a - a_hi.astype(F32)).astype(BF16)
    w_hi = w.astype(BF16)
    w_lo = (w - w_hi.astype(F32)).astype(BF16)
    rows = a.shape[0]
    both = _dot(jnp.concatenate([a_hi, a_lo], axis=0), w_hi)
    o_ref[...] = both[:rows] + both[rows:] + _dot(a_hi, w_lo) + b_ref[...]


def _ada_call(c_all, w_ada, b_ada, casts):
    depth, d, n = w_ada.shape
    rows = c_all.shape[0]
    tn = ADALN_TILE_COLS
    per_layer = n // tn
    cast_in, cast_out, cast_shape = _cast_specs(casts, depth * per_layer, lambda l, j: l * per_layer + j)
    return pl.pallas_call(
        _ada_kernel,
        grid=(depth, per_layer),
        in_specs=[pl.BlockSpec((rows, d), lambda l, j: (0, 0)),
                  pl.BlockSpec((None, d, tn), lambda l, j: (l, 0, j)),
                  pl.BlockSpec((None, 1, tn), lambda l, j: (l, 0, j))] + cast_in,
        out_specs=[pl.BlockSpec((None, rows, tn), lambda l, j: (l, 0, j))] + cast_out,
        out_shape=[jax.ShapeDtypeStruct((depth, rows, n), F32)] + cast_shape,
        compiler_params=_params(2),
        name="adaln_mod",
    )(c_all, w_ada, b_ada.reshape(depth, 1, n), *[w for w, _ in casts])


def _modulated_norm(x, w, mod, which, d_model):
    shift = mod[:, 3 * which * d_model:(3 * which + 1) * d_model]
    scale = mod[:, (3 * which + 1) * d_model:(3 * which + 2) * d_model]
    return _rms(x) * w * (1.0 + scale) + shift


def _inproj_kernel(x_ref, mod_ref, n1w_ref, w_ref, qw_ref, kw_ref,
                   *refs, tm, d_model, dils, state_rows, tiles_per_seq, deinterleave, n_aliased):
    refs = refs[n_aliased:]
    if deinterleave:
        q_refs, k_refs, v_refs, st_refs = refs[0:9:3], refs[1:9:3], refs[2:9:3], refs[9:12]
        h_scr, slab, slab2, kv_stash = refs[12:]
    else:
        q_refs, st_refs = refs[0:3], refs[3:6]
        h_scr, slab, slab2, kv_stash = refs[6:]
    ti = pl.program_id(0) % tiles_per_seq
    tile_rows = 2 * HEADS

    h = _modulated_norm(x_ref[...], n1w_ref[...], mod_ref[0], 0, d_model)
    h_scr[...] = h.astype(BF16)

    def proj(c0, width):
        return _dot(h_scr[...], w_ref[:, c0:c0 + width])

    def put_tiles(ref, first_row, val):
        for hh in range(HEADS):
            ref[pl.ds(first_row + hh, tm, stride=tile_rows), :] = val[:, hh * HEAD_DIM:(hh + 1) * HEAD_DIM]

    def put(ref, g, val, slot, slabs=None):
        d = dils[g]
        if d == 1:
            ref[...] = val.astype(ref.dtype)
            return
        if slabs is None:
            slabs = [slab.at[g - 1, hh] for hh in range(HEADS)]
            for hh in range(HEADS):
                slabs[hh][...] = val[:, hh * HEAD_DIM:(hh + 1) * HEAD_DIM]
        if d <= RELAYOUT_STRIDE:
            for r in range(d):
                for hh in range(HEADS):
                    ref[0, r, :, hh * HEAD_DIM:(hh + 1) * HEAD_DIM] = (
                        slabs[hh][pl.ds(r, tm // d, stride=d), :].astype(ref.dtype))
            return
        s1, s2 = RELAYOUT_STRIDE, d // RELAYOUT_STRIDE
        part = tm // s1
        for hh in range(HEADS):
            for r1 in range(s1):
                slab2[slot, hh,r1 * part:(r1 + 1) * part, :] = slabs[hh][pl.ds(r1, part, stride=s1), :]
            for r1 in range(s1):
                for r2 in range(s2):
                    ref[0, r1 + s1 * r2, :, hh * HEAD_DIM:(hh + 1) * HEAD_DIM] = (
                        slab2[slot, hh,pl.ds(r1 * part + r2, tm // d, stride=s2), :].astype(ref.dtype))

    def head_rms(z, w):
        parts = [_rms(z[:, hh * HEAD_DIM:(hh + 1) * HEAD_DIM]) for hh in range(HEADS)]
        return jnp.concatenate(parts, axis=-1) * w

    def put_state(g, kn, v):
        if state_rows[g] is None:
            put_tiles(st_refs[g], 0, kn)
            put_tiles(st_refs[g], HEADS, v)
        else:
            for hh in range(HEADS):
                cols = slice(hh * HEAD_DIM, (hh + 1) * HEAD_DIM)
                kv_stash[g, hh] = kn[:, cols]
                kv_stash[g, HEADS + hh] = v[:, cols]

    def flush_state(g):
        st, rows = st_refs[g], min(state_rows[g], tm)
        first = tiles_per_seq - max(state_rows[g] // tm, 1)

        @pl.when(ti >= first)
        def _():
            for j in range(2 * HEADS):
                st[pl.ds(j, rows, stride=2 * HEADS), :] = kv_stash[g, j, tm - rows:, :]

    for g in reversed(range(N_GROUPS)):
        c0 = g * GROUP_COLS
        q = head_rms(proj(c0, GROUP_COLS), qw_ref[:, c0:c0 + GROUP_COLS] * QK_SCALE)
        kn = head_rms(proj(QKV_COLS + c0, GROUP_COLS), kw_ref[:, c0:c0 + GROUP_COLS])
        v = proj(2 * QKV_COLS + c0, GROUP_COLS)
        put_state(g, kn, v)
        if not deinterleave:
            put_tiles(q_refs[g], 0, q)
            put_tiles(q_refs[g], HEADS, jnp.zeros_like(q))
            continue
        put(q_refs[g], g, q, 0)
        put(k_refs[g], g, kn, 1, [kv_stash.at[g, hh] for hh in range(HEADS)])
        put(v_refs[g], g, v, 2, [kv_stash.at[g, HEADS + hh] for hh in range(HEADS)])

    for g in range(N_GROUPS):
        if state_rows[g] is not None:
            flush_state(g)


def _inproj_call(x, mod, n1w, w_in, qw, kw, *, seq, tm, deinterleave, act_dtype,
                 layer=0, depth=1, prev_states=None):
    t, d_model = x.shape
    batch = t // seq
    tiles_per_seq = seq // tm

    def tok(cols, dtype):
        return (jax.ShapeDtypeStruct((t, cols), dtype), pl.BlockSpec((tm, cols), lambda i: (i, 0)))

    def grp(g, dtype):
        d = DILATIONS[g]
        if d == 1:
            return tok(GROUP_COLS, dtype)
        assert tm % d == 0 and (tm // d) % 16 == 0
        return (jax.ShapeDtypeStruct((batch, d, seq // d, GROUP_COLS), dtype),
                pl.BlockSpec((1, d, tm // d, GROUP_COLS),
                             lambda i: (i // tiles_per_seq, 0, i % tiles_per_seq, 0)))

    tile_rows = 2 * HEADS
    outs = []
    for g in range(N_GROUPS):
        if deinterleave:
            outs += [grp(g, act_dtype)] * 3
        else:
            assert t == tm
            outs.append((jax.ShapeDtypeStruct((t * tile_rows, LANES), F32),
                         pl.BlockSpec((t * tile_rows, LANES), lambda i: (0, 0))))

    state_rows = []
    for g in range(N_GROUPS):
        if not deinterleave:
            state_rows.append(None)
            outs.append((jax.ShapeDtypeStruct((depth * t * tile_rows, LANES), F32),
                         pl.BlockSpec((t * tile_rows, LANES), lambda i: (layer, 0))))
            continue
        rows = min(WINDOWS[g], seq)
        state_rows.append(rows)
        blk_rows = min(rows, tm)
        assert rows % blk_rows == 0
        nblk = rows // blk_rows
        shape = jax.ShapeDtypeStruct((depth * batch * rows * 2 * HEADS, LANES), F32)
        spec = pl.BlockSpec(
            (blk_rows * 2 * HEADS, LANES),
            lambda i, nblk=nblk: ((layer * batch + i // tiles_per_seq) * nblk
                                  + jnp.maximum(i % tiles_per_seq - (tiles_per_seq - nblk), 0), 0))
        outs.append((shape, spec))

    aliased = list(prev_states) if prev_states is not None else []
    n_in = 6
    w_qkv = pl.BlockSpec((d_model, 3 * QKV_COLS), lambda i: (0, 0),
                         pipeline_mode=pl.Buffered(1))
    kern = functools.partial(_inproj_kernel, tm=tm, d_model=d_model, dils=DILATIONS,
                             state_rows=tuple(state_rows), tiles_per_seq=tiles_per_seq,
                             deinterleave=deinterleave, n_aliased=len(aliased))
    return pl.pallas_call(
        kern,
        grid=(t // tm,),
        in_specs=[pl.BlockSpec((tm, d_model), lambda i: (i, 0)),
                  pl.BlockSpec((1,) + mod.shape[1:], lambda i: (i // tiles_per_seq, 0, 0)),
                  _resident_layer(n1w.shape, layer),
                  w_qkv,
                  _resident_layer(qw.shape, layer), _resident_layer(kw.shape, layer)]
                 + [pl.BlockSpec(memory_space=pl.ANY)] * len(aliased),
        out_specs=[o[1] for o in outs],
        out_shape=[o[0] for o in outs],
        input_output_aliases={n_in + g: len(outs) - N_GROUPS + g for g in range(len(aliased))},
        scratch_shapes=[pltpu.VMEM((tm, d_model), BF16),
                        pltpu.VMEM((N_GROUPS - 1, HEADS, tm, LANES), F32),
                        pltpu.VMEM((3, HEADS, tm, LANES), F32),
                        pltpu.VMEM((N_GROUPS, 2 * HEADS, tm, LANES), F32)],
        compiler_params=_params(1),
        name="inproj_deint" if deinterleave else "inproj_tok",
    )(x, mod, n1w, w_in, qw, kw, *aliased)


def _alibi_slopes():
    h = np.arange(1, N_GROUPS * HEADS + 1, dtype=np.float32)
    s = np.power(np.float32(2.0), -8.0 * h / (N_GROUPS * HEADS)).astype(np.float32)
    return (s.astype(np.float64) * LOG2E).astype(np.float32).reshape(N_GROUPS, HEADS)


def _band_bias():
    qi = np.arange(N_STEPS)[:, None]
    ki = np.arange(2 * N_STEPS)[None, :]
    dist = N_STEPS + qi - ki
    valid = (dist >= 0) & (dist <= N_STEPS)
    slopes = _alibi_slopes()
    out = np.empty((N_GROUPS, HEADS, 2, N_STEPS, 2 * N_STEPS), np.float32)
    for g in range(N_GROUPS):
        for hh in range(HEADS):
            b = np.where(valid, -slopes[g, hh] * (dist * DILATIONS[g]).astype(np.float32),
                         np.float32(NEG_INF)).astype(np.float32)
            out[g, hh, 1] = b
            first = b.copy()
            first[:, :N_STEPS] = NEG_INF
            out[g, hh, 0] = first
    return out


def _attn_block(q, k2, v2, bias):
    s = _dot_nt(q, k2) + bias
    m = jnp.max(s, axis=-1, keepdims=True)
    p = jnp.exp2(s - m)
    l = jnp.sum(p, axis=-1, keepdims=True)
    return _dot(p.astype(v2.dtype), v2), m, l


def _attn_kernel(q0, k0, v0, q1, k1, v1, q2, k2, v2, bias_ref, *refs, seq, n_cast):
    cast_src, refs = refs[:n_cast], refs[n_cast:]
    o_ref, cast_dst = refs[0], refs[1:1 + n_cast]
    o_scr, m_scr, l_scr, stage_scr = refs[1 + n_cast:]
    blk = N_STEPS
    s1 = RELAYOUT_STRIDE

    for src, dst in zip(cast_src, cast_dst):
        dst[...] = src[...].astype(dst.dtype)

    def run_group(g, qr, kr, vr):
        d = DILATIONS[g]
        nblk = seq // d // blk

        def one_block(r, i):
            own = pl.multiple_of(i * blk, blk)
            prev = pl.multiple_of(jnp.maximum(i - 1, 0) * blk, blk)
            if d == 1:
                ld = lambda ref, start: ref[pl.ds(start, blk), :]
            else:
                ld = lambda ref, start: ref[0, r, pl.ds(start, blk), :]
            k2_ = jnp.concatenate([ld(kr, prev), ld(kr, own)], axis=0)
            v2_ = jnp.concatenate([ld(vr, prev), ld(vr, own)], axis=0)
            o, m, l = _attn_block(ld(qr, own), k2_, v2_, bias_ref[g, 0, jnp.minimum(i, 1)])
            m = jnp.broadcast_to(m, (blk, LANES))
            l = jnp.broadcast_to(l, (blk, LANES))
            if 1 < d <= s1:
                rows = pl.ds(i * (blk * d) + r, blk, stride=d)
                o_scr[g - 1, rows, :] = o
                m_scr[g - 1, rows, :] = m
                l_scr[g - 1, rows, :] = l
                return
            if d > s1:
                s2 = d // s1
                rows = pl.ds((r % s1) * (seq // s1) + i * (blk * s2) + r // s1, blk, stride=s2)
                stage_scr[0, rows, :] = o
                stage_scr[1, rows, :] = m
                stage_scr[2, rows, :] = l
                return
            rows = pl.ds(own, blk)
            ms = [m, m_scr[0, rows, :], m_scr[1, rows, :]]
            mx = jnp.maximum(jnp.maximum(ms[0], ms[1]), ms[2])
            ws = [jnp.exp2(x - mx) for x in ms]
            num = ws[0] * o + ws[1] * o_scr[0, rows, :] + ws[2] * o_scr[1, rows, :]
            den = ws[0] * l + ws[1] * l_scr[0, rows, :] + ws[2] * l_scr[1, rows, :]
            o_ref[rows, :] = (num / den).astype(o_ref.dtype)

        ur = min(ATTN_UNROLL, d)
        ui = ATTN_UNROLL // ur
        assert d % ur == 0 and nblk % ui == 0

        def body(it, carry):
            i0 = (it // (d // ur)) * ui
            r0 = (it % (d // ur)) * ur
            for a in range(ui):
                for b in range(ur):
                    one_block(r0 + b, i0 + a)
            return carry

        jax.lax.fori_loop(0, (d // ur) * (nblk // ui), body, 0)

    run_group(2, q2, k2, v2)
    run_group(1, q1, k1, v1)
    g = N_GROUPS - 1
    assert DILATIONS[g] > s1 and all(d <= s1 for d in DILATIONS[:g])
    per_stream = seq // s1
    for a, dst in enumerate((o_scr, m_scr, l_scr)):
        for r1 in range(s1):
            for c in range(per_stream // blk):
                dst[g - 1, pl.ds(s1 * c * blk + r1, blk, stride=s1), :] = (
                    stage_scr[a, r1 * per_stream + c * blk:r1 * per_stream + (c + 1) * blk, :])
    run_group(0, q0, k0, v0)


def _attn_call(qkv, bias, casts, *, batch, seq):
    cast_in, cast_out, cast_shape = _cast_specs(casts, batch * HEADS, lambda b, h: b * HEADS + h)
    in_specs = []
    for g in range(N_GROUPS):
        d = DILATIONS[g]
        if d == 1:
            spec = pl.BlockSpec((seq, HEAD_DIM), lambda b, h: (b, h))
        else:
            spec = pl.BlockSpec((1, d, seq // d, HEAD_DIM), lambda b, h: (b, 0, 0, h))
        in_specs += [spec] * 3
    in_specs.append(pl.BlockSpec((N_GROUPS, 1, 2, N_STEPS, 2 * N_STEPS), lambda b, h: (0, h, 0, 0, 0)))
    return pl.pallas_call(
        functools.partial(_attn_kernel, seq=seq, n_cast=len(casts)),
        grid=(batch, HEADS),
        in_specs=in_specs + cast_in,
        out_specs=[pl.BlockSpec((seq, HEAD_DIM), lambda b, h: (b, h))] + cast_out,
        out_shape=[jax.ShapeDtypeStruct((batch * seq, GROUP_COLS), BF16)] + cast_shape,
        scratch_shapes=[pltpu.VMEM((N_GROUPS - 1, seq, LANES), F32)] * 3
                       + [pltpu.VMEM((3, seq, LANES), F32)],
        compiler_params=_params(2),
        name="attn_prompt",
    )(*qkv, bias, *[w for w, _ in casts])


def _attn_sample_kernel(q0, q1, q2, n0, n1, n2, c0, c1, c2, slope_ref, o_ref, *, bb):
    q_refs, n_refs, c_refs = (q0, q1, q2), (n0, n1, n2), (c0, c1, c2)
    half = N_STEPS // 2
    tile = (2 * HEADS, HEAD_DIM)
    top = jax.lax.broadcasted_iota(jnp.int32, tile, 0) < HEADS
    steps_a = (N_STEPS - jax.lax.broadcasted_iota(jnp.int32, (half, 1, 1), 0)).astype(F32)
    back = jnp.where(top[:, 0:1], steps_a, steps_a - float(half))

    def both_halves(t):
        return jnp.where(top, t, pltpu.roll(t, HEADS, axis=0))

    for b in range(bb):
        outs, lses = [], []
        for g in range(N_GROUPS):
            q8, kv8 = q_refs[g][b], n_refs[g][b]
            xa, xb = c_refs[g][b, 0:half], c_refs[g][b, half:N_STEPS]
            slope = both_halves(slope_ref[g])[:, 0:1]
            bias = (back * (-float(DILATIONS[g]))) * slope
            prod = jnp.where(top, xa * q8, pltpu.roll(xb * q8, HEADS, axis=1))
            s = jnp.sum(prod, axis=-1, keepdims=True) + bias
            s_self = both_halves(jnp.broadcast_to(jnp.sum(kv8 * q8, axis=-1, keepdims=True), tile))
            m = jnp.broadcast_to(jnp.max(s, axis=0), tile)
            m = jnp.maximum(jnp.maximum(m, pltpu.roll(m, HEADS, axis=0)), s_self)
            p = jnp.exp2(s - m[:, 0:1])
            p_self = jnp.exp2(s_self - m)
            l = jnp.broadcast_to(jnp.sum(p, axis=0), tile)
            l = l + pltpu.roll(l, HEADS, axis=0) + p_self
            p_b = jnp.broadcast_to(p, xb.shape)
            acc = jnp.sum(pltpu.roll(p_b, HEADS, axis=1) * xa + p_b * xb, axis=0) + p_self * kv8
            outs.append(acc / l)
            lses.append(m + jnp.log(l) * LOG2E)
        mx = jnp.maximum(jnp.maximum(lses[0], lses[1]), lses[2])
        ws = [jnp.exp2(v - mx) for v in lses]
        num = ws[0] * outs[0] + ws[1] * outs[1] + ws[2] * outs[2]
        o8 = num / (ws[0] + ws[1] + ws[2])
        for hh in range(HEADS):
            o_ref[b:b + 1, hh * HEAD_DIM:(hh + 1) * HEAD_DIM] = o8[HEADS + hh:HEADS + hh + 1, :]


def _attn_sample_call(q_tiles, new_tiles, caches, layer, slopes8):
    bd = q_tiles[0].shape[0]
    bb = SAMPLE_ATTN_SEQS
    nb = bd // bb
    tile = (2 * HEADS, HEAD_DIM)
    in_specs = [pl.BlockSpec((bb,) + tile, lambda i: (i, 0, 0))] * 3
    in_specs += [pl.BlockSpec((bb,) + tile, lambda i: (layer * nb + i, 0, 0))] * 3
    in_specs += [pl.BlockSpec((None, bb, N_STEPS) + tile, lambda i: (layer, i, 0, 0, 0))] * 3
    in_specs.append(pl.BlockSpec((N_GROUPS,) + tile, lambda i: (0, 0, 0)))
    return pl.pallas_call(
        functools.partial(_attn_sample_kernel, bb=bb),
        grid=(nb,),
        in_specs=in_specs,
        out_specs=pl.BlockSpec((bb, GROUP_COLS), lambda i: (i, 0)),
        out_shape=jax.ShapeDtypeStruct((bd, GROUP_COLS), F32),
        compiler_params=_params(1),
        name="attn_sample",
    )(*q_tiles, *new_tiles, *caches, slopes8)


def _post_math(x, mod, ya, w, ws_ref, bs_ref, mix_scr, *, chunk_rows, d_model, d_ff):
    g1 = mod[:, 2 * d_model:3 * d_model]
    g2 = mod[:, 5 * d_model:6 * d_model]

    h = _modulated_norm(x, w["n1w"][...], mod, 0, d_model).astype(BF16)
    sw = SGU_WIDTH
    wg1, wg2 = w["wg1"], w["wg2"]
    u = _gelu(_dot(h, wg1[:, 0:sw]))
    vs = _gelu(_dot(h, wg1[:, sw:2 * sw]))
    mu = jnp.mean(vs, axis=-1, keepdims=True)
    vc = vs - mu
    var = jnp.mean(vc * vc, axis=-1, keepdims=True)
    vs = vc * jax.lax.rsqrt(var + EPS) * w["lnw"][...] + w["lnb"][...]

    if chunk_rows is not None:
        row = jax.lax.broadcasted_iota(jnp.int32, (CHUNK, CHUNK), 0)
        col = jax.lax.broadcasted_iota(jnp.int32, (CHUNK, CHUNK), 1)
        tril = col <= row
        lane = jax.lax.broadcasted_iota(jnp.int32, (CHUNK, LANES), 1)
        low = lane < SGU_GROUP_DIM
        wms = [jnp.where(tril, ws_ref[gi], 0.0).astype(BF16) for gi in range(SGU_GROUPS)]
        vs_b = vs.astype(BF16)
        for c in range(chunk_rows // CHUNK):
            rows = slice(c * CHUNK, (c + 1) * CHUNK)
            for j in range(SGU_WIDTH // LANES):
                cols = slice(j * LANES, (j + 1) * LANES)
                v = vs_b[rows, cols]
                mix_scr[rows, cols] = jnp.where(low, _dot(wms[2 * j], v), _dot(wms[2 * j + 1], v))
        mixed = mix_scr[...] + jnp.concatenate([bs_ref[...]] * (chunk_rows // CHUNK), axis=0)
    else:
        mixed = vs * ws_ref[...] + bs_ref[...]

    ys = (u * mixed).astype(BF16)
    half = d_model // 2
    xa = _dot(ya.astype(BF16), w["wpa"][...])
    ga = jnp.concatenate([_sigmoid(_dot(h, wg1[:, 2 * sw:2 * sw + half])),
                          _sigmoid(_dot(h, wg2[:, 0:half]))], axis=-1)
    gated_att = ga * xa
    xs = _dot(ys, w["wps"][...])
    gb = _sigmoid(_dot(h, wg2[:, half:half + d_model]))
    merged = (gated_att + gb * xs).astype(BF16)
    x1 = x + g1 * _dot(merged, w["wo"][...])
    h2 = _modulated_norm(x1, w["n2w"][...], mod, 1, d_model).astype(BF16)
    a = _dot(h2, w["wfi"][:, 0:d_ff])
    b = _dot(h2, w["wfi"][:, d_ff:2 * d_ff])
    act = (a * _sigmoid(a) * b).astype(BF16)
    return x1 + g2 * _dot(act, w["wfo"][...]), vs


_POST_WEIGHTS = ("n1w", "wg1", "wg2", "lnw", "lnb", "wpa", "wps", "wo", "n2w", "wfi", "wfo")


def _post_kernel(x_ref, mod_ref, ya_ref, ws_ref, bs_ref, xs_ref, mods_ref, yas_ref, wss_ref, bss_ref,
                 *refs, tm, d_model, d_ff):
    n_w = len(_POST_WEIGHTS)
    w = dict(zip(_POST_WEIGHTS, refs[:n_w]))
    o_ref, os_ref, vs_ref, mix_scr = refs[n_w:]
    o_ref[...], _ = _post_math(x_ref[...], mod_ref[0], ya_ref[...], w, ws_ref, bs_ref, mix_scr,
                               chunk_rows=tm, d_model=d_model, d_ff=d_ff)

    @pl.when(pl.program_id(0) == pl.num_programs(0) - 1)
    def _():
        os_ref[...], vs_ref[...] = _post_math(xs_ref[...], mods_ref[0], yas_ref[...], w, wss_ref,
                                              bss_ref, None, chunk_rows=None,
                                              d_model=d_model, d_ff=d_ff)


def _post_call(x, mod, ya, ws, bs, xs, mod_s, ya_s, ws_s, bs_s,
               n1w, w_in, lnw, lnb, wpa, wps, wo, n2w, wfi, wfo, *, seq, tm, layer):
    t, d_model = x.shape
    ts = xs.shape[0]
    d_ff = wfo.shape[0]
    tiles_per_seq = seq // tm
    gblk = QKV_COLS
    assert 2 * SGU_WIDTH + d_model // 2 == gblk and w_in.shape[1] == 5 * gblk

    def w_gate(j):
        return pl.BlockSpec((d_model, gblk), lambda i: (0, 3 + j), pipeline_mode=pl.Buffered(1))

    def tok(cols):
        return pl.BlockSpec((tm, cols), lambda i: (i, 0))

    weight_specs = [_resident_layer(n1w.shape, layer), w_gate(0), w_gate(1),
                    _resident_layer(lnw.shape, layer), _resident_layer(lnb.shape, layer),
                    _resident_layer(wpa.shape, layer), _resident_layer(wps.shape, layer),
                    _resident_layer(wo.shape, layer), _resident_layer(n2w.shape, layer),
                    _resident(wfi.shape), _resident(wfo.shape)]
    assert len(weight_specs) == len(_POST_WEIGHTS)
    kern = functools.partial(_post_kernel, tm=tm, d_model=d_model, d_ff=d_ff)
    return pl.pallas_call(
        kern,
        grid=(t // tm,),
        in_specs=[tok(d_model),
                  pl.BlockSpec((1,) + mod.shape[1:], lambda i: (i // tiles_per_seq, 0, 0)),
                  tok(GROUP_COLS), _resident(ws.shape), _resident(bs.shape),
                  _resident(xs.shape), _resident(mod_s.shape), _resident(ya_s.shape),
                  _resident(ws_s.shape), _resident(bs_s.shape)] + weight_specs,
        out_specs=[tok(d_model),
                   pl.BlockSpec((ts, d_model), lambda i: (0, 0)),
                   pl.BlockSpec((ts, SGU_WIDTH), lambda i: (0, 0))],
        out_shape=[jax.ShapeDtypeStruct((t, d_model), F32),
                   jax.ShapeDtypeStruct((ts, d_model), F32),
                   jax.ShapeDtypeStruct((ts, SGU_WIDTH), F32)],
        scratch_shapes=[pltpu.VMEM((tm, SGU_WIDTH), F32)],
        compiler_params=_params(1),
        name="post",
    )(x, mod, ya, ws, bs, xs, mod_s, ya_s, ws_s, bs_s,
      n1w, w_in, w_in, lnw, lnb, wpa, wps, wo, n2w, wfi, wfo)


def kernel(x_prompt, x_sample, cache_kv_w128, cache_kv_w512, cache_kv_w2048, c_prompt, c_sample,
           w_ada, b_ada, norm1_w, w_in, q_norm_w, k_norm_w, sgu_ln_w, sgu_ln_b, w_spatial, b_spatial,
           w_proj_att, w_proj_sgu, w_out, norm2_w, w_ffn_in, w_ffn_out):
    batch, seq, d_model = x_prompt.shape
    bd, dec_seq, _ = x_sample.shape
    depth = w_in.shape[0]
    assert dec_seq == 1 and seq % (N_STEPS * DILATIONS[-1]) == 0
    caches_in = (cache_kv_w128, cache_kv_w512, cache_kv_w2048)
    for g in range(N_GROUPS):
        assert caches_in[g].shape[2] == WINDOWS[g]

    n_c = batch + bd
    c_all = jnp.concatenate([c_prompt, c_sample], axis=0)
    c_all = jnp.pad(c_all, ((0, -n_c % 16), (0, 0)))
    mod, w_in_l = _ada_call(c_all, w_ada, b_ada, [(w_in, 0)])

    band_bias = jnp.asarray(_band_bias())
    tile = (2 * HEADS, HEAD_DIM)
    slopes8 = np.zeros((N_GROUPS,) + tile, np.float32)
    slopes8[:, :HEADS, :] = _alibi_slopes()[:, :, None]
    slopes8 = jnp.asarray(slopes8)
    caches = [caches_in[g].reshape((depth, bd, N_STEPS, DILATIONS[g] * tile[0], tile[1]))
              for g in range(N_GROUPS)]

    xp = x_prompt.reshape(batch * seq, d_model)
    xs = x_sample.reshape(bd, d_model)
    p_states = s_states = None
    s_v = []
    wpa, wps, wo = w_proj_att.astype(BF16), w_proj_sgu.astype(BF16), w_out.astype(BF16)
    qw = jnp.tile(q_norm_w[:, :, None, :], (1, 1, HEADS, 1)).reshape(depth, 1, QKV_COLS)
    kw = jnp.tile(k_norm_w[:, :, None, :], (1, 1, HEADS, 1)).reshape(depth, 1, QKV_COLS)
    lnw, lnb = sgu_ln_w.reshape(depth, 1, -1), sgu_ln_b.reshape(depth, 1, -1)
    n1w, n2w = norm1_w.reshape(depth, 1, -1), norm2_w.reshape(depth, 1, -1)
    for l in range(depth):
        mod_p = mod[l, :batch].reshape(batch, 1, -1)
        mod_s = mod[l, batch:n_c].reshape(1, bd, -1)
        bs_p = jnp.repeat(b_spatial[l].T, SGU_GROUP_DIM, axis=1)
        ws_s = jnp.repeat(w_spatial[l][:, 0, 0], SGU_GROUP_DIM).reshape(1, SGU_WIDTH)
        bs_s = jnp.repeat(b_spatial[l][:, 0], SGU_GROUP_DIM).reshape(1, SGU_WIDTH)

        outs = _inproj_call(xp, mod_p, n1w, w_in_l, qw, kw,
                            seq=seq, tm=INPROJ_TILE_ROWS, deinterleave=True, act_dtype=BF16,
                            layer=l, depth=depth, prev_states=p_states)
        qkv, p_states = outs[:9], outs[9:]
        casts = [(w_ffn_in, l), (w_ffn_out, l)] + ([(w_in, l + 1)] if l + 1 < depth else [])
        ya, wfi, wfo, *w_in_next = _attn_call(qkv, band_bias, casts, batch=batch, seq=seq)

        outs = _inproj_call(xs, mod_s, n1w, w_in_l, qw, kw,
                            seq=bd, tm=bd, deinterleave=False, act_dtype=F32,
                            layer=l, depth=depth, prev_states=s_states)
        q_tiles, s_states = [o.reshape((bd,) + tile) for o in outs[:3]], outs[3:]
        new_tiles = [s.reshape((depth * bd,) + tile) for s in s_states]
        ya_s = _attn_sample_call(q_tiles, new_tiles, caches, l, slopes8)

        xp, xs, vs = _post_call(xp, mod_p, ya, w_spatial[l], bs_p, xs, mod_s, ya_s, ws_s, bs_s,
                                n1w, w_in_l, lnw, lnb, wpa, wps, wo, n2w, wfi, wfo,
                                seq=seq, tm=POST_TILE_ROWS, layer=l)
        s_v.append(vs.reshape(bd, 1, SGU_WIDTH))
        if w_in_next:
            (w_in_l,) = w_in_next

    p_states =[s.reshape(depth, batch, -1, 2, HEADS, HEAD_DIM) for s in p_states]
    s_states = [s.reshape(depth, bd, 1, 2, HEADS, HEAD_DIM) for s in s_states]
    return (xp.reshape(batch, seq, d_model), xs.reshape(bd, 1, d_model),
            p_states[0], p_states[1], p_states[2], s_states[0], s_states[1], s_states[2],
            jnp.stack(s_v))
```

```python
import functools
import math

import jax
import jax.numpy as jnp
import numpy as np
from jax.experimental import pallas as pl
from jax.experimental.pallas import tpu as pltpu

HEAD_DIM = 128
N_GROUPS = 3
HEADS = 4
GROUP_COLS = HEADS * HEAD_DIM
QKV_COLS = N_GROUPS * GROUP_COLS
WINDOWS = (128, 512, 2048)
DILATIONS = (1, 4, 16)
N_STEPS = 128
CHUNK = 128
SGU_WIDTH = 512
SGU_GROUPS = 8
SGU_GROUP_DIM = SGU_WIDTH // SGU_GROUPS
N_ADA = 6
EPS = 1e-6
NEG_INF = -1e30
LOG2E = math.log2(math.e)
QK_SCALE = HEAD_DIM ** -0.5 * LOG2E

LANES = 128
RELAYOUT_STRIDE = 4
VMEM_LIMIT_BYTES = 56 * 1024 * 1024

INPROJ_TILE_ROWS = 512
POST_TILE_ROWS = 512
ADALN_TILE_COLS = 1536
SAMPLE_ATTN_SEQS = 8
ATTN_UNROLL = 32
FFN_SPLIT = 1536

BF16 = jnp.bfloat16
F32 = jnp.float32


def _params(n_axes):
    return pltpu.CompilerParams(dimension_semantics=("arbitrary",) * n_axes,
                                vmem_limit_bytes=VMEM_LIMIT_BYTES)


def _resident(shape):
    nd = len(shape)
    return pl.BlockSpec(shape, lambda *_: (0,) * nd, pipeline_mode=pl.Buffered(1))


def _resident_layer(shape, layer):
    nd = len(shape)
    return pl.BlockSpec((None,) + tuple(shape[1:]), lambda *_: (layer,) + (0,) * (nd - 1),
                        pipeline_mode=pl.Buffered(1))


def _dot(a, b):
    return jnp.dot(a, b, preferred_element_type=F32)


def _dot_nt(a, b):
    return jax.lax.dot_general(a, b, (((1,), (1,)), ((), ())), preferred_element_type=F32)


def _rms(x):
    return x * jax.lax.rsqrt(jnp.mean(x * x, axis=-1, keepdims=True) + EPS)


def _gelu(x):
    return 0.5 * x * (1.0 + jax.lax.erf(x * (2.0 ** -0.5)))


def _sigmoid(x):
    return 1.0 / (1.0 + jnp.exp(-x))


def _cast_specs(casts, n_steps, step):
    ins, outs, shapes = [], [], []
    for w, layer in casts:
        _, k_dim, n_dim = w.shape
        band = k_dim // n_steps
        assert band * n_steps == k_dim and band % 16 == 0
        ins.append(pl.BlockSpec((None, band, n_dim), lambda *g, layer=layer: (layer, step(*g), 0)))
        outs.append(pl.BlockSpec((band, n_dim), lambda *g: (step(*g), 0)))
        shapes.append(jax.ShapeDtypeStruct((k_dim, n_dim), BF16))
    return ins, outs, shapes


def _ada_kernel(c_ref, w_ref, b_ref, *refs):
    n_cast = (len(refs) - 1) // 2
    o_ref = refs[n_cast]
    for src, dst in zip(refs[:n_cast], refs[n_cast + 1:]):
        dst[...] = src[...].astype(dst.dtype)
    c = c_ref[...]
    a = c * _sigmoid(c)
    w = w_ref[...]
    a_hi = a.astype(BF16)
    a_lo = (a - a_hi.astype(F32)).astype(BF16)
    w_hi = w.astype(BF16)
    w_lo = (w - w_hi.astype(F32)).astype(BF16)
    rows = a.shape[0]
    both = _dot(jnp.concatenate([a_hi, a_lo], axis=0), w_hi)
    o_ref[...] = both[:rows] + both[rows:] + _dot(a_hi, w_lo) + b_ref[...]


def _ada_call(c_all, w_ada, b_ada, casts):
    depth, d, n = w_ada.shape
    rows = c_all.shape[0]
    tn = ADALN_TILE_COLS
    per_layer = n // tn
    cast_in, cast_out, cast_shape = _cast_specs(casts, depth * per_layer, lambda l, j: l * per_layer + j)
    return pl.pallas_call(
        _ada_kernel,
        grid=(depth, per_layer),
        in_specs=[pl.BlockSpec((rows, d), lambda l, j: (0, 0)),
                  pl.BlockSpec((None, d, tn), lambda l, j: (l, 0, j)),
                  pl.BlockSpec((None, 1, tn), lambda l, j: (l, 0, j))] + cast_in,
        out_specs=[pl.BlockSpec((None, rows, tn), lambda l, j: (l, 0, j))] + cast_out,
        out_shape=[jax.ShapeDtypeStruct((depth, rows, n), F32)] + cast_shape,
        compiler_params=_params(2),
        name="adaln_mod",
    )(c_all, w_ada, b_ada.reshape(depth, 1, n), *[w for w, _ in casts])


def _modulated_norm(x, w, mod, which, d_model):
    shift = mod[:, 3 * which * d_model:(3 * which + 1) * d_model]
    scale = mod[:, (3 * which + 1) * d_model:(3 * which + 2) * d_model]
    return _rms(x) * w * (1.0 + scale) + shift


def _inproj_kernel(x_ref, mod_ref, n1w_ref, w_ref, qw_ref, kw_ref,
                   *refs, tm, d_model, dils, state_rows, tiles_per_seq, deinterleave, n_aliased):
    refs = refs[n_aliased:]
    if deinterleave:
        q_refs, k_refs, v_refs, st_refs = refs[0:9:3], refs[1:9:3], refs[2:9:3], refs[9:12]
        h_scr, slab, slab2, kv_stash = refs[12:]
    else:
        q_refs, st_refs = refs[0:3], refs[3:6]
        h_scr, slab, slab2, kv_stash = refs[6:]
    ti = pl.program_id(0) % tiles_per_seq
    tile_rows = 2 * HEADS

    h = _modulated_norm(x_ref[...], n1w_ref[...], mod_ref[0], 0, d_model)
    h_scr[...] = h.astype(BF16)

    def proj(c0, width):
        return _dot(h_scr[...], w_ref[:, c0:c0 + width])

    def put_tiles(ref, first_row, val):
        for hh in range(HEADS):
            ref[pl.ds(first_row + hh, tm, stride=tile_rows), :] = val[:, hh * HEAD_DIM:(hh + 1) * HEAD_DIM]

    def put(ref, g, val, slot, slabs=None):
        d = dils[g]
        if d == 1:
            ref[...] = val.astype(ref.dtype)
            return
        if slabs is None:
            slabs = [slab.at[g - 1, hh] for hh in range(HEADS)]
            for hh in range(HEADS):
                slabs[hh][...] = val[:, hh * HEAD_DIM:(hh + 1) * HEAD_DIM]
        if d <= RELAYOUT_STRIDE:
            for r in range(d):
                for hh in range(HEADS):
                    ref[0, r, :, hh * HEAD_DIM:(hh + 1) * HEAD_DIM] = (
                        slabs[hh][pl.ds(r, tm // d, stride=d), :].astype(ref.dtype))
            return
        s1, s2 = RELAYOUT_STRIDE, d // RELAYOUT_STRIDE
        part = tm // s1
        for hh in range(HEADS):
            for r1 in range(s1):
                slab2[slot, hh,r1 * part:(r1 + 1) * part, :] = slabs[hh][pl.ds(r1, part, stride=s1), :]
            for r1 in range(s1):
                for r2 in range(s2):
                    ref[0, r1 + s1 * r2, :, hh * HEAD_DIM:(hh + 1) * HEAD_DIM] = (
                        slab2[slot, hh,pl.ds(r1 * part + r2, tm // d, stride=s2), :].astype(ref.dtype))

    def head_rms(z, w):
        parts = [_rms(z[:, hh * HEAD_DIM:(hh + 1) * HEAD_DIM]) for hh in range(HEADS)]
        return jnp.concatenate(parts, axis=-1) * w

    def put_state(g, kn, v):
        if state_rows[g] is None:
            put_tiles(st_refs[g], 0, kn)
            put_tiles(st_refs[g], HEADS, v)
        else:
            for hh in range(HEADS):
                cols = slice(hh * HEAD_DIM, (hh + 1) * HEAD_DIM)
                kv_stash[g, hh] = kn[:, cols]
                kv_stash[g, HEADS + hh] = v[:, cols]

    def flush_state(g):
        st, rows = st_refs[g], min(state_rows[g], tm)
        first = tiles_per_seq - max(state_rows[g] // tm, 1)

        @pl.when(ti >= first)
        def _():
            for j in range(2 * HEADS):
                st[pl.ds(j, rows, stride=2 * HEADS), :] = kv_stash[g, j, tm - rows:, :]

    for g in reversed(range(N_GROUPS)):
        c0 = g * GROUP_COLS
        q = head_rms(proj(c0, GROUP_COLS), qw_ref[:, c0:c0 + GROUP_COLS] * QK_SCALE)
        kn = head_rms(proj(QKV_COLS + c0, GROUP_COLS), kw_ref[:, c0:c0 + GROUP_COLS])
        v = proj(2 * QKV_COLS + c0, GROUP_COLS)
        put_state(g, kn, v)
        if not deinterleave:
            put_tiles(q_refs[g], 0, q)
            put_tiles(q_refs[g], HEADS, jnp.zeros_like(q))
            continue
        put(q_refs[g], g, q, 0)
        put(k_refs[g], g, kn, 1, [kv_stash.at[g, hh] for hh in range(HEADS)])
        put(v_refs[g], g, v, 2, [kv_stash.at[g, HEADS + hh] for hh in range(HEADS)])

    for g in range(N_GROUPS):
        if state_rows[g] is not None:
            flush_state(g)


def _inproj_call(x, mod, n1w, w_in, qw, kw, *, seq, tm, deinterleave, act_dtype,
                 layer=0, depth=1, prev_states=None):
    t, d_model = x.shape
    batch = t // seq
    tiles_per_seq = seq // tm

    def tok(cols, dtype):
        return (jax.ShapeDtypeStruct((t, cols), dtype), pl.BlockSpec((tm, cols), lambda i: (i, 0)))

    def grp(g, dtype):
        d = DILATIONS[g]
        if d == 1:
            return tok(GROUP_COLS, dtype)
        assert tm % d == 0 and (tm // d) % 16 == 0
        return (jax.ShapeDtypeStruct((batch, d, seq // d, GROUP_COLS), dtype),
                pl.BlockSpec((1, d, tm // d, GROUP_COLS),
                             lambda i: (i // tiles_per_seq, 0, i % tiles_per_seq, 0)))

    tile_rows = 2 * HEADS
    outs = []
    for g in range(N_GROUPS):
        if deinterleave:
            outs += [grp(g, act_dtype)] * 3
        else:
            assert t == tm
            outs.append((jax.ShapeDtypeStruct((t * tile_rows, LANES), F32),
                         pl.BlockSpec((t * tile_rows, LANES), lambda i: (0, 0))))

    state_rows = []
    for g in range(N_GROUPS):
        if not deinterleave:
            state_rows.append(None)
            outs.append((jax.ShapeDtypeStruct((depth * t * tile_rows, LANES), F32),
                         pl.BlockSpec((t * tile_rows, LANES), lambda i: (layer, 0))))
            continue
        rows = min(WINDOWS[g], seq)
        state_rows.append(rows)
        blk_rows = min(rows, tm)
        assert rows % blk_rows == 0
        nblk = rows // blk_rows
        shape = jax.ShapeDtypeStruct((depth * batch * rows * 2 * HEADS, LANES), F32)
        spec = pl.BlockSpec(
            (blk_rows * 2 * HEADS, LANES),
            lambda i, nblk=nblk: ((layer * batch + i // tiles_per_seq) * nblk
                                  + jnp.maximum(i % tiles_per_seq - (tiles_per_seq - nblk), 0), 0))
        outs.append((shape, spec))

    aliased = list(prev_states) if prev_states is not None else []
    n_in = 6
    w_qkv = pl.BlockSpec((d_model, 3 * QKV_COLS), lambda i: (0, 0),
                         pipeline_mode=pl.Buffered(1))
    kern = functools.partial(_inproj_kernel, tm=tm, d_model=d_model, dils=DILATIONS,
                             state_rows=tuple(state_rows), tiles_per_seq=tiles_per_seq,
                             deinterleave=deinterleave, n_aliased=len(aliased))
    return pl.pallas_call(
        kern,
        grid=(t // tm,),
        in_specs=[pl.BlockSpec((tm, d_model), lambda i: (i, 0)),
                  pl.BlockSpec((1,) + mod.shape[1:], lambda i: (i // tiles_per_seq, 0, 0)),
                  _resident_layer(n1w.shape, layer),
                  w_qkv,
                  _resident_layer(qw.shape, layer), _resident_layer(kw.shape, layer)]
                 + [pl.BlockSpec(memory_space=pl.ANY)] * len(aliased),
        out_specs=[o[1] for o in outs],
        out_shape=[o[0] for o in outs],
        input_output_aliases={n_in + g: len(outs) - N_GROUPS + g for g in range(len(aliased))},
        scratch_shapes=[pltpu.VMEM((tm, d_model), BF16),
                        pltpu.VMEM((N_GROUPS - 1, HEADS, tm, LANES), F32),
                        pltpu.VMEM((3, HEADS, tm, LANES), F32),
                        pltpu.VMEM((N_GROUPS, 2 * HEADS, tm, LANES), F32)],
        compiler_params=_params(1),
        name="inproj_deint" if deinterleave else "inproj_tok",
    )(x, mod, n1w, w_in, qw, kw, *aliased)


def _alibi_slopes():
    h = np.arange(1, N_GROUPS * HEADS + 1, dtype=np.float32)
    s = np.power(np.float32(2.0), -8.0 * h / (N_GROUPS * HEADS)).astype(np.float32)
    return (s.astype(np.float64) * LOG2E).astype(np.float32).reshape(N_GROUPS, HEADS)


def _band_bias():
    qi = np.arange(N_STEPS)[:, None]
    ki = np.arange(2 * N_STEPS)[None, :]
    dist = N_STEPS + qi - ki
    valid = (dist >= 0) & (dist <= N_STEPS)
    slopes = _alibi_slopes()
    out = np.empty((N_GROUPS, HEADS, 2, N_STEPS, 2 * N_STEPS), np.float32)
    for g in range(N_GROUPS):
        for hh in range(HEADS):
            b = np.where(valid, -slopes[g, hh] * (dist * DILATIONS[g]).astype(np.float32),
                         np.float32(NEG_INF)).astype(np.float32)
            out[g, hh, 1] = b
            first = b.copy()
            first[:, :N_STEPS] = NEG_INF
            out[g, hh, 0] = first
    return out


def _attn_block(q, k2, v2, bias):
    s = _dot_nt(q, k2) + bias
    m = jnp.max(s, axis=-1, keepdims=True)
    p = jnp.exp2(s - m)
    l = jnp.sum(p, axis=-1, keepdims=True)
    return _dot(p.astype(v2.dtype), v2), m, l


def _attn_kernel(q0, k0, v0, q1, k1, v1, q2, k2, v2, bias_ref, *refs, seq, n_cast):
    cast_src, refs = refs[:n_cast], refs[n_cast:]
    o_ref, cast_dst = refs[0], refs[1:1 + n_cast]
    o_scr, m_scr, l_scr, stage_scr = refs[1 + n_cast:]
    blk = N_STEPS
    s1 = RELAYOUT_STRIDE

    for src, dst in zip(cast_src, cast_dst):
        dst[...] = src[...].astype(dst.dtype)

    def run_group(g, qr, kr, vr):
        d = DILATIONS[g]
        nblk = seq // d // blk

        def one_block(r, i):
            own = pl.multiple_of(i * blk, blk)
            prev = pl.multiple_of(jnp.maximum(i - 1, 0) * blk, blk)
            if d == 1:
                ld = lambda ref, start: ref[pl.ds(start, blk), :]
            else:
                ld = lambda ref, start: ref[0, r, pl.ds(start, blk), :]
            k2_ = jnp.concatenate([ld(kr, prev), ld(kr, own)], axis=0)
            v2_ = jnp.concatenate([ld(vr, prev), ld(vr, own)], axis=0)
            o, m, l = _attn_block(ld(qr, own), k2_, v2_, bias_ref[g, 0, jnp.minimum(i, 1)])
            m = jnp.broadcast_to(m, (blk, LANES))
            l = jnp.broadcast_to(l, (blk, LANES))
            if 1 < d <= s1:
                rows = pl.ds(i * (blk * d) + r, blk, stride=d)
                o_scr[g - 1, rows, :] = o
                m_scr[g - 1, rows, :] = m
                l_scr[g - 1, rows, :] = l
                return
            if d > s1:
                s2 = d // s1
                rows = pl.ds((r % s1) * (seq // s1) + i * (blk * s2) + r // s1, blk, stride=s2)
                stage_scr[0, rows, :] = o
                stage_scr[1, rows, :] = m
                stage_scr[2, rows, :] = l
                return
            rows = pl.ds(own, blk)
            ms = [m, m_scr[0, rows, :], m_scr[1, rows, :]]
            mx = jnp.maximum(jnp.maximum(ms[0], ms[1]), ms[2])
            ws = [jnp.exp2(x - mx) for x in ms]
            num = ws[0] * o + ws[1] * o_scr[0, rows, :] + ws[2] * o_scr[1, rows, :]
            den = ws[0] * l + ws[1] * l_scr[0, rows, :] + ws[2] * l_scr[1, rows, :]
            o_ref[rows, :] = (num / den).astype(o_ref.dtype)

        ur = min(ATTN_UNROLL, d)
        ui = ATTN_UNROLL // ur
        assert d % ur == 0 and nblk % ui == 0

        def body(it, carry):
            i0 = (it // (d // ur)) * ui
            r0 = (it % (d // ur)) * ur
            for a in range(ui):
                for b in range(ur):
                    one_block(r0 + b, i0 + a)
            return carry

        jax.lax.fori_loop(0, (d // ur) * (nblk // ui), body, 0)

    run_group(2, q2, k2, v2)
    run_group(1, q1, k1, v1)
    g = N_GROUPS - 1
    assert DILATIONS[g] > s1 and all(d <= s1 for d in DILATIONS[:g])
    per_stream = seq // s1
    for a, dst in enumerate((o_scr, m_scr, l_scr)):
        for r1 in range(s1):
            for c in range(per_stream // blk):
                dst[g - 1, pl.ds(s1 * c * blk + r1, blk, stride=s1), :] = (
                    stage_scr[a, r1 * per_stream + c * blk:r1 * per_stream + (c + 1) * blk, :])
    run_group(0, q0, k0, v0)


def _attn_call(qkv, bias, casts, *, batch, seq):
    cast_in, cast_out, cast_shape = _cast_specs(casts, batch * HEADS, lambda b, h: b * HEADS + h)
    in_specs = []
    for g in range(N_GROUPS):
        d = DILATIONS[g]
        if d == 1:
            spec = pl.BlockSpec((seq, HEAD_DIM), lambda b, h: (b, h))
        else:
            spec = pl.BlockSpec((1, d, seq // d, HEAD_DIM), lambda b, h: (b, 0, 0, h))
        in_specs += [spec] * 3
    in_specs.append(pl.BlockSpec((N_GROUPS, 1, 2, N_STEPS, 2 * N_STEPS), lambda b, h: (0, h, 0, 0, 0)))
    return pl.pallas_call(
        functools.partial(_attn_kernel, seq=seq, n_cast=len(casts)),
        grid=(batch, HEADS),
        in_specs=in_specs + cast_in,
        out_specs=[pl.BlockSpec((seq, HEAD_DIM), lambda b, h: (b, h))] + cast_out,
        out_shape=[jax.ShapeDtypeStruct((batch * seq, GROUP_COLS), BF16)] + cast_shape,
        scratch_shapes=[pltpu.VMEM((N_GROUPS - 1, seq, LANES), F32)] * 3
                       + [pltpu.VMEM((3, seq, LANES), F32)],
        compiler_params=_params(2),
        name="attn_prompt",
    )(*qkv, bias, *[w for w, _ in casts])


def _attn_sample_kernel(q0, q1, q2, n0, n1, n2, c0, c1, c2, slope_ref, o_ref, *, bb):
    q_refs, n_refs, c_refs = (q0, q1, q2), (n0, n1, n2), (c0, c1, c2)
    half = N_STEPS // 2
    tile = (2 * HEADS, HEAD_DIM)
    top = jax.lax.broadcasted_iota(jnp.int32, tile, 0) < HEADS
    steps_a = (N_STEPS - jax.lax.broadcasted_iota(jnp.int32, (half, 1, 1), 0)).astype(F32)
    back = jnp.where(top[:, 0:1], steps_a, steps_a - float(half))

    def both_halves(t):
        return jnp.where(top, t, pltpu.roll(t, HEADS, axis=0))

    for b in range(bb):
        outs, lses = [], []
        for g in range(N_GROUPS):
            q8, kv8 = q_refs[g][b], n_refs[g][b]
            xa, xb = c_refs[g][b, 0:half], c_refs[g][b, half:N_STEPS]
            slope = both_halves(slope_ref[g])[:, 0:1]
            bias = (back * (-float(DILATIONS[g]))) * slope
            prod = jnp.where(top, xa * q8, pltpu.roll(xb * q8, HEADS, axis=1))
            s = jnp.sum(prod, axis=-1, keepdims=True) + bias
            s_self = both_halves(jnp.broadcast_to(jnp.sum(kv8 * q8, axis=-1, keepdims=True), tile))
            m = jnp.broadcast_to(jnp.max(s, axis=0), tile)
            m = jnp.maximum(jnp.maximum(m, pltpu.roll(m, HEADS, axis=0)), s_self)
            p = jnp.exp2(s - m[:, 0:1])
            p_self = jnp.exp2(s_self - m)
            l = jnp.broadcast_to(jnp.sum(p, axis=0), tile)
            l = l + pltpu.roll(l, HEADS, axis=0) + p_self
            p_b = jnp.broadcast_to(p, xb.shape)
            acc = jnp.sum(pltpu.roll(p_b, HEADS, axis=1) * xa + p_b * xb, axis=0) + p_self * kv8
            outs.append(acc / l)
            lses.append(m + jnp.log(l) * LOG2E)
        mx = jnp.maximum(jnp.maximum(lses[0], lses[1]), lses[2])
        ws = [jnp.exp2(v - mx) for v in lses]
        num = ws[0] * outs[0] + ws[1] * outs[1] + ws[2] * outs[2]
        o8 = num / (ws[0] + ws[1] + ws[2])
        for hh in range(HEADS):
            o_ref[b:b + 1, hh * HEAD_DIM:(hh + 1) * HEAD_DIM] = o8[HEADS + hh:HEADS + hh + 1, :]


def _attn_sample_call(q_tiles, new_tiles, caches, layer, slopes8):
    bd = q_tiles[0].shape[0]
    bb = SAMPLE_ATTN_SEQS
    nb = bd // bb
    tile = (2 * HEADS, HEAD_DIM)
    in_specs = [pl.BlockSpec((bb,) + tile, lambda i: (i, 0, 0))] * 3
    in_specs += [pl.BlockSpec((bb,) + tile, lambda i: (layer * nb + i, 0, 0))] * 3
    in_specs += [pl.BlockSpec((None, bb, N_STEPS) + tile, lambda i: (layer, i, 0, 0, 0))] * 3
    in_specs.append(pl.BlockSpec((N_GROUPS,) + tile, lambda i: (0, 0, 0)))
    return pl.pallas_call(
        functools.partial(_attn_sample_kernel, bb=bb),
        grid=(nb,),
        in_specs=in_specs,
        out_specs=pl.BlockSpec((bb, GROUP_COLS), lambda i: (i, 0)),
        out_shape=jax.ShapeDtypeStruct((bd, GROUP_COLS), F32),
        compiler_params=_params(1),
        name="attn_sample",
    )(*q_tiles, *new_tiles, *caches, slopes8)


def _post_math(x, mod, ya, w, ws_ref, bs_ref, mix_scr, *, chunk_rows, d_model, d_ff):
    g1 = mod[:, 2 * d_model:3 * d_model]
    g2 = mod[:, 5 * d_model:6 * d_model]

    h = _modulated_norm(x, w["n1w"][...], mod, 0, d_model).astype(BF16)
    sw = SGU_WIDTH
    wg1, wg2 = w["wg1"], w["wg2"]
    u = _gelu(_dot(h, wg1[:, 0:sw]))
    vs = _gelu(_dot(h, wg1[:, sw:2 * sw]))
    mu = jnp.mean(vs, axis=-1, keepdims=True)
    vc = vs - mu
    var = jnp.mean(vc * vc, axis=-1, keepdims=True)
    vs = vc * jax.lax.rsqrt(var + EPS) * w["lnw"][...] + w["lnb"][...]
    half = d_model // 2
    ga = jnp.concatenate([_sigmoid(_dot(h, wg1[:, 2 * sw:2 * sw + half])),
                          _sigmoid(_dot(h, wg2[:, 0:half]))], axis=-1)
    gb = _sigmoid(_dot(h, wg2[:, half:half + d_model]))

    if chunk_rows is not None:
        row = jax.lax.broadcasted_iota(jnp.int32, (CHUNK, CHUNK), 0)
        col = jax.lax.broadcasted_iota(jnp.int32, (CHUNK, CHUNK), 1)
        tril = col <= row
        lane = jax.lax.broadcasted_iota(jnp.int32, (CHUNK, LANES), 1)
        low = lane < SGU_GROUP_DIM
        wms = [jnp.where(tril, ws_ref[gi], 0.0).astype(BF16) for gi in range(SGU_GROUPS)]
        vs_b = vs.astype(BF16)
        for c in range(chunk_rows // CHUNK):
            rows = slice(c * CHUNK, (c + 1) * CHUNK)
            for j in range(SGU_WIDTH // LANES):
                cols = slice(j * LANES, (j + 1) * LANES)
                v = vs_b[rows, cols]
                mix_scr[rows, cols] = jnp.where(low, _dot(wms[2 * j], v), _dot(wms[2 * j + 1], v))
        mixed = mix_scr[...] + jnp.concatenate([bs_ref[...]] * (chunk_rows // CHUNK), axis=0)
    else:
        mixed = vs * ws_ref[...] + bs_ref[...]

    ys = (u * mixed).astype(BF16)
    xa = _dot(ya.astype(BF16), w["wpa"][...])
    xs = _dot(ys, w["wps"][...])
    merged = (ga * xa + gb * xs).astype(BF16)
    x1 = x + g1 * _dot(merged, w["wo"][...])
    h2 = _modulated_norm(x1, w["n2w"][...], mod, 1, d_model).astype(BF16)
    ffn = None
    for c0, c1 in ((0, FFN_SPLIT), (FFN_SPLIT, d_ff)):
        a = _dot(h2, w["wfi"][:, c0:c1])
        b = _dot(h2, w["wfi"][:, d_ff + c0:d_ff + c1])
        act = (a * _sigmoid(a) * b).astype(BF16)
        part = _dot(act, w["wfo"][c0:c1, :])
        ffn = part if ffn is None else ffn + part
    return x1 + g2 * ffn, vs


_POST_WEIGHTS = ("n1w", "wg1", "wg2", "lnw", "lnb", "wpa", "wps", "wo", "n2w", "wfi", "wfo")


def _post_kernel(x_ref, mod_ref, ya_ref, ws_ref, bs_ref, xs_ref, mods_ref, yas_ref, wss_ref, bss_ref,
                 *refs, tm, d_model, d_ff):
    n_w = len(_POST_WEIGHTS)
    w = dict(zip(_POST_WEIGHTS, refs[:n_w]))
    o_ref, os_ref, vs_ref, mix_scr = refs[n_w:]
    o_ref[...], _ = _post_math(x_ref[...], mod_ref[0], ya_ref[...], w, ws_ref, bs_ref, mix_scr,
                               chunk_rows=tm, d_model=d_model, d_ff=d_ff)

    @pl.when(pl.program_id(0) == pl.num_programs(0) - 1)
    def _():
        os_ref[...], vs_ref[...] = _post_math(xs_ref[...], mods_ref[0], yas_ref[...], w, wss_ref,
                                              bss_ref, None, chunk_rows=None,
                                              d_model=d_model, d_ff=d_ff)


def _post_call(x, mod, ya, ws, bs, xs, mod_s, ya_s, ws_s, bs_s,
               n1w, w_in, lnw, lnb, wpa, wps, wo, n2w, wfi, wfo, *, seq, tm, layer):
    t, d_model = x.shape
    ts = xs.shape[0]
    d_ff = wfo.shape[0]
    tiles_per_seq = seq // tm
    gblk = QKV_COLS
    assert 2 * SGU_WIDTH + d_model // 2 == gblk and w_in.shape[1] == 5 * gblk

    def w_gate(j):
        return pl.BlockSpec((d_model, gblk), lambda i: (0, 3 + j), pipeline_mode=pl.Buffered(1))

    def tok(cols):
        return pl.BlockSpec((tm, cols), lambda i: (i, 0))

    weight_specs = [_resident_layer(n1w.shape, layer), w_gate(0), w_gate(1),
                    _resident_layer(lnw.shape, layer), _resident_layer(lnb.shape, layer),
                    _resident_layer(wpa.shape, layer), _resident_layer(wps.shape, layer),
                    _resident_layer(wo.shape, layer), _resident_layer(n2w.shape, layer),
                    _resident(wfi.shape), _resident(wfo.shape)]
    assert len(weight_specs) == len(_POST_WEIGHTS)
    kern = functools.partial(_post_kernel, tm=tm, d_model=d_model, d_ff=d_ff)
    return pl.pallas_call(
        kern,
        grid=(t // tm,),
        in_specs=[tok(d_model),
                  pl.BlockSpec((1,) + mod.shape[1:], lambda i: (i // tiles_per_seq, 0, 0)),
                  tok(GROUP_COLS), _resident(ws.shape), _resident(bs.shape),
                  _resident(xs.shape), _resident(mod_s.shape), _resident(ya_s.shape),
                  _resident(ws_s.shape), _resident(bs_s.shape)] + weight_specs,
        out_specs=[tok(d_model),
                   pl.BlockSpec((ts, d_model), lambda i: (0, 0)),
                   pl.BlockSpec((ts, SGU_WIDTH), lambda i: (0, 0))],
        out_shape=[jax.ShapeDtypeStruct((t, d_model), F32),
                   jax.ShapeDtypeStruct((ts, d_model), F32),
                   jax.ShapeDtypeStruct((ts, SGU_WIDTH), F32)],
        scratch_shapes=[pltpu.VMEM((tm, SGU_WIDTH), F32)],
        compiler_params=_params(1),
        name="post",
    )(x, mod, ya, ws, bs, xs, mod_s, ya_s, ws_s, bs_s,
      n1w, w_in, w_in, lnw, lnb, wpa, wps, wo, n2w, wfi, wfo)


def kernel(x_prompt, x_sample, cache_kv_w128, cache_kv_w512, cache_kv_w2048, c_prompt, c_sample,
           w_ada, b_ada, norm1_w, w_in, q_norm_w, k_norm_w, sgu_ln_w, sgu_ln_b, w_spatial, b_spatial,
           w_proj_att, w_proj_sgu, w_out, norm2_w, w_ffn_in, w_ffn_out):
    batch, seq, d_model = x_prompt.shape
    bd, dec_seq, _ = x_sample.shape
    depth = w_in.shape[0]
    assert dec_seq == 1 and seq % (N_STEPS * DILATIONS[-1]) == 0
    caches_in = (cache_kv_w128, cache_kv_w512, cache_kv_w2048)
    for g in range(N_GROUPS):
        assert caches_in[g].shape[2] == WINDOWS[g]

    n_c = batch + bd
    c_all = jnp.concatenate([c_prompt, c_sample], axis=0)
    c_all = jnp.pad(c_all, ((0, -n_c % 16), (0, 0)))
    mod, w_in_l = _ada_call(c_all, w_ada, b_ada, [(w_in, 0)])

    band_bias = jnp.asarray(_band_bias())
    tile = (2 * HEADS, HEAD_DIM)
    slopes8 = np.zeros((N_GROUPS,) + tile, np.float32)
    slopes8[:, :HEADS, :] = _alibi_slopes()[:, :, None]
    slopes8 = jnp.asarray(slopes8)
    caches = [caches_in[g].reshape((depth, bd, N_STEPS, DILATIONS[g] * tile[0], tile[1]))
              for g in range(N_GROUPS)]

    xp = x_prompt.reshape(batch * seq, d_model)
    xs = x_sample.reshape(bd, d_model)
    p_states = s_states = None
    s_v = []
    wpa, wps, wo = w_proj_att.astype(BF16), w_proj_sgu.astype(BF16), w_out.astype(BF16)
    qw = jnp.tile(q_norm_w[:, :, None, :], (1, 1, HEADS, 1)).reshape(depth, 1, QKV_COLS)
    kw = jnp.tile(k_norm_w[:, :, None, :], (1, 1, HEADS, 1)).reshape(depth, 1, QKV_COLS)
    lnw, lnb = sgu_ln_w.reshape(depth, 1, -1), sgu_ln_b.reshape(depth, 1, -1)
    n1w, n2w = norm1_w.reshape(depth, 1, -1), norm2_w.reshape(depth, 1, -1)
    for l in range(depth):
        mod_p = mod[l, :batch].reshape(batch, 1, -1)
        mod_s = mod[l, batch:n_c].reshape(1, bd, -1)
        bs_p = jnp.repeat(b_spatial[l].T, SGU_GROUP_DIM, axis=1)
        ws_s = jnp.repeat(w_spatial[l][:, 0, 0], SGU_GROUP_DIM).reshape(1, SGU_WIDTH)
        bs_s = jnp.repeat(b_spatial[l][:, 0], SGU_GROUP_DIM).reshape(1, SGU_WIDTH)

        outs = _inproj_call(xp, mod_p, n1w, w_in_l, qw, kw,
                            seq=seq, tm=INPROJ_TILE_ROWS, deinterleave=True, act_dtype=BF16,
                            layer=l, depth=depth, prev_states=p_states)
        qkv, p_states = outs[:9], outs[9:]
        casts = [(w_ffn_in, l), (w_ffn_out, l)] + ([(w_in, l + 1)] if l + 1 < depth else [])
        ya, wfi, wfo, *w_in_next = _attn_call(qkv, band_bias, casts, batch=batch, seq=seq)

        outs = _inproj_call(xs, mod_s, n1w, w_in_l, qw, kw,
                            seq=bd, tm=bd, deinterleave=False, act_dtype=F32,
                            layer=l, depth=depth, prev_states=s_states)
        q_tiles, s_states = [o.reshape((bd,) + tile) for o in outs[:3]], outs[3:]
        new_tiles = [s.reshape((depth * bd,) + tile) for s in s_states]
        ya_s = _attn_sample_call(q_tiles, new_tiles, caches, l, slopes8)

        xp, xs, vs = _post_call(xp, mod_p, ya, w_spatial[l], bs_p, xs, mod_s, ya_s, ws_s, bs_s,
                                n1w, w_in_l, lnw, lnb, wpa, wps, wo, n2w, wfi, wfo,
                                seq=seq, tm=POST_TILE_ROWS, layer=l)
        s_v.append(vs.reshape(bd, 1, SGU_WIDTH))
        if w_in_next:
            (w_in_l,) = w_in_next

    p_states =[s.reshape(depth, batch, -1, 2, HEADS, HEAD_DIM) for s in p_states]
    s_states = [s.reshape(depth, bd, 1, 2, HEADS, HEAD_DIM) for s in s_states]
    return (xp.reshape(batch, seq, d_model), xs.reshape(bd, 1, d_model),
            p_states[0], p_states[1], p_states[2], s_states[0], s_states[1], s_states[2],
            jnp.stack(s_v))
```
